```python
import math
import jax
import jax.numpy as jnp
from jax import lax
import numpy as np

D_MODEL = 1024
BATCH = 32
SEQ = 2048
DEPTH = 4

N_MIXERS = 4
HEAD_DIM = 64
GROUP_WIDTH = D_MODEL // N_MIXERS
GROUP_HEADS = GROUP_WIDTH // HEAD_DIM
DIFF_DIM = HEAD_DIM // 2
ROPE_THETA = 10000.0
Q_BLOCK = 128
GDN_CHUNK = 64
CONV_K = 4
D_FF = ((8 * D_MODEL // 3 + 255) // 256) * 256
RWKV_DECAY_LORA = 64
RWKV_A_LORA = 64
RWKV_V_LORA = 32
RWKV_G_LORA = 160
N_MOD = 9
EPS = 1e-6
LNX_EPS = 64e-5

SPLIT_SIZES = (
    3 * GROUP_WIDTH,
    3 * GROUP_WIDTH,
    3 * GROUP_WIDTH,
    GROUP_HEADS, GROUP_HEADS, GROUP_WIDTH,
    GROUP_WIDTH, GROUP_WIDTH, GROUP_WIDTH,
    RWKV_DECAY_LORA, RWKV_A_LORA, RWKV_G_LORA,
)
RWKV_START = 9 * GROUP_WIDTH + 2 * GROUP_HEADS + GROUP_WIDTH

kernel_name = "hybrid_parallel_heads_diff_stick_gdn_rwkv7_macaron_adaln"


def rms_norm(x, gain, eps=EPS):
    xf = x.astype(jnp.float32)
    y = xf * lax.rsqrt(jnp.mean(xf * xf, axis=-1, keepdims=True) + eps)
    return (y * gain.astype(jnp.float32)).astype(x.dtype)


def l2_norm(x, eps=EPS):
    xf = x.astype(jnp.float32)
    return (xf * lax.rsqrt(jnp.sum(xf * xf, axis=-1, keepdims=True) + eps)).astype(x.dtype)


def modulate(h, gain, shift, scale):
    return rms_norm(h, gain) * (1 + scale) + shift


def swiglu(h, w13, w2):
    gate, up = jnp.split(h @ w13, 2, axis=-1)
    return (jax.nn.silu(gate) * up) @ w2


def split_cols(y, sizes):
    out, start = [], 0
    for s in sizes:
        out.append(y[..., start:start + s])
        start += s
    return out


def rope_tables(seq, dim):
    inv = ROPE_THETA ** (-jnp.arange(0, dim, 2, dtype=jnp.float32) / dim)
    ang = jnp.arange(seq, dtype=jnp.float32)[:, None] * inv[None, :]
    return jnp.cos(ang), jnp.sin(ang)


def apply_rope(x, cos, sin):
    extra = (1,) * (x.ndim - 3)
    cos = cos.reshape(cos.shape[:1] + extra + cos.shape[1:]).astype(x.dtype)
    sin = sin.reshape(sin.shape[:1] + extra + sin.shape[1:]).astype(x.dtype)
    x1, x2 = jnp.split(x, 2, axis=-1)
    return jnp.concatenate([x1 * cos - x2 * sin, x2 * cos + x1 * sin], axis=-1)


def token_shift_mix(y, mu):
    prev = jnp.pad(y, ((0, 0), (1, 0), (0, 0)))[:, :-1]
    return y + (prev - y) * mu


def causal_depthwise_conv(x, w):
    K, C = w.shape
    return lax.conv_general_dilated(
        x, w[:, None, :].astype(x.dtype), window_strides=(1,), padding=[(K - 1, 0)],
        dimension_numbers=("NWC", "WIO", "NWC"), feature_group_count=C)


def mixer_diff_attention(qkv, qk_g, lam_vec, subln_g, lambda_init, cos, sin):
    B_, S_ = qkv.shape[:2]
    q, k, v = jnp.split(qkv, 3, axis=-1)
    q = q.reshape(B_, S_, GROUP_HEADS, 2, DIFF_DIM)
    k = k.reshape(B_, S_, GROUP_HEADS, 2, DIFF_DIM)
    v = v.reshape(B_, S_, GROUP_HEADS, HEAD_DIM)
    q = apply_rope(rms_norm(q, qk_g[0]), cos, sin)
    k = apply_rope(rms_norm(k, qk_g[1]), cos, sin)
    lf = lam_vec.astype(jnp.float32)
    lam = jnp.exp(jnp.sum(lf[0] * lf[1])) - jnp.exp(jnp.sum(lf[2] * lf[3])) + lambda_init
    scale = DIFF_DIM ** -0.5
    outs = []
    for i in range(S_ // Q_BLOCK):
        q0, q1 = i * Q_BLOCK, (i + 1) * Q_BLOCK
        s = jnp.einsum("bqhcd,bkhcd->bhcqk", q[:, q0:q1], k[:, :q1]).astype(jnp.float32) * scale
        mask = jnp.arange(q1)[None, :] <= (q0 + jnp.arange(Q_BLOCK))[:, None]
        p = jax.nn.softmax(jnp.where(mask, s, -jnp.inf), axis=-1)
        w = p[:, :, 0] - lam * p[:, :, 1]
        outs.append(jnp.einsum("bhqk,bkhd->bqhd", w.astype(v.dtype), v[:, :q1]))
    o = jnp.concatenate(outs, axis=1)
    o = rms_norm(o, subln_g) * (1.0 - lambda_init)
    return o.reshape(B_, S_, GROUP_WIDTH)


def mixer_stick_breaking(qkv):
    B_, S_ = qkv.shape[:2]
    q, k, v = [t.reshape(B_, S_, GROUP_HEADS, HEAD_DIM) for t in jnp.split(qkv, 3, axis=-1)]
    scale = HEAD_DIM ** -0.5
    outs = []
    for i in range(S_ // Q_BLOCK):
        q0, q1 = i * Q_BLOCK, (i + 1) * Q_BLOCK
        z = jnp.einsum("bqhd,bkhd->bhqk", q[:, q0:q1], k[:, :q1]).astype(jnp.float32) * scale
        strict = jnp.arange(q1)[None, :] < (q0 + jnp.arange(Q_BLOCK))[:, None]
        log_keep = jnp.where(strict, jax.nn.log_sigmoid(-z), 0.0)
        between = lax.cumsum(log_keep, axis=3, reverse=True) - log_keep
        a = jnp.where(strict, jnp.exp(jax.nn.log_sigmoid(z) + between), 0.0)
        outs.append(jnp.einsum("bhqk,bkhd->bqhd", a.astype(v.dtype), v[:, :q1]))
    return jnp.concatenate(outs, axis=1).reshape(B_, S_, GROUP_WIDTH)


def gated_delta_rule(q, k, v, g, beta):
    B_, S_, H, Dk = k.shape
    Dv = v.shape[-1]
    C = GDN_CHUNK
    n = S_ // C

    def chunks(t):
        t = t.astype(jnp.float32).reshape((B_, n, C, H) + t.shape[3:])
        return jnp.moveaxis(t, 3, 1)

    q, k, v, beta = chunks(q), chunks(k), chunks(v), chunks(beta)
    g = jnp.cumsum(chunks(g), axis=-1)
    kb = k * beta[..., None]
    vb = v * beta[..., None]
    incl = jnp.tril(jnp.ones((C, C), bool))
    strict = jnp.tril(jnp.ones((C, C), bool), -1)
    diff = g[..., :, None] - g[..., None, :]
    decay = jnp.where(incl, jnp.exp(jnp.where(incl, diff, 0.0)), 0.0)
    A = jnp.where(strict, jnp.einsum("bhnid,bhnjd->bhnij", kb, k) * decay, 0.0)
    rhs = jnp.concatenate([vb, kb * jnp.exp(g)[..., None]], axis=-1)
    sol = lax.linalg.triangular_solve(A + jnp.eye(C, dtype=A.dtype), rhs, left_side=True,
                                      lower=True, unit_diagonal=True)
    u, w = sol[..., :Dv], sol[..., Dv:]
    qk = jnp.einsum("bhnid,bhnjd->bhnij", q, k) * decay
    q_dec = q * jnp.exp(g)[..., None]
    k_dec = k * jnp.exp(g[..., -1:] - g)[..., None]
    g_last = jnp.exp(g[..., -1])
    xs = (jnp.moveaxis(u, 2, 0), jnp.moveaxis(w, 2, 0), jnp.moveaxis(qk, 2, 0),
          jnp.moveaxis(q_dec, 2, 0), jnp.moveaxis(k_dec, 2, 0), jnp.moveaxis(g_last, 2, 0))

    def step(S, inp):
        u_i, w_i, qk_i, qd_i, kd_i, gl_i = inp
        v_new = u_i - jnp.einsum("bhck,bhkv->bhcv", w_i, S)
        o = jnp.einsum("bhck,bhkv->bhcv", qd_i, S) + jnp.einsum("bhij,bhjv->bhiv", qk_i, v_new)
        S = S * gl_i[..., None, None] + jnp.einsum("bhck,bhcv->bhkv", kd_i, v_new)
        return S, o

    S0 = jnp.zeros((B_, H, Dk, Dv), jnp.float32)
    _, o = lax.scan(step, S0, xs)
    return jnp.transpose(o, (1, 0, 3, 2, 4)).reshape(B_, S_, H, Dv)


def mixer_gated_deltanet(qkv, beta_raw, decay_raw, gate, conv_w, a_log, dt_bias, norm_g):
    B_, S_ = qkv.shape[:2]
    qkv = jax.nn.silu(causal_depthwise_conv(qkv, conv_w))
    q, k, v = [t.reshape(B_, S_, GROUP_HEADS, HEAD_DIM) for t in jnp.split(qkv, 3, axis=-1)]
    q = l2_norm(q) * (HEAD_DIM ** -0.5)
    k = l2_norm(k)
    beta = jax.nn.sigmoid(beta_raw.astype(jnp.float32))
    g = -jnp.exp(a_log.astype(jnp.float32)) * jax.nn.softplus(
        decay_raw.astype(jnp.float32) + dt_bias.astype(jnp.float32))
    o = gated_delta_rule(q, k, v, g, beta).astype(qkv.dtype)
    o = rms_norm(o, norm_g) * jax.nn.silu(gate.reshape(B_, S_, GROUP_HEADS, HEAD_DIM))
    return o.reshape(B_, S_, GROUP_WIDTH)


def wkv7(r, w, k, v, a, b):
    B_, S_, H, N = r.shape
    xs = tuple(jnp.moveaxis(t, 1, 0) for t in (r, w, k, v, a, b))

    def step(S, inp):
        r_t, w_t, k_t, v_t, a_t, b_t = inp
        sa = jnp.einsum("bhij,bhj->bhi", S, a_t)
        S = S * w_t[:, :, None, :] + sa[..., None] * b_t[..., None, :] + v_t[..., None] * k_t[..., None, :]
        return S, jnp.einsum("bhij,bhj->bhi", S, r_t)

    _, y = lax.scan(step, jnp.zeros((B_, H, N, N), jnp.float32), xs)
    return jnp.moveaxis(y, 0, 1)


def mixer_rwkv7(rcols, v_first, w0, w_up, a0, a_up, g_up, k_k, k_a, r_k, lnx_w, lnx_b,
                v0=None, vmix_up=None):
    r, k, v, wd, ad, gd = rcols[:6]
    B_, S_ = r.shape[:2]
    dtype = r.dtype
    f32 = jnp.float32
    w_log = -jax.nn.softplus(-(w0 + jnp.tanh(wd) @ w_up).astype(f32)) - 0.5
    decay = jnp.exp(-jnp.exp(w_log))
    a = jax.nn.sigmoid((a0 + ad @ a_up).astype(f32))
    g = (jax.nn.sigmoid(gd) @ g_up).astype(f32)
    if v_first is None:
        v_first = v
    else:
        v = v + (v_first - v) * jax.nn.sigmoid(v0 + rcols[6] @ vmix_up)

    def heads(t):
        return t.reshape(B_, S_, GROUP_HEADS, HEAD_DIM)

    kk = heads(l2_norm(heads(k * k_k)).astype(f32).reshape(B_, S_, GROUP_WIDTH))
    k = k.astype(f32) * (1 + (a - 1) * k_a.astype(f32))
    r_h, k_h, v_h, a_h = heads(r.astype(f32)), heads(k), heads(v.astype(f32)), heads(a)
    y = wkv7(r_h, heads(decay), k_h, v_h, -kk, kk * a_h)
    mean = jnp.mean(y, axis=-1, keepdims=True)
    var = jnp.mean(jnp.square(y - mean), axis=-1, keepdims=True)
    lnx_w_h = lnx_w.astype(f32).reshape(GROUP_HEADS, HEAD_DIM)
    lnx_b_h = lnx_b.astype(f32).reshape(GROUP_HEADS, HEAD_DIM)
    y = (y - mean) * lax.rsqrt(var + LNX_EPS) * lnx_w_h + lnx_b_h
    y = y + jnp.sum(r_h * k_h * r_k.astype(f32), axis=-1, keepdims=True) * v_h
    o = (y.reshape(B_, S_, GROUP_WIDTH) * g).astype(dtype)
    return o, v_first


def setup_inputs(seed: int = 0) -> dict:
    key = jax.random.key(seed)
    keys = jax.random.split(key, 64)
    counter = [0]

    def nk():
        counter[0] += 1
        return keys[counter[0] - 1]

    def nrm(shape, scale):
        return jax.random.normal(nk(), shape, jnp.float32) * scale

    def unif(shape, lo, hi):
        return jax.random.uniform(nk(), shape, jnp.float32, lo, hi)

    L, D, GW, H, F = DEPTH, D_MODEL, GROUP_WIDTH, GROUP_HEADS, D_FF
    P = sum(SPLIT_SIZES)
    P_RWKV = P - RWKV_START
    dt = jnp.exp(unif((L, H), math.log(1e-3), math.log(1e-1)))
    return {
        "x": nrm((BATCH, SEQ, D), 1.0),
        "c": nrm((BATCH, D), 1.0),
        "ada_w": nrm((L, D, N_MOD * D), 0.5 * D ** -0.5),
        "ada_b": nrm((L, N_MOD * D), 0.01),
        "norm_g": 1.0 + nrm((L, 3, D), 0.02),
        "ffn_w13": nrm((L, 2, D, 2 * F), D ** -0.5),
        "ffn_w2": nrm((L, 2, F, D), F ** -0.5),
        "w_in": nrm((L, D, P), D ** -0.5),
        "w_out": nrm((L, N_MIXERS * GW, D), (N_MIXERS * GW) ** -0.5),
        "diff_qk_g": 1.0 + nrm((L, 2, DIFF_DIM), 0.02),
        "diff_lambda": nrm((L, 4, DIFF_DIM), 0.1),
        "diff_subln_g": 1.0 + nrm((L, HEAD_DIM), 0.02),
        "gdn_conv_w": nrm((L, CONV_K, 3 * GW), CONV_K ** -0.5),
        "gdn_a_log": jnp.log(unif((L, H), 1.0, 16.0)),
        "gdn_dt_bias": dt + jnp.log(-jnp.expm1(-dt)),
        "gdn_norm_g": 1.0 + nrm((L, HEAD_DIM), 0.02),
        "rwkv_mu": unif((L, P_RWKV), 0.0, 1.0),
        "rwkv_w0": unif((L, GW), -6.0, 0.0),
        "rwkv_w_up": nrm((L, RWKV_DECAY_LORA, GW), 0.1),
        "rwkv_a0": nrm((L, GW), 0.1),
        "rwkv_a_up": nrm((L, RWKV_A_LORA, GW), 0.5 * RWKV_A_LORA ** -0.5),
        "rwkv_g_up": nrm((L, RWKV_G_LORA, GW), RWKV_G_LORA ** -0.5),
        "rwkv_k_k": 0.85 + nrm((L, GW), 0.05),
        "rwkv_k_a": 1.0 + nrm((L, GW), 0.05),
        "rwkv_r_k": nrm((L, H, HEAD_DIM), 0.1),
        "rwkv_lnx_w": 1.0 + nrm((L, GW), 0.05),
        "rwkv_lnx_b": nrm((L, GW), 0.01),
        "rwkv_vmix_down": nrm((L - 1, D, RWKV_V_LORA), D ** -0.5),
        "rwkv_vmix_mu": unif((L - 1, RWKV_V_LORA), 0.0, 1.0),
        "rwkv_v0": nrm((L - 1, GW), 0.1),
        "rwkv_vmix_up": nrm((L - 1, RWKV_V_LORA, GW), 0.5 * RWKV_V_LORA ** -0.5),
    }


def reference(x, c, ada_w, ada_b, norm_g, ffn_w13, ffn_w2, w_in, w_out, diff_qk_g,
              diff_lambda, diff_subln_g, gdn_conv_w, gdn_a_log, gdn_dt_bias, gdn_norm_g,
              rwkv_mu, rwkv_w0, rwkv_w_up, rwkv_a0, rwkv_a_up, rwkv_g_up, rwkv_k_k, rwkv_k_a,
              rwkv_r_k, rwkv_lnx_w, rwkv_lnx_b, rwkv_vmix_down, rwkv_vmix_mu, rwkv_v0,
              rwkv_vmix_up):
    B_, S_, D_ = x.shape
    cos, sin = rope_tables(S_, DIFF_DIM)
    cond = jax.nn.silu(c)
    h = x
    v_first = None
    for l in range(DEPTH):
        mod = (cond @ ada_w[l] + ada_b[l]).reshape(B_, N_MOD, 1, D_)
        sh1, sc1, g1, sh2, sc2, g2, sh3, sc3, g3 = [mod[:, i] for i in range(N_MOD)]

        hn = modulate(h, norm_g[l, 0], sh1, sc1)
        h = h + 0.5 * g1 * swiglu(hn, ffn_w13[l, 0], ffn_w2[l, 0])

        hn = modulate(h, norm_g[l, 1], sh2, sc2)
        if l == 0:
            w_in_l, sizes, mu = w_in[l], SPLIT_SIZES, rwkv_mu[l]
        else:
            w_in_l = jnp.concatenate([w_in[l], rwkv_vmix_down[l - 1]], axis=1)
            sizes = SPLIT_SIZES + (RWKV_V_LORA,)
            mu = jnp.concatenate([rwkv_mu[l], rwkv_vmix_mu[l - 1]])
        proj = hn @ w_in_l
        cols = split_cols(proj, sizes)
        lambda_init = 0.8 - 0.6 * math.exp(-0.3 * l)

        o_a = mixer_diff_attention(cols[0], diff_qk_g[l], diff_lambda[l], diff_subln_g[l],
                                   lambda_init, cos, sin)
        o_b = mixer_stick_breaking(cols[1])
        o_c = mixer_gated_deltanet(cols[2], cols[3], cols[4], cols[5], gdn_conv_w[l],
                                   gdn_a_log[l], gdn_dt_bias[l], gdn_norm_g[l])
        rcols = split_cols(token_shift_mix(proj[..., RWKV_START:], mu), sizes[6:])
        if l == 0:
            o_d, v_first = mixer_rwkv7(rcols, None, rwkv_w0[l], rwkv_w_up[l], rwkv_a0[l],
                                       rwkv_a_up[l], rwkv_g_up[l], rwkv_k_k[l], rwkv_k_a[l],
                                       rwkv_r_k[l], rwkv_lnx_w[l], rwkv_lnx_b[l])
        else:
            o_d, v_first = mixer_rwkv7(rcols, v_first, rwkv_w0[l], rwkv_w_up[l], rwkv_a0[l],
                                       rwkv_a_up[l], rwkv_g_up[l], rwkv_k_k[l], rwkv_k_a[l],
                                       rwkv_r_k[l], rwkv_lnx_w[l], rwkv_lnx_b[l],
                                       rwkv_v0[l - 1], rwkv_vmix_up[l - 1])
        mixed = jnp.concatenate([o_a, o_b, o_c, o_d.astype(o_a.dtype)], axis=-1)
        h = h + g2 * (mixed @ w_out[l])

        hn = modulate(h, norm_g[l, 2], sh3, sc3)
        h = h + 0.5 * g3 * swiglu(hn, ffn_w13[l, 1], ffn_w2[l, 1])
    return h
```

```python
import functools
import math

import numpy as np
import jax
import jax.numpy as jnp
from jax import lax
from jax.experimental import pallas as pl
from jax.experimental.pallas import tpu as pltpu

F32 = jnp.float32
BF16 = jnp.bfloat16
HIGHEST = lax.Precision.HIGHEST

D_MODEL = 1024
N_HEADS = 4
HEAD_DIM = 64
GROUP_W = N_HEADS * HEAD_DIM
DIFF_DIM = HEAD_DIM // 2
D_FF = 2816
N_MOD = 9
EPS = 1e-6
LNX_EPS = 64e-5
ROPE_THETA = 10000.0
CONV_K = 4
RWKV_START = 9 * GROUP_W + 2 * N_HEADS + GROUP_W

ATTN_W = 6 * GROUP_W
GDN_W = 3 * GROUP_W + GROUP_W + 128
RWKV_W = 3 * GROUP_W + 128 + 128 + 256 + 128
PROJ_W = ATTN_W + GDN_W + RWKV_W
VMEM_LIMIT = 56 * 1024 * 1024


def _cparams(sem):
    return pltpu.CompilerParams(dimension_semantics=sem, vmem_limit_bytes=VMEM_LIMIT)


def _sigmoid(x):
    return 1.0 / (1.0 + jnp.exp(-x))


def _softplus(x):
    return jnp.maximum(x, 0.0) + jnp.log(1.0 + jnp.exp(-jnp.abs(x)))


def _group_ones(n, group):
    r = lax.broadcasted_iota(jnp.int32, (n, n), 0) // group
    c = lax.broadcasted_iota(jnp.int32, (n, n), 1) // group
    return (r == c).astype(F32)


def _group_sum(x, group):
    return jnp.dot(x, _group_ones(x.shape[-1], group), precision=HIGHEST,
                   preferred_element_type=F32)


def _modulate(x, gain, shift, scale):
    ms = jnp.mean(x * x, axis=-1, keepdims=True)
    return (x * lax.rsqrt(ms + EPS) * gain) * (1.0 + scale) + shift


def _mod_kernel(c_ref, w_ref, b_ref, o_ref):
    c = c_ref[...]
    cond = c * _sigmoid(c)
    o_ref[...] = jnp.dot(cond, w_ref[...], precision=HIGHEST,
                         preferred_element_type=F32) + b_ref[...]


def _mod_all(c, ada_w, ada_b):
    L, D, N = ada_w.shape
    B = c.shape[0]
    tn = 1024
    out = pl.pallas_call(
        _mod_kernel,
        name="adaln_mod",
        grid=(L, N // tn),
        in_specs=[
            pl.BlockSpec((B, D), lambda l, n: (0, 0)),
            pl.BlockSpec((None, D, tn), lambda l, n: (l, 0, n)),
            pl.BlockSpec((None, 1, tn), lambda l, n: (l, 0, n)),
        ],
        out_specs=pl.BlockSpec((None, B, tn), lambda l, n: (l, 0, n)),
        out_shape=jax.ShapeDtypeStruct((L, B, N), F32),
        compiler_params=_cparams(("parallel", "parallel")),
    )(c, ada_w, ada_b.reshape(L, 1, N))
    return out.reshape(L, B, N_MOD, D)


def _ffn_kernel(x_ref, mod_ref, g_ref, w1_ref, w3_ref, w2_ref, o_ref, hn_ref, acc_ref, *, mb, nf):
    f = pl.program_id(1)

    @pl.when(f == 0)
    def _():
        hn = _modulate(x_ref[...], g_ref[...], mod_ref[mb:mb + 1, :], mod_ref[mb + 1:mb + 2, :])
        hn_ref[...] = hn.astype(BF16)
        acc_ref[...] = jnp.zeros_like(acc_ref)

    hn = hn_ref[...]
    gate = jnp.dot(hn, w1_ref[...], preferred_element_type=F32)
    up = jnp.dot(hn, w3_ref[...], preferred_element_type=F32)
    act = (gate * _sigmoid(gate) * up).astype(BF16)
    acc_ref[...] += jnp.dot(act, w2_ref[...], preferred_element_type=F32)

    @pl.when(f == nf - 1)
    def _():
        o_ref[...] = x_ref[...] + 0.5 * mod_ref[mb + 2:mb + 3, :] * acc_ref[...]


def _ffn(h, mod_l, gain, w13, w2, l, j, mb, S):
    T, D = h.shape
    tm = min(1024, S)
    tf = 256
    nf = D_FF // tf
    tpb = S // tm
    return pl.pallas_call(
        functools.partial(_ffn_kernel, mb=mb, nf=nf),
        name="ffn",
        grid=(T // tm, nf),
        in_specs=[
            pl.BlockSpec((tm, D), lambda i, f: (i, 0)),
            pl.BlockSpec((None, N_MOD, D), lambda i, f: (i // tpb, 0, 0)),
            pl.BlockSpec((1, D), lambda i, f: (0, 0)),
            pl.BlockSpec((None, None, D, tf), lambda i, f: (l, j, 0, f)),
            pl.BlockSpec((None, None, D, tf), lambda i, f: (l, j, 0, f + nf)),
            pl.BlockSpec((None, None, tf, D), lambda i, f: (l, j, f, 0)),
        ],
        out_specs=pl.BlockSpec((tm, D), lambda i, f: (i, 0)),
        out_shape=jax.ShapeDtypeStruct((T, D), F32),
        scratch_shapes=[pltpu.VMEM((tm, D), BF16), pltpu.VMEM((tm, D), F32)],
        compiler_params=_cparams(("parallel", "arbitrary")),
    )(h, mod_l, gain.reshape(1, D), w13, w13, w2)


def _inproj_kernel(x_ref, mod_ref, g_ref, w_ref, qkg_ref, cos_ref, sin_ref,
                   attn_ref, gdn_ref, rwkv_ref):
    hn = _modulate(x_ref[...], g_ref[...], mod_ref[3:4, :], mod_ref[4:5, :]).astype(BF16)

    pa = jnp.dot(hn, w_ref[:, 0:3 * GROUP_W], preferred_element_type=F32)
    cos = cos_ref[...]
    sin = sin_ref[...]
    for idx, scale in ((0, DIFF_DIM ** -0.5), (1, 1.0)):
        x1 = pa[:, idx * 256:idx * 256 + 128]
        x2 = pa[:, idx * 256 + 128:idx * 256 + 256]
        ms = _group_sum(x1 * x1 + x2 * x2, 16) * (1.0 / DIFF_DIM)
        inv = lax.rsqrt(ms + EPS)
        n1 = x1 * inv * qkg_ref[2 * idx:2 * idx + 1, :]
        n2 = x2 * inv * qkg_ref[2 * idx + 1:2 * idx + 2, :]
        attn_ref[:, idx * 256:idx * 256 + 128] = ((n1 * cos - n2 * sin) * scale).astype(BF16)
        attn_ref[:, idx * 256 + 128:idx * 256 + 256] = ((n2 * cos + n1 * sin) * scale).astype(BF16)
    attn_ref[:, 512:768] = pa[:, 512:768].astype(BF16)

    pb = jnp.dot(hn, w_ref[:, 3 * GROUP_W:ATTN_W], preferred_element_type=F32)
    attn_ref[:, 768:1024] = (pb[:, 0:256] * (HEAD_DIM ** -0.5)).astype(BF16)
    attn_ref[:, 1024:1536] = pb[:, 256:768].astype(BF16)

    gdn_ref[...] = jnp.dot(hn, w_ref[:, ATTN_W:ATTN_W + GDN_W], preferred_element_type=F32)
    rwkv_ref[...] = jnp.dot(hn, w_ref[:, ATTN_W + GDN_W:PROJ_W], preferred_element_type=F32)


def _inproj(h, mod_l, gain, w_perm, l, qkg, cos128, sin128, S):
    T, D = h.shape
    tm = min(512, S)
    tpb = S // tm
    return pl.pallas_call(
        _inproj_kernel,
        name="inproj",
        grid=(T // tm,),
        in_specs=[
            pl.BlockSpec((tm, D), lambda i: (i, 0)),
            pl.BlockSpec((None, N_MOD, D), lambda i: (i // tpb, 0, 0)),
            pl.BlockSpec((1, D), lambda i: (0, 0)),
            pl.BlockSpec((None, D, PROJ_W), lambda i: (l, 0, 0)),
            pl.BlockSpec((4, 128), lambda i: (0, 0)),
            pl.BlockSpec((tm, 128), lambda i: (i % tpb, 0)),
            pl.BlockSpec((tm, 128), lambda i: (i % tpb, 0)),
        ],
        out_specs=[
            pl.BlockSpec((tm, ATTN_W), lambda i: (i, 0)),
            pl.BlockSpec((tm, GDN_W), lambda i: (i, 0)),
            pl.BlockSpec((tm, RWKV_W), lambda i: (i, 0)),
        ],
        out_shape=[
            jax.ShapeDtypeStruct((T, ATTN_W), BF16),
            jax.ShapeDtypeStruct((T, GDN_W), F32),
            jax.ShapeDtypeStruct((T, RWKV_W), F32),
        ],
        compiler_params=_cparams(("parallel",)),
    )(h, mod_l, gain.reshape(1, D), w_perm, qkg, cos128, sin128)


def _diff_kernel(q_ref, k_ref, v_ref, lamv_ref, g_ref, o_ref, acc_ref, *, lam_init, tq, tk):
    i = pl.program_id(1)
    q = q_ref[...]
    lane = lax.broadcasted_iota(jnp.int32, (1, GROUP_W), 1)
    sub = (lane % 128) // 16
    lv = lamv_ref[...]
    lam = (jnp.exp(jnp.sum(lv[0:1, :] * lv[1:2, :], axis=1, keepdims=True))
           - jnp.exp(jnp.sum(lv[2:3, :] * lv[3:4, :], axis=1, keepdims=True)) + lam_init)
    nkv = (i * tq + tq - 1) // tk + 1
    row = i * tq + lax.broadcasted_iota(jnp.int32, (2 * tq, tk), 0) % tq
    col0 = lax.broadcasted_iota(jnp.int32, (2 * tq, tk), 1)
    out = jnp.zeros((tq, GROUP_W), F32)
    for h in range(N_HEADS):
        qc = jnp.concatenate([jnp.where(sub == 2 * h, q, jnp.zeros_like(q)),
                              jnp.where(sub == 2 * h + 1, q, jnp.zeros_like(q))], axis=0)
        acc_ref[...] = jnp.zeros_like(acc_ref)

        def body(kt, carry):
            m, l = carry
            start = pl.multiple_of(kt * tk, tk)
            k = k_ref[pl.ds(start, tk), :]
            v = v_ref[pl.ds(start, tk), :]
            s = lax.dot_general(qc, k, (((1,), (1,)), ((), ())), preferred_element_type=F32)
            s = jnp.where(col0 + kt * tk <= row, s, -jnp.inf)
            m_new = jnp.maximum(m, jnp.max(s, axis=1, keepdims=True))
            alpha = jnp.exp(m - m_new)
            p = jnp.exp(s - m_new)
            l_new = alpha * l + jnp.sum(p, axis=1, keepdims=True)
            acc_ref[...] = alpha * acc_ref[...] + jnp.dot(p.astype(BF16), v,
                                                          preferred_element_type=F32)
            return m_new, l_new

        m0 = jnp.full((2 * tq, 1), -jnp.inf, F32)
        l0 = jnp.zeros((2 * tq, 1), F32)
        m, l = lax.fori_loop(0, nkv, body, (m0, l0))
        o = acc_ref[...] / l
        oh = o[:tq] - lam * o[tq:]
        out = jnp.where(lane // HEAD_DIM == h, oh, out)
    ms = _group_sum(out * out, HEAD_DIM) * (1.0 / HEAD_DIM)
    o_ref[...] = (out * lax.rsqrt(ms + EPS) * g_ref[...] * (1.0 - lam_init)).astype(BF16)


def _diff_attention(attn, lamv, subln_g, lam_init):
    B, S, _ = attn.shape
    tq = min(128, S)
    tk = min(256, S)
    return pl.pallas_call(
        functools.partial(_diff_kernel, lam_init=lam_init, tq=tq, tk=tk),
        name="diff_attn",
        grid=(B, S // tq),
        in_specs=[
            pl.BlockSpec((None, tq, GROUP_W), lambda b, i: (b, i, 0)),
            pl.BlockSpec((None, S, GROUP_W), lambda b, i: (b, 0, 1)),
            pl.BlockSpec((None, S, GROUP_W), lambda b, i: (b, 0, 2)),
            pl.BlockSpec((4, DIFF_DIM), lambda b, i: (0, 0)),
            pl.BlockSpec((1, GROUP_W), lambda b, i: (0, 0)),
        ],
        out_specs=pl.BlockSpec((None, tq, GROUP_W), lambda b, i: (b, i, 0)),
        out_shape=jax.ShapeDtypeStruct((B, S, GROUP_W), BF16),
        scratch_shapes=[pltpu.VMEM((2 * tq, GROUP_W), F32)],
        compiler_params=_cparams(("parallel", "parallel")),
    )(attn, attn, attn, lamv, jnp.tile(subln_g, N_HEADS).reshape(1, GROUP_W))


def _sb_kernel(q_ref, k_ref, v_ref, o_ref, acc_ref, *, t):
    i = pl.program_id(1)
    q = q_ref[...]
    lane = lax.broadcasted_iota(jnp.int32, (1, GROUP_W), 1)
    r_i = lax.broadcasted_iota(jnp.int32, (t, t), 0)
    c_i = lax.broadcasted_iota(jnp.int32, (t, t), 1)
    strict = c_i < r_i
    tri = (r_i > c_i).astype(BF16)
    out = jnp.zeros((t, GROUP_W), F32)
    for h in range(N_HEADS):
        qh = jnp.where(lane // HEAD_DIM == h, q, jnp.zeros_like(q))
        acc_ref[...] = jnp.zeros_like(acc_ref)

        def tile(kt, carry, diag):
            start = pl.multiple_of(kt * t, t)
            k = k_ref[pl.ds(start, t), :]
            v = v_ref[pl.ds(start, t), :]
            z = lax.dot_general(qh, k, (((1,), (1,)), ((), ())), preferred_element_type=F32)
            ls = jnp.minimum(z, 0.0) - jnp.log(1.0 + jnp.exp(-jnp.abs(z)))
            lk = ls - z
            if diag:
                lk = jnp.where(strict, lk, 0.0)
            hi = lk.astype(BF16)
            lo = (lk - hi.astype(F32)).astype(BF16)
            between = (jnp.dot(hi, tri, preferred_element_type=F32)
                       + jnp.dot(lo, tri, preferred_element_type=F32) + carry)
            a = jnp.exp(ls + between)
            if diag:
                a = jnp.where(strict, a, 0.0)
            acc_ref[...] += jnp.dot(a.astype(BF16), v, preferred_element_type=F32)
            return carry + jnp.sum(lk, axis=1, keepdims=True)

        carry = tile(i, jnp.zeros((t, 1), F32), True)
        lax.fori_loop(0, i, lambda n, c: tile(i - 1 - n, c, False), carry)
        out = jnp.where(lane // HEAD_DIM == h, acc_ref[...], out)
    o_ref[...] = out.astype(BF16)


def _sb_attention(attn):
    B, S, _ = attn.shape
    t = min(256, S)
    return pl.pallas_call(
        functools.partial(_sb_kernel, t=t),
        name="sb_attn",
        grid=(B, S // t),
        in_specs=[
            pl.BlockSpec((None, t, GROUP_W), lambda b, i: (b, i, 3)),
            pl.BlockSpec((None, S, GROUP_W), lambda b, i: (b, 0, 4)),
            pl.BlockSpec((None, S, GROUP_W), lambda b, i: (b, 0, 5)),
        ],
        out_specs=pl.BlockSpec((None, t, GROUP_W), lambda b, i: (b, i, 0)),
        out_shape=jax.ShapeDtypeStruct((B, S, GROUP_W), BF16),
        scratch_shapes=[pltpu.VMEM((t, GROUP_W), F32)],
        compiler_params=_cparams(("parallel", "parallel")),
    )(attn, attn, attn)


def _gdn_prep_kernel(x_ref, prev_ref, small_ref, cw_ref, alog_ref, dtb_ref,
                     q_ref, k_ref, v_ref, ab_ref, xs_ref, *, ts):
    i = pl.program_id(1)
    prev = prev_ref[...]
    xs_ref[0:8, :] = jnp.where(i == 0, jnp.zeros_like(prev), prev)
    xs_ref[8:8 + ts, :] = x_ref[...]
    conv = jnp.zeros((ts, 3 * GROUP_W), F32)
    for j in range(CONV_K):
        conv = conv + xs_ref[pl.ds(8 - (CONV_K - 1) + j, ts), :] * cw_ref[j:j + 1, :]
    y = conv * _sigmoid(conv)
    q = y[:, 0:GROUP_W]
    k = y[:, GROUP_W:2 * GROUP_W]
    q_ref[...] = q * lax.rsqrt(_group_sum(q * q, HEAD_DIM) + EPS) * (HEAD_DIM ** -0.5)
    k_ref[...] = k * lax.rsqrt(_group_sum(k * k, HEAD_DIM) + EPS)
    v_ref[...] = y[:, 2 * GROUP_W:3 * GROUP_W]
    small = small_ref[...]
    beta = _sigmoid(small)
    alpha = jnp.exp(-jnp.exp(alog_ref[...]) * _softplus(small + dtb_ref[...]))
    lane = lax.broadcasted_iota(jnp.int32, (1, 128), 1)
    ab_ref[...] = jnp.where(lane < N_HEADS, beta, alpha)


def _gdn_prep(gdn, conv_w, a_log, dt_bias):
    B, S, _ = gdn.shape
    ts = min(512, S)
    pad = jnp.zeros((N_HEADS,), F32)
    alog = jnp.concatenate([pad, a_log, jnp.zeros((120,), F32)]).reshape(1, 128)
    dtb = jnp.concatenate([pad, dt_bias, jnp.zeros((120,), F32)]).reshape(1, 128)
    tok = jax.ShapeDtypeStruct((B, S, GROUP_W), F32)
    return pl.pallas_call(
        functools.partial(_gdn_prep_kernel, ts=ts),
        name="gdn_prep",
        grid=(B, S // ts),
        in_specs=[
            pl.BlockSpec((None, ts, 3 * GROUP_W), lambda b, i: (b, i, 0)),
            pl.BlockSpec((None, 8, 3 * GROUP_W), lambda b, i: (b, jnp.maximum(i * (ts // 8) - 1, 0), 0)),
            pl.BlockSpec((None, ts, 128), lambda b, i: (b, i, 8)),
            pl.BlockSpec((CONV_K, 3 * GROUP_W), lambda b, i: (0, 0)),
            pl.BlockSpec((1, 128), lambda b, i: (0, 0)),
            pl.BlockSpec((1, 128), lambda b, i: (0, 0)),
        ],
        out_specs=[
            pl.BlockSpec((None, ts, GROUP_W), lambda b, i: (b, i, 0)),
            pl.BlockSpec((None, ts, GROUP_W), lambda b, i: (b, i, 0)),
            pl.BlockSpec((None, ts, GROUP_W), lambda b, i: (b, i, 0)),
            pl.BlockSpec((None, ts, 128), lambda b, i: (b, i, 0)),
        ],
        out_shape=[tok, tok, tok, jax.ShapeDtypeStruct((B, S, 128), F32)],
        scratch_shapes=[pltpu.VMEM((ts + 8, 3 * GROUP_W), F32)],
        compiler_params=_cparams(("parallel", "parallel")),
    )(gdn, gdn, gdn, conv_w, alog, dtb)


def _gdn_scan_kernel(q_ref, k_ref, v_ref, al_ref, be_ref, o_ref, s_ref, *, tt):
    @pl.when(pl.program_id(0) == 0)
    def _():
        s_ref[...] = jnp.zeros_like(s_ref)

    def step(t, _):
        al = al_ref[pl.ds(t, 1), :]
        be = be_ref[pl.ds(t, 1), :]
        parts = [None] * 4
        for kk in range(HEAD_DIM):
            term = s_ref[kk] * k_ref[t, pl.ds(kk, 1), :]
            parts[kk % 4] = term if parts[kk % 4] is None else parts[kk % 4] + term
        ks = (parts[0] + parts[1]) + (parts[2] + parts[3])
        d = be * (v_ref[t] - al * ks)
        parts = [None] * 4
        for kk in range(HEAD_DIM):
            s_new = al * s_ref[kk] + k_ref[t, pl.ds(kk, 1), :] * d
            s_ref[kk] = s_new
            term = s_new * q_ref[t, pl.ds(kk, 1), :]
            parts[kk % 4] = term if parts[kk % 4] is None else parts[kk % 4] + term
        o_ref[t] = (parts[0] + parts[1]) + (parts[2] + parts[3])
        return 0

    lax.fori_loop(0, tt, step, 0)


def _gdn_scan(q, k, v, al, be):
    S, N, Ln = q.shape
    tt = min(32, S)
    seq = pl.BlockSpec((tt, N, Ln), lambda i: (i, 0, 0))
    vec = pl.BlockSpec((tt, Ln), lambda i: (i, 0))
    return pl.pallas_call(
        functools.partial(_gdn_scan_kernel, tt=tt),
        name="gdn_scan",
        grid=(S // tt,),
        in_specs=[seq, seq, seq, vec, vec],
        out_specs=seq,
        out_shape=jax.ShapeDtypeStruct((S, N, Ln), F32),
        scratch_shapes=[pltpu.VMEM((N, N, Ln), F32)],
        compiler_params=_cparams(("arbitrary",)),
    )(q, k, v, al, be)


def _gdn_post_kernel(o_ref, gate_ref, g_ref, out_ref):
    o = o_ref[...]
    gate = gate_ref[...]
    ms = _group_sum(o * o, HEAD_DIM) * (1.0 / HEAD_DIM)
    out_ref[...] = (o * lax.rsqrt(ms + EPS) * g_ref[...] * (gate * _sigmoid(gate))).astype(BF16)


def _gdn_post(o, gdn, norm_g):
    B, S, _ = o.shape
    ts = min(512, S)
    return pl.pallas_call(
        _gdn_post_kernel,
        name="gdn_post",
        grid=(B, S // ts),
        in_specs=[
            pl.BlockSpec((None, ts, GROUP_W), lambda b, i: (b, i, 0)),
            pl.BlockSpec((None, ts, GROUP_W), lambda b, i: (b, i, 3)),
            pl.BlockSpec((1, GROUP_W), lambda b, i: (0, 0)),
        ],
        out_specs=pl.BlockSpec((None, ts, GROUP_W), lambda b, i: (b, i, 0)),
        out_shape=jax.ShapeDtypeStruct((B, S, GROUP_W), BF16),
        compiler_params=_cparams(("parallel", "parallel")),
    )(o, gdn, jnp.tile(norm_g, N_HEADS).reshape(1, GROUP_W))


def _rwkv_prep_kernel(*refs, ts, first):
    if first:
        (x_ref, prev_ref, mu_ref, vecs_ref, wup_ref, aup_ref, gup_ref,
         r_ref, w_ref, k_ref, v_ref, a_ref, b_ref, g_ref, vf_ref, xs_ref) = refs
    else:
        (x_ref, prev_ref, mu_ref, vecs_ref, wup_ref, aup_ref, gup_ref, vup_ref, vfin_ref,
         r_ref, w_ref, k_ref, v_ref, a_ref, b_ref, g_ref, xs_ref) = refs
    i = pl.program_id(1)
    prev = prev_ref[...]
    xs_ref[0:8, :] = jnp.where(i == 0, jnp.zeros_like(prev), prev)
    x = x_ref[...]
    xs_ref[8:8 + ts, :] = x
    xm = x + (xs_ref[pl.ds(7, ts), :] - x) * mu_ref[...]
    r = xm[:, 0:256]
    k = xm[:, 256:512]
    v = xm[:, 512:768]
    wd = xm[:, 768:896]
    ad = xm[:, 896:1024]
    gd = xm[:, 1024:1280]
    w0, a0, k_k, k_a = vecs_ref[0:1, :], vecs_ref[1:2, :], vecs_ref[2:3, :], vecs_ref[3:4, :]
    dot = functools.partial(jnp.dot, precision=HIGHEST, preferred_element_type=F32)
    w_log = -_softplus(-(w0 + dot(jnp.tanh(wd), wup_ref[...]))) - 0.5
    a = _sigmoid(a0 + dot(ad, aup_ref[...]))
    g_ref[...] = dot(_sigmoid(gd), gup_ref[...])
    if first:
        vf_ref[...] = v
    else:
        vd = xm[:, 1280:1408]
        v = v + (vfin_ref[...] - v) * _sigmoid(vecs_ref[4:5, :] + dot(vd, vup_ref[...]))
    kk = k * k_k
    kk = kk * lax.rsqrt(_group_sum(kk * kk, HEAD_DIM) + EPS)
    r_ref[...] = r
    w_ref[...] = jnp.exp(-jnp.exp(w_log))
    k_ref[...] = k * (1.0 + (a - 1.0) * k_a)
    v_ref[...] = v
    a_ref[...] = -kk
    b_ref[...] = kk * a


def _pad_rows(w, rows):
    return jnp.concatenate([w, jnp.zeros((rows - w.shape[0], w.shape[1]), w.dtype)], axis=0)


def _rwkv_prep(rw, mu, vecs, w_up, a_up, g_up, vmix_up, v_first):
    B, S, _ = rw.shape
    ts = min(256, S)
    first = v_first is None
    tok_spec = pl.BlockSpec((None, ts, GROUP_W), lambda b, i: (b, i, 0))
    const = lambda shape: pl.BlockSpec(shape, lambda b, i: (0,) * len(shape))
    in_specs = [
        pl.BlockSpec((None, ts, RWKV_W), lambda b, i: (b, i, 0)),
        pl.BlockSpec((None, 8, RWKV_W), lambda b, i: (b, jnp.maximum(i * (ts // 8) - 1, 0), 0)),
        const((1, RWKV_W)), const((8, GROUP_W)),
        const((128, GROUP_W)), const((128, GROUP_W)), const((256, GROUP_W)),
    ]
    args = [rw, rw, mu, vecs, _pad_rows(w_up, 128), _pad_rows(a_up, 128), _pad_rows(g_up, 256)]
    if not first:
        in_specs += [const((128, GROUP_W)), tok_spec]
        args += [_pad_rows(vmix_up, 128), v_first]
    tok = jax.ShapeDtypeStruct((B, S, GROUP_W), F32)
    n_out = 8 if first else 7
    return pl.pallas_call(
        functools.partial(_rwkv_prep_kernel, ts=ts, first=first),
        name="rwkv_prep",
        grid=(B, S // ts),
        in_specs=in_specs,
        out_specs=[tok_spec] * n_out,
        out_shape=[tok] * n_out,
        scratch_shapes=[pltpu.VMEM((ts + 8, RWKV_W), F32)],
        compiler_params=_cparams(("parallel", "parallel")),
    )(*args)


def _wkv_scan_kernel(r_ref, w_ref, k_ref, v_ref, a_ref, b_ref, y_ref, s_ref, *, tt):
    @pl.when(pl.program_id(0) == 0)
    def _():
        s_ref[...] = jnp.zeros_like(s_ref)

    def step(t, _):
        parts = [None] * 4
        for j in range(HEAD_DIM):
            term = s_ref[j] * a_ref[t, pl.ds(j, 1), :]
            parts[j % 4] = term if parts[j % 4] is None else parts[j % 4] + term
        sa = (parts[0] + parts[1]) + (parts[2] + parts[3])
        vt = v_ref[t]
        parts = [None] * 4
        for j in range(HEAD_DIM):
            s_new = (s_ref[j] * w_ref[t, pl.ds(j, 1), :]
                     + (sa * b_ref[t, pl.ds(j, 1), :] + vt * k_ref[t, pl.ds(j, 1), :]))
            s_ref[j] = s_new
            term = s_new * r_ref[t, pl.ds(j, 1), :]
            parts[j % 4] = term if parts[j % 4] is None else parts[j % 4] + term
        y_ref[t] = (parts[0] + parts[1]) + (parts[2] + parts[3])
        return 0

    lax.fori_loop(0, tt, step, 0)


def _wkv_scan(r, w, k, v, a, b):
    S, N, Ln = r.shape
    tt = min(32, S)
    seq = pl.BlockSpec((tt, N, Ln), lambda i: (i, 0, 0))
    return pl.pallas_call(
        functools.partial(_wkv_scan_kernel, tt=tt),
        name="wkv_scan",
        grid=(S // tt,),
        in_specs=[seq] * 6,
        out_specs=seq,
        out_shape=jax.ShapeDtypeStruct((S, N, Ln), F32),
        scratch_shapes=[pltpu.VMEM((N, N, Ln), F32)],
        compiler_params=_cparams(("arbitrary",)),
    )(r, w, k, v, a, b)


def _rwkv_post_kernel(y_ref, r_ref, k_ref, v_ref, g_ref, vecs_ref, out_ref):
    y = y_ref[...]
    inv_n = 1.0 / HEAD_DIM
    yc = y - _group_sum(y, HEAD_DIM) * inv_n
    var = _group_sum(yc * yc, HEAD_DIM) * inv_n
    yn = yc * lax.rsqrt(var + LNX_EPS) * vecs_ref[0:1, :] + vecs_ref[1:2, :]
    bonus = _group_sum(r_ref[...] * k_ref[...] * vecs_ref[2:3, :], HEAD_DIM)
    out_ref[...] = ((yn + bonus * v_ref[...]) * g_ref[...]).astype(BF16)


def _rwkv_post(y, r, k, v, g, vecs):
    B, S, _ = y.shape
    ts = min(512, S)
    tok_spec = pl.BlockSpec((None, ts, GROUP_W), lambda b, i: (b, i, 0))
    return pl.pallas_call(
        _rwkv_post_kernel,
        name="rwkv_post",
        grid=(B, S // ts),
        in_specs=[tok_spec] * 5 + [pl.BlockSpec((8, GROUP_W), lambda b, i: (0, 0))],
        out_specs=tok_spec,
        out_shape=jax.ShapeDtypeStruct((B, S, GROUP_W), BF16),
        compiler_params=_cparams(("parallel", "parallel")),
    )(y, r, k, v, g, vecs)


def _outproj_kernel(h_ref, mod_ref, oa_ref, ob_ref, oc_ref, od_ref, w_ref, out_ref):
    acc = jnp.dot(oa_ref[...], w_ref[0:GROUP_W, :], preferred_element_type=F32)
    acc += jnp.dot(ob_ref[...], w_ref[GROUP_W:2 * GROUP_W, :], preferred_element_type=F32)
    acc += jnp.dot(oc_ref[...], w_ref[2 * GROUP_W:3 * GROUP_W, :], preferred_element_type=F32)
    acc += jnp.dot(od_ref[...], w_ref[3 * GROUP_W:4 * GROUP_W, :], preferred_element_type=F32)
    out_ref[...] = h_ref[...] + mod_ref[5:6, :] * acc


def _outproj(h, mod_l, o_a, o_b, o_c, o_d, w_out, l, S):
    T, D = h.shape
    tm = min(1024, S)
    tpb = S // tm
    mix = pl.BlockSpec((tm, GROUP_W), lambda i: (i, 0))
    return pl.pallas_call(
        _outproj_kernel,
        name="outproj",
        grid=(T // tm,),
        in_specs=[
            pl.BlockSpec((tm, D), lambda i: (i, 0)),
            pl.BlockSpec((None, N_MOD, D), lambda i: (i // tpb, 0, 0)),
            mix, mix, mix, mix,
            pl.BlockSpec((None, D, D), lambda i: (l, 0, 0)),
        ],
        out_specs=pl.BlockSpec((tm, D), lambda i: (i, 0)),
        out_shape=jax.ShapeDtypeStruct((T, D), F32),
        compiler_params=_cparams(("parallel",)),
    )(h, mod_l, o_a, o_b, o_c, o_d, w_out)


def _pad_cols(w, width):
    pad = width - w.shape[-1]
    return jnp.concatenate([w, jnp.zeros(w.shape[:-1] + (pad,), w.dtype)], axis=-1)


def _rope_perm():
    p = np.arange(128)
    first = (p // 16) * DIFF_DIM + p % 16
    return np.concatenate([first, first + 16])


def _permute_w_in(w_in, vmix_down):
    L, D, _ = w_in.shape
    perm = _rope_perm()
    g0 = 9 * GROUP_W
    r0 = RWKV_START
    vd = jnp.concatenate([jnp.zeros((1, D, vmix_down.shape[-1]), w_in.dtype), vmix_down], axis=0)
    cols = [
        w_in[..., perm], w_in[..., GROUP_W + perm], w_in[..., 2 * GROUP_W:3 * GROUP_W],
        w_in[..., 3 * GROUP_W:6 * GROUP_W],
        w_in[..., 6 * GROUP_W:g0], w_in[..., g0 + 8:g0 + 8 + GROUP_W], _pad_cols(w_in[..., g0:g0 + 8], 128),
        w_in[..., r0:r0 + 768], _pad_cols(w_in[..., r0 + 768:r0 + 832], 128),
        _pad_cols(w_in[..., r0 + 832:r0 + 896], 128), _pad_cols(w_in[..., r0 + 896:r0 + 1056], 256),
        _pad_cols(vd, 128),
    ]
    return jnp.concatenate(cols, axis=-1).astype(BF16)


def _permute_mu(mu, vmix_mu):
    L = mu.shape[0]
    vm = jnp.concatenate([jnp.zeros((1, vmix_mu.shape[-1]), mu.dtype), vmix_mu], axis=0)
    cols = [mu[:, 0:768], _pad_cols(mu[:, 768:832], 128), _pad_cols(mu[:, 832:896], 128),
            _pad_cols(mu[:, 896:1056], 256), _pad_cols(vm, 128)]
    return jnp.concatenate(cols, axis=-1).reshape(L, 1, RWKV_W)


def _to_lanes(x):
    B, S, _ = x.shape
    return x.reshape(B, S, N_HEADS, HEAD_DIM).transpose(1, 3, 0, 2).reshape(S, HEAD_DIM, B * N_HEADS)


def _from_lanes(y, B):
    S = y.shape[0]
    return y.reshape(S, HEAD_DIM, B, N_HEADS).transpose(2, 0, 3, 1).reshape(B, S, GROUP_W)


def _head_scalars_to_lanes(x):
    B, S, _ = x.shape
    return x.transpose(1, 0, 2).reshape(S, B * N_HEADS)


def kernel(x, c, ada_w, ada_b, norm_g, ffn_w13, ffn_w2, w_in, w_out, diff_qk_g, diff_lambda, diff_subln_g, gdn_conv_w, gdn_a_log, gdn_dt_bias, gdn_norm_g, rwkv_mu, rwkv_w0, rwkv_w_up, rwkv_a0, rwkv_a_up, rwkv_g_up, rwkv_k_k, rwkv_k_a, rwkv_r_k, rwkv_lnx_w, rwkv_lnx_b, rwkv_vmix_down, rwkv_vmix_mu, rwkv_v0, rwkv_vmix_up):
    B, S, D = x.shape
    L = ada_w.shape[0]
    T = B * S

    mod = _mod_all(c, ada_w, ada_b)
    w13 = ffn_w13.astype(BF16)
    w2 = ffn_w2.astype(BF16)
    w_perm = _permute_w_in(w_in, rwkv_vmix_down)
    w_o = w_out.astype(BF16)
    mu = _permute_mu(rwkv_mu, rwkv_vmix_mu)

    inv = ROPE_THETA ** (-jnp.arange(0, DIFF_DIM, 2, dtype=F32) / DIFF_DIM)
    ang = jnp.arange(S, dtype=F32)[:, None] * inv[None, :]
    cos128 = jnp.tile(jnp.cos(ang), (1, 8))
    sin128 = jnp.tile(jnp.sin(ang), (1, 8))

    h = x.reshape(T, D)
    v_first = None
    for l in range(L):
        lam_init = 0.8 - 0.6 * math.exp(-0.3 * l)
        h = _ffn(h, mod[l], norm_g[l, 0], w13, w2, l, 0, 0, S)

        qkg = jnp.stack([jnp.tile(diff_qk_g[l, 0, :16], 8), jnp.tile(diff_qk_g[l, 0, 16:], 8),
                         jnp.tile(diff_qk_g[l, 1, :16], 8), jnp.tile(diff_qk_g[l, 1, 16:], 8)])
        attn, gdn, rw = _inproj(h, mod[l], norm_g[l, 1], w_perm, l, qkg, cos128, sin128, S)
        attn = attn.reshape(B, S, ATTN_W)
        gdn = gdn.reshape(B, S, GDN_W)
        rw = rw.reshape(B, S, RWKV_W)

        o_a = _diff_attention(attn, diff_lambda[l], diff_subln_g[l], lam_init)
        o_b = _sb_attention(attn)

        gq, gk, gv, gab = _gdn_prep(gdn, gdn_conv_w[l], gdn_a_log[l], gdn_dt_bias[l])
        go = _gdn_scan(_to_lanes(gq), _to_lanes(gk), _to_lanes(gv),
                       _head_scalars_to_lanes(gab[..., N_HEADS:2 * N_HEADS]),
                       _head_scalars_to_lanes(gab[..., 0:N_HEADS]))
        o_c = _gdn_post(_from_lanes(go, B), gdn, gdn_norm_g[l])

        zeros = jnp.zeros((GROUP_W,), F32)
        vecs = jnp.stack([rwkv_w0[l], rwkv_a0[l], rwkv_k_k[l], rwkv_k_a[l],
                          rwkv_v0[l - 1] if l > 0 else zeros, zeros, zeros, zeros])
        outs = _rwkv_prep(rw, mu[l], vecs, rwkv_w_up[l], rwkv_a_up[l], rwkv_g_up[l],
                          rwkv_vmix_up[l - 1] if l > 0 else None, v_first)
        rr, rw_, rk, rv, ra, rb, rg = outs[:7]
        if l == 0:
            v_first = outs[7]
        ry = _wkv_scan(*[_to_lanes(t) for t in (rr, rw_, rk, rv, ra, rb)])
        pvecs = jnp.stack([rwkv_lnx_w[l], rwkv_lnx_b[l], rwkv_r_k[l].reshape(GROUP_W),
                           zeros, zeros, zeros, zeros, zeros])
        o_d = _rwkv_post(_from_lanes(ry, B), rr, rk, rv, rg, pvecs)

        flat = lambda t: t.reshape(T, GROUP_W)
        h = _outproj(h, mod[l], flat(o_a), flat(o_b), flat(o_c), flat(o_d), w_o, l, S)
        h = _ffn(h, mod[l], norm_g[l, 2], w13, w2, l, 1, 6, S)
    return h.reshape(B, S, D)
```

```python
import functools
import math

import numpy as np
import jax
import jax.numpy as jnp
from jax import lax
from jax.experimental import pallas as pl
from jax.experimental.pallas import tpu as pltpu

F32 = jnp.float32
BF16 = jnp.bfloat16
HIGHEST = lax.Precision.HIGHEST

D_MODEL = 1024
N_HEADS = 4
HEAD_DIM = 64
GROUP_W = N_HEADS * HEAD_DIM
DIFF_DIM = HEAD_DIM // 2
D_FF = 2816
N_MOD = 9
EPS = 1e-6
LNX_EPS = 64e-5
ROPE_THETA = 10000.0
CONV_K = 4
RWKV_START = 9 * GROUP_W + 2 * N_HEADS + GROUP_W

ATTN_W = 6 * GROUP_W
GDN_W = 3 * GROUP_W + GROUP_W + 128
RWKV_W = 3 * GROUP_W + 128 + 128 + 256 + 128
PROJ_W = ATTN_W + GDN_W + RWKV_W
VMEM_LIMIT = 56 * 1024 * 1024


def _cparams(sem):
    return pltpu.CompilerParams(dimension_semantics=sem, vmem_limit_bytes=VMEM_LIMIT)


def _sigmoid(x):
    return 1.0 / (1.0 + jnp.exp(-x))


def _softplus(x):
    return jnp.maximum(x, 0.0) + jnp.log(1.0 + jnp.exp(-jnp.abs(x)))


def _group_ones(n, group):
    r = lax.broadcasted_iota(jnp.int32, (n, n), 0) // group
    c = lax.broadcasted_iota(jnp.int32, (n, n), 1) // group
    return (r == c).astype(F32)


def _group_sum(x, group):
    return jnp.dot(x, _group_ones(x.shape[-1], group), precision=HIGHEST,
                   preferred_element_type=F32)


def _modulate(x, gain, shift, scale):
    ms = jnp.mean(x * x, axis=-1, keepdims=True)
    return (x * lax.rsqrt(ms + EPS) * gain) * (1.0 + scale) + shift


def _mod_kernel(c_ref, w_ref, b_ref, o_ref):
    c = c_ref[...]
    cond = c * _sigmoid(c)
    o_ref[...] = jnp.dot(cond, w_ref[...], precision=HIGHEST,
                         preferred_element_type=F32) + b_ref[...]


def _mod_all(c, ada_w, ada_b):
    L, D, N = ada_w.shape
    B = c.shape[0]
    tn = 1024
    out = pl.pallas_call(
        _mod_kernel,
        name="adaln_mod",
        grid=(L, N // tn),
        in_specs=[
            pl.BlockSpec((B, D), lambda l, n: (0, 0)),
            pl.BlockSpec((None, D, tn), lambda l, n: (l, 0, n)),
            pl.BlockSpec((None, 1, tn), lambda l, n: (l, 0, n)),
        ],
        out_specs=pl.BlockSpec((None, B, tn), lambda l, n: (l, 0, n)),
        out_shape=jax.ShapeDtypeStruct((L, B, N), F32),
        compiler_params=_cparams(("parallel", "parallel")),
    )(c, ada_w, ada_b.reshape(L, 1, N))
    return out.reshape(L, B, N_MOD, D)


def _ffn_kernel(x_ref, mod_ref, g_ref, w13_ref, w2_ref, o_ref, act_ref, *, mb, tf):
    hn = _modulate(x_ref[...], g_ref[...], mod_ref[mb:mb + 1, :], mod_ref[mb + 1:mb + 2, :])
    hn = hn.astype(BF16)
    for f in range(D_FF // tf):
        gate = jnp.dot(hn, w13_ref[:, f * tf:(f + 1) * tf], preferred_element_type=F32)
        up = jnp.dot(hn, w13_ref[:, D_FF + f * tf:D_FF + (f + 1) * tf], preferred_element_type=F32)
        act_ref[:, f * tf:(f + 1) * tf] = (gate * _sigmoid(gate) * up).astype(BF16)
    acc = jnp.dot(act_ref[...], w2_ref[...], preferred_element_type=F32)
    o_ref[...] = x_ref[...] + 0.5 * mod_ref[mb + 2:mb + 3, :] * acc


def _ffn(h, mod_l, gain, w13, w2, l, j, mb, S):
    T, D = h.shape
    tm = min(1024, S)
    tf = 256
    tpb = S // tm
    resident = dict(pipeline_mode=pl.Buffered(1))
    return pl.pallas_call(
        functools.partial(_ffn_kernel, mb=mb, tf=tf),
        name="ffn",
        grid=(T // tm,),
        in_specs=[
            pl.BlockSpec((tm, D), lambda i: (i, 0)),
            pl.BlockSpec((None, N_MOD, D), lambda i: (i // tpb, 0, 0)),
            pl.BlockSpec((1, D), lambda i: (0, 0)),
            pl.BlockSpec((None, None, D, 2 * D_FF), lambda i: (l, j, 0, 0), **resident),
            pl.BlockSpec((None, None, D_FF, D), lambda i: (l, j, 0, 0), **resident),
        ],
        out_specs=pl.BlockSpec((tm, D), lambda i: (i, 0)),
        out_shape=jax.ShapeDtypeStruct((T, D), F32),
        scratch_shapes=[pltpu.VMEM((tm, D_FF), BF16)],
        compiler_params=_cparams(("parallel",)),
    )(h, mod_l, gain.reshape(1, D), w13, w2)


def _inproj_kernel(x_ref, mod_ref, g_ref, w_ref, qkg_ref, cos_ref, sin_ref,
                   attn_ref, gdn_ref, rwkv_ref):
    hn = _modulate(x_ref[...], g_ref[...], mod_ref[3:4, :], mod_ref[4:5, :]).astype(BF16)

    pa = jnp.dot(hn, w_ref[:, 0:3 * GROUP_W], preferred_element_type=F32)
    cos = cos_ref[...]
    sin = sin_ref[...]
    for idx, scale in ((0, DIFF_DIM ** -0.5), (1, 1.0)):
        x1 = pa[:, idx * 256:idx * 256 + 128]
        x2 = pa[:, idx * 256 + 128:idx * 256 + 256]
        ms = _group_sum(x1 * x1 + x2 * x2, 16) * (1.0 / DIFF_DIM)
        inv = lax.rsqrt(ms + EPS)
        n1 = x1 * inv * qkg_ref[2 * idx:2 * idx + 1, :]
        n2 = x2 * inv * qkg_ref[2 * idx + 1:2 * idx + 2, :]
        attn_ref[:, idx * 256:idx * 256 + 128] = ((n1 * cos - n2 * sin) * scale).astype(BF16)
        attn_ref[:, idx * 256 + 128:idx * 256 + 256] = ((n2 * cos + n1 * sin) * scale).astype(BF16)
    attn_ref[:, 512:768] = pa[:, 512:768].astype(BF16)

    pb = jnp.dot(hn, w_ref[:, 3 * GROUP_W:ATTN_W], preferred_element_type=F32)
    attn_ref[:, 768:1024] = (pb[:, 0:256] * (HEAD_DIM ** -0.5)).astype(BF16)
    attn_ref[:, 1024:1536] = pb[:, 256:768].astype(BF16)

    gdn_ref[...] = jnp.dot(hn, w_ref[:, ATTN_W:ATTN_W + GDN_W], preferred_element_type=F32)
    rwkv_ref[...] = jnp.dot(hn, w_ref[:, ATTN_W + GDN_W:PROJ_W], preferred_element_type=F32)


def _inproj(h, mod_l, gain, w_perm, l, qkg, cos128, sin128, S):
    T, D = h.shape
    tm = min(512, S)
    tpb = S // tm
    return pl.pallas_call(
        _inproj_kernel,
        name="inproj",
        grid=(T // tm,),
        in_specs=[
            pl.BlockSpec((tm, D), lambda i: (i, 0)),
            pl.BlockSpec((None, N_MOD, D), lambda i: (i // tpb, 0, 0)),
            pl.BlockSpec((1, D), lambda i: (0, 0)),
            pl.BlockSpec((None, D, PROJ_W), lambda i: (l, 0, 0)),
            pl.BlockSpec((4, 128), lambda i: (0, 0)),
            pl.BlockSpec((tm, 128), lambda i: (i % tpb, 0)),
            pl.BlockSpec((tm, 128), lambda i: (i % tpb, 0)),
        ],
        out_specs=[
            pl.BlockSpec((tm, ATTN_W), lambda i: (i, 0)),
            pl.BlockSpec((tm, GDN_W), lambda i: (i, 0)),
            pl.BlockSpec((tm, RWKV_W), lambda i: (i, 0)),
        ],
        out_shape=[
            jax.ShapeDtypeStruct((T, ATTN_W), BF16),
            jax.ShapeDtypeStruct((T, GDN_W), F32),
            jax.ShapeDtypeStruct((T, RWKV_W), F32),
        ],
        compiler_params=_cparams(("parallel",)),
    )(h, mod_l, gain.reshape(1, D), w_perm, qkg, cos128, sin128)


def _diff_kernel(q_ref, k_ref, vt_ref, lamv_ref, g_ref, o_ref, qc_ref, acc_ref,
                 *, lam_init, tq, tk):
    i = pl.program_id(1)
    q = q_ref[...]
    lane = lax.broadcasted_iota(jnp.int32, (1, GROUP_W), 1)
    sub = (lane % 128) // 16
    for h in range(N_HEADS):
        qc_ref[h, 0:tq, :] = jnp.where(sub == 2 * h, q, jnp.zeros_like(q))
        qc_ref[h, tq:2 * tq, :] = jnp.where(sub == 2 * h + 1, q, jnp.zeros_like(q))
    acc_ref[...] = jnp.zeros_like(acc_ref)

    def tile(kt, carry, masked):
        start = pl.multiple_of(kt * tk, tk)
        k = k_ref[pl.ds(start, tk), :]
        heads = range(N_HEADS)
        ss = [lax.dot_general(k, qc_ref[h], (((1,), (1,)), ((), ())), preferred_element_type=F32)
              for h in heads]
        out, ps, alphas = [], [], []
        for h in heads:
            m, l = carry[h]
            s = ss[h]
            if masked:
                key = kt * tk + lax.broadcasted_iota(jnp.int32, (tk, 2 * tq), 0)
                qpos = i * tq + lax.broadcasted_iota(jnp.int32, (tk, 2 * tq), 1) % tq
                s = jnp.where(key <= qpos, s, -jnp.inf)
            m_new = jnp.maximum(m, jnp.max(s, axis=0, keepdims=True))
            alpha = jnp.exp(m - m_new)
            p = jnp.exp(s - m_new)
            out.append((m_new, alpha * l + jnp.sum(p, axis=0, keepdims=True)))
            ps.append(p.astype(BF16))
            alphas.append(alpha)
        for h in heads:
            vt = vt_ref[kt, h * HEAD_DIM:(h + 1) * HEAD_DIM, :]
            acc_ref[h] = alphas[h] * acc_ref[h] + jnp.dot(vt, ps[h], preferred_element_type=F32)
        return tuple(out)

    init = tuple((jnp.full((1, 2 * tq), -jnp.inf, F32), jnp.zeros((1, 2 * tq), F32))
                 for _ in range(N_HEADS))
    kd = (i * tq) // tk
    carry = lax.fori_loop(0, kd, lambda kt, c: tile(kt, c, False), init)
    carry = tile(kd, carry, True)

    lv = lamv_ref[...]
    lam = (jnp.exp(jnp.sum(lv[0:1, :] * lv[1:2, :], axis=1, keepdims=True))
           - jnp.exp(jnp.sum(lv[2:3, :] * lv[3:4, :], axis=1, keepdims=True)) + lam_init)
    rows = []
    for h in range(N_HEADS):
        o = acc_ref[h] / carry[h][1]
        oh = o[:, :tq] - lam * o[:, tq:]
        ms = jnp.mean(oh * oh, axis=0, keepdims=True)
        rows.append(oh * lax.rsqrt(ms + EPS) * g_ref[h * HEAD_DIM:(h + 1) * HEAD_DIM, :]
                    * (1.0 - lam_init))
    o_ref[...] = jnp.concatenate(rows, axis=0).T.astype(BF16)


def _kv_tiles_transposed(v, tk):
    B, S, C = v.shape
    return v.reshape(B, S // tk, tk, C).transpose(0, 1, 3, 2)


def _diff_attention(attn, lamv, subln_g, lam_init):
    B, S, _ = attn.shape
    tq = min(128, S)
    tk = min(256, S)
    vt = _kv_tiles_transposed(attn[..., 2 * GROUP_W:3 * GROUP_W], tk)
    gain = jnp.broadcast_to(jnp.tile(subln_g, N_HEADS)[:, None], (GROUP_W, tq))
    return pl.pallas_call(
        functools.partial(_diff_kernel, lam_init=lam_init, tq=tq, tk=tk),
        name="diff_attn",
        grid=(B, S // tq),
        in_specs=[
            pl.BlockSpec((None, tq, GROUP_W), lambda b, i: (b, i, 0)),
            pl.BlockSpec((None, S, GROUP_W), lambda b, i: (b, 0, 1)),
            pl.BlockSpec((None, S // tk, GROUP_W, tk), lambda b, i: (b, 0, 0, 0)),
            pl.BlockSpec((4, DIFF_DIM), lambda b, i: (0, 0)),
            pl.BlockSpec((GROUP_W, tq), lambda b, i: (0, 0)),
        ],
        out_specs=pl.BlockSpec((None, tq, GROUP_W), lambda b, i: (b, i, 0)),
        out_shape=jax.ShapeDtypeStruct((B, S, GROUP_W), BF16),
        scratch_shapes=[pltpu.VMEM((N_HEADS, 2 * tq, GROUP_W), BF16),
                        pltpu.VMEM((N_HEADS, HEAD_DIM, 2 * tq), F32)],
        compiler_params=_cparams(("parallel", "parallel")),
    )(attn, attn, vt, lamv, gain)


def _sb_kernel(q_ref, k_ref, vt_ref, o_ref, qh_ref, tri_ref, acc_ref, *, t):
    i = pl.program_id(1)
    q = q_ref[...]
    lane = lax.broadcasted_iota(jnp.int32, (1, GROUP_W), 1)
    r_i = lax.broadcasted_iota(jnp.int32, (t, t), 0)
    c_i = lax.broadcasted_iota(jnp.int32, (t, t), 1)
    tri = (c_i > r_i).astype(BF16)
    tri_ref[:, 0:t] = tri
    tri_ref[:, t:2 * t] = tri
    for h in range(N_HEADS):
        qh_ref[h] = jnp.where(lane // HEAD_DIM == h, q, jnp.zeros_like(q))
    acc_ref[...] = jnp.zeros_like(acc_ref)

    def tile(kt, carry, diag):
        start = pl.multiple_of(kt * t, t)
        k = k_ref[pl.ds(start, t), :]
        heads = range(N_HEADS)
        strict = r_i < c_i
        zs = [lax.dot_general(k, qh_ref[h], (((1,), (1,)), ((), ())), preferred_element_type=F32)
              for h in heads]
        lss, hilos, sums = [], [], []
        for h in heads:
            z = zs[h]
            ls = jnp.minimum(z, 0.0) - jnp.log(1.0 + jnp.exp(-jnp.abs(z)))
            lk = ls - z
            if diag:
                lk = jnp.where(strict, lk, 0.0)
            hi = lk.astype(BF16)
            lo = (lk - hi.astype(F32)).astype(BF16)
            lss.append(ls)
            hilos.append(jnp.concatenate([hi, lo], axis=0))
            sums.append(jnp.sum(lk, axis=0, keepdims=True))
        bts = [jnp.dot(tri_ref[...], hilos[h], preferred_element_type=F32) for h in heads]
        pas = []
        for h in heads:
            a = jnp.exp(lss[h] + (bts[h] + carry[h]))
            if diag:
                a = jnp.where(strict, a, 0.0)
            pas.append(a.astype(BF16))
        for h in heads:
            vt = vt_ref[kt, h * HEAD_DIM:(h + 1) * HEAD_DIM, :]
            acc_ref[h] += jnp.dot(vt, pas[h], preferred_element_type=F32)
        return tuple(carry[h] + sums[h] for h in heads)

    carry = tile(i, tuple(jnp.zeros((1, t), F32) for _ in range(N_HEADS)), True)
    lax.fori_loop(0, i, lambda n, c: tile(i - 1 - n, c, False), carry)
    o_ref[...] = jnp.concatenate([acc_ref[h] for h in range(N_HEADS)], axis=0).T.astype(BF16)


def _sb_attention(attn):
    B, S, _ = attn.shape
    t = min(256, S)
    vt = _kv_tiles_transposed(attn[..., 5 * GROUP_W:6 * GROUP_W], t)
    return pl.pallas_call(
        functools.partial(_sb_kernel, t=t),
        name="sb_attn",
        grid=(B, S // t),
        in_specs=[
            pl.BlockSpec((None, t, GROUP_W), lambda b, i: (b, i, 3)),
            pl.BlockSpec((None, S, GROUP_W), lambda b, i: (b, 0, 4)),
            pl.BlockSpec((None, S // t, GROUP_W, t), lambda b, i: (b, 0, 0, 0)),
        ],
        out_specs=pl.BlockSpec((None, t, GROUP_W), lambda b, i: (b, i, 0)),
        out_shape=jax.ShapeDtypeStruct((B, S, GROUP_W), BF16),
        scratch_shapes=[pltpu.VMEM((N_HEADS, t, GROUP_W), BF16),
                        pltpu.VMEM((t, 2 * t), BF16),
                        pltpu.VMEM((N_HEADS, HEAD_DIM, t), F32)],
        compiler_params=_cparams(("parallel", "parallel")),
    )(attn, attn, vt)


def _gdn_prep_kernel(x_ref, prev_ref, small_ref, cw_ref, alog_ref, dtb_ref,
                     q_ref, k_ref, v_ref, ab_ref, xs_ref, *, ts):
    i = pl.program_id(1)
    prev = prev_ref[...]
    xs_ref[0:8, :] = jnp.where(i == 0, jnp.zeros_like(prev), prev)
    xs_ref[8:8 + ts, :] = x_ref[...]
    conv = jnp.zeros((ts, 3 * GROUP_W), F32)
    for j in range(CONV_K):
        conv = conv + xs_ref[pl.ds(8 - (CONV_K - 1) + j, ts), :] * cw_ref[j:j + 1, :]
    y = conv * _sigmoid(conv)
    q = y[:, 0:GROUP_W]
    k = y[:, GROUP_W:2 * GROUP_W]
    q_ref[...] = q * lax.rsqrt(_group_sum(q * q, HEAD_DIM) + EPS) * (HEAD_DIM ** -0.5)
    k_ref[...] = k * lax.rsqrt(_group_sum(k * k, HEAD_DIM) + EPS)
    v_ref[...] = y[:, 2 * GROUP_W:3 * GROUP_W]
    small = small_ref[...]
    beta = _sigmoid(small)
    alpha = jnp.exp(-jnp.exp(alog_ref[...]) * _softplus(small + dtb_ref[...]))
    lane = lax.broadcasted_iota(jnp.int32, (1, 128), 1)
    ab_ref[...] = jnp.where(lane < N_HEADS, beta, alpha)


def _gdn_prep(gdn, conv_w, a_log, dt_bias):
    B, S, _ = gdn.shape
    ts = min(512, S)
    pad = jnp.zeros((N_HEADS,), F32)
    alog = jnp.concatenate([pad, a_log, jnp.zeros((120,), F32)]).reshape(1, 128)
    dtb = jnp.concatenate([pad, dt_bias, jnp.zeros((120,), F32)]).reshape(1, 128)
    tok = jax.ShapeDtypeStruct((B, S, GROUP_W), F32)
    return pl.pallas_call(
        functools.partial(_gdn_prep_kernel, ts=ts),
        name="gdn_prep",
        grid=(B, S // ts),
        in_specs=[
            pl.BlockSpec((None, ts, 3 * GROUP_W), lambda b, i: (b, i, 0)),
            pl.BlockSpec((None, 8, 3 * GROUP_W), lambda b, i: (b, jnp.maximum(i * (ts // 8) - 1, 0), 0)),
            pl.BlockSpec((None, ts, 128), lambda b, i: (b, i, 8)),
            pl.BlockSpec((CONV_K, 3 * GROUP_W), lambda b, i: (0, 0)),
            pl.BlockSpec((1, 128), lambda b, i: (0, 0)),
            pl.BlockSpec((1, 128), lambda b, i: (0, 0)),
        ],
        out_specs=[
            pl.BlockSpec((None, ts, GROUP_W), lambda b, i: (b, i, 0)),
            pl.BlockSpec((None, ts, GROUP_W), lambda b, i: (b, i, 0)),
            pl.BlockSpec((None, ts, GROUP_W), lambda b, i: (b, i, 0)),
            pl.BlockSpec((None, ts, 128), lambda b, i: (b, i, 0)),
        ],
        out_shape=[tok, tok, tok, jax.ShapeDtypeStruct((B, S, 128), F32)],
        scratch_shapes=[pltpu.VMEM((ts + 8, 3 * GROUP_W), F32)],
        compiler_params=_cparams(("parallel", "parallel")),
    )(gdn, gdn, gdn, conv_w, alog, dtb)


def _gdn_scan_kernel(q_ref, k_ref, v_ref, al_ref, be_ref, kn_ref, o_ref, s_ref, ks_ref, *, tt):
    @pl.when(pl.program_id(0) == 0)
    def _():
        s_ref[...] = jnp.zeros_like(s_ref)
        ks_ref[...] = jnp.zeros_like(ks_ref)

    def step(t, ks, k_next):
        al = al_ref[pl.ds(t, 1), :]
        d = be_ref[pl.ds(t, 1), :] * (v_ref[t] - al * ks)
        o_parts = [None] * 2
        n_parts = [None] * 2
        for kk in range(HEAD_DIM):
            s_new = al * s_ref[kk] + k_ref[t, pl.ds(kk, 1), :] * d
            s_ref[kk] = s_new
            t_o = s_new * q_ref[t, pl.ds(kk, 1), :]
            t_n = s_new * k_next(kk)
            o_parts[kk % 2] = t_o if o_parts[kk % 2] is None else o_parts[kk % 2] + t_o
            n_parts[kk % 2] = t_n if n_parts[kk % 2] is None else n_parts[kk % 2] + t_n
        o_ref[t] = o_parts[0] + o_parts[1]
        return n_parts[0] + n_parts[1]

    ks = lax.fori_loop(0, tt - 1,
                       lambda t, ks: step(t, ks, lambda kk: k_ref[t + 1, pl.ds(kk, 1), :]),
                       ks_ref[...])
    ks_ref[...] = step(tt - 1, ks, lambda kk: kn_ref[0, pl.ds(kk, 1), :])


def _next_row_spec(tt, S, N, Ln):
    return pl.BlockSpec((1, N, Ln), lambda i: (jnp.minimum((i + 1) * tt, S - 1), 0, 0))


def _gdn_scan(q, k, v, al, be):
    S, N, Ln = q.shape
    tt = min(32, S)
    seq = pl.BlockSpec((tt, N, Ln), lambda i: (i, 0, 0))
    vec = pl.BlockSpec((tt, Ln), lambda i: (i, 0))
    return pl.pallas_call(
        functools.partial(_gdn_scan_kernel, tt=tt),
        name="gdn_scan",
        grid=(S // tt,),
        in_specs=[seq, seq, seq, vec, vec, _next_row_spec(tt, S, N, Ln)],
        out_specs=seq,
        out_shape=jax.ShapeDtypeStruct((S, N, Ln), F32),
        scratch_shapes=[pltpu.VMEM((N, N, Ln), F32), pltpu.VMEM((N, Ln), F32)],
        compiler_params=_cparams(("arbitrary",)),
    )(q, k, v, al, be, k)


def _gdn_post_kernel(o_ref, gate_ref, g_ref, out_ref):
    o = o_ref[...]
    gate = gate_ref[...]
    ms = _group_sum(o * o, HEAD_DIM) * (1.0 / HEAD_DIM)
    out_ref[...] = (o * lax.rsqrt(ms + EPS) * g_ref[...] * (gate * _sigmoid(gate))).astype(BF16)


def _gdn_post(o, gdn, norm_g):
    B, S, _ = o.shape
    ts = min(512, S)
    return pl.pallas_call(
        _gdn_post_kernel,
        name="gdn_post",
        grid=(B, S // ts),
        in_specs=[
            pl.BlockSpec((None, ts, GROUP_W), lambda b, i: (b, i, 0)),
            pl.BlockSpec((None, ts, GROUP_W), lambda b, i: (b, i, 3)),
            pl.BlockSpec((1, GROUP_W), lambda b, i: (0, 0)),
        ],
        out_specs=pl.BlockSpec((None, ts, GROUP_W), lambda b, i: (b, i, 0)),
        out_shape=jax.ShapeDtypeStruct((B, S, GROUP_W), BF16),
        compiler_params=_cparams(("parallel", "parallel")),
    )(o, gdn, jnp.tile(norm_g, N_HEADS).reshape(1, GROUP_W))


def _rwkv_prep_kernel(*refs, ts, first):
    if first:
        (x_ref, prev_ref, mu_ref, vecs_ref, wup_ref, aup_ref, gup_ref,
         r_ref, w_ref, k_ref, v_ref, a_ref, b_ref, g_ref, vf_ref, xs_ref) = refs
    else:
        (x_ref, prev_ref, mu_ref, vecs_ref, wup_ref, aup_ref, gup_ref, vup_ref, vfin_ref,
         r_ref, w_ref, k_ref, v_ref, a_ref, b_ref, g_ref, xs_ref) = refs
    i = pl.program_id(1)
    prev = prev_ref[...]
    xs_ref[0:8, :] = jnp.where(i == 0, jnp.zeros_like(prev), prev)
    x = x_ref[...]
    xs_ref[8:8 + ts, :] = x
    xm = x + (xs_ref[pl.ds(7, ts), :] - x) * mu_ref[...]
    r = xm[:, 0:256]
    k = xm[:, 256:512]
    v = xm[:, 512:768]
    wd = xm[:, 768:896]
    ad = xm[:, 896:1024]
    gd = xm[:, 1024:1280]
    w0, a0, k_k, k_a = vecs_ref[0:1, :], vecs_ref[1:2, :], vecs_ref[2:3, :], vecs_ref[3:4, :]
    dot = functools.partial(jnp.dot, precision=HIGHEST, preferred_element_type=F32)
    w_log = -_softplus(-(w0 + dot(jnp.tanh(wd), wup_ref[...]))) - 0.5
    a = _sigmoid(a0 + dot(ad, aup_ref[...]))
    g_ref[...] = dot(_sigmoid(gd), gup_ref[...])
    if first:
        vf_ref[...] = v
    else:
        vd = xm[:, 1280:1408]
        v = v + (vfin_ref[...] - v) * _sigmoid(vecs_ref[4:5, :] + dot(vd, vup_ref[...]))
    kk = k * k_k
    kk = kk * lax.rsqrt(_group_sum(kk * kk, HEAD_DIM) + EPS)
    r_ref[...] = r
    w_ref[...] = jnp.exp(-jnp.exp(w_log))
    k_ref[...] = k * (1.0 + (a - 1.0) * k_a)
    v_ref[...] = v
    a_ref[...] = -kk
    b_ref[...] = kk * a


def _pad_rows(w, rows):
    return jnp.concatenate([w, jnp.zeros((rows - w.shape[0], w.shape[1]), w.dtype)], axis=0)


def _rwkv_prep(rw, mu, vecs, w_up, a_up, g_up, vmix_up, v_first):
    B, S, _ = rw.shape
    ts = min(256, S)
    first = v_first is None
    tok_spec = pl.BlockSpec((None, ts, GROUP_W), lambda b, i: (b, i, 0))
    const = lambda shape: pl.BlockSpec(shape, lambda b, i: (0,) * len(shape))
    in_specs = [
        pl.BlockSpec((None, ts, RWKV_W), lambda b, i: (b, i, 0)),
        pl.BlockSpec((None, 8, RWKV_W), lambda b, i: (b, jnp.maximum(i * (ts // 8) - 1, 0), 0)),
        const((1, RWKV_W)), const((8, GROUP_W)),
        const((128, GROUP_W)), const((128, GROUP_W)), const((256, GROUP_W)),
    ]
    args = [rw, rw, mu, vecs, _pad_rows(w_up, 128), _pad_rows(a_up, 128), _pad_rows(g_up, 256)]
    if not first:
        in_specs += [const((128, GROUP_W)), tok_spec]
        args += [_pad_rows(vmix_up, 128), v_first]
    tok = jax.ShapeDtypeStruct((B, S, GROUP_W), F32)
    n_out = 8 if first else 7
    return pl.pallas_call(
        functools.partial(_rwkv_prep_kernel, ts=ts, first=first),
        name="rwkv_prep",
        grid=(B, S // ts),
        in_specs=in_specs,
        out_specs=[tok_spec] * n_out,
        out_shape=[tok] * n_out,
        scratch_shapes=[pltpu.VMEM((ts + 8, RWKV_W), F32)],
        compiler_params=_cparams(("parallel", "parallel")),
    )(*args)


def _wkv_scan_kernel(r_ref, w_ref, k_ref, v_ref, a_ref, b_ref, an_ref, y_ref, s_ref, sa_ref, *, tt):
    @pl.when(pl.program_id(0) == 0)
    def _():
        s_ref[...] = jnp.zeros_like(s_ref)
        sa_ref[...] = jnp.zeros_like(sa_ref)

    def step(t, sa, a_next):
        vt = v_ref[t]
        y_parts = [None] * 2
        n_parts = [None] * 2
        for j in range(HEAD_DIM):
            s_new = (s_ref[j] * w_ref[t, pl.ds(j, 1), :]
                     + (sa * b_ref[t, pl.ds(j, 1), :] + vt * k_ref[t, pl.ds(j, 1), :]))
            s_ref[j] = s_new
            t_y = s_new * r_ref[t, pl.ds(j, 1), :]
            t_n = s_new * a_next(j)
            y_parts[j % 2] = t_y if y_parts[j % 2] is None else y_parts[j % 2] + t_y
            n_parts[j % 2] = t_n if n_parts[j % 2] is None else n_parts[j % 2] + t_n
        y_ref[t] = y_parts[0] + y_parts[1]
        return n_parts[0] + n_parts[1]

    sa = lax.fori_loop(0, tt - 1,
                       lambda t, sa: step(t, sa, lambda j: a_ref[t + 1, pl.ds(j, 1), :]),
                       sa_ref[...])
    sa_ref[...] = step(tt - 1, sa, lambda j: an_ref[0, pl.ds(j, 1), :])


def _wkv_scan(r, w, k, v, a, b):
    S, N, Ln = r.shape
    tt = min(32, S)
    seq = pl.BlockSpec((tt, N, Ln), lambda i: (i, 0, 0))
    return pl.pallas_call(
        functools.partial(_wkv_scan_kernel, tt=tt),
        name="wkv_scan",
        grid=(S // tt,),
        in_specs=[seq] * 6 + [_next_row_spec(tt, S, N, Ln)],
        out_specs=seq,
        out_shape=jax.ShapeDtypeStruct((S, N, Ln), F32),
        scratch_shapes=[pltpu.VMEM((N, N, Ln), F32), pltpu.VMEM((N, Ln), F32)],
        compiler_params=_cparams(("arbitrary",)),
    )(r, w, k, v, a, b, a)


def _rwkv_post_kernel(y_ref, r_ref, k_ref, v_ref, g_ref, vecs_ref, out_ref):
    y = y_ref[...]
    inv_n = 1.0 / HEAD_DIM
    yc = y - _group_sum(y, HEAD_DIM) * inv_n
    var = _group_sum(yc * yc, HEAD_DIM) * inv_n
    yn = yc * lax.rsqrt(var + LNX_EPS) * vecs_ref[0:1, :] + vecs_ref[1:2, :]
    bonus = _group_sum(r_ref[...] * k_ref[...] * vecs_ref[2:3, :], HEAD_DIM)
    out_ref[...] = ((yn + bonus * v_ref[...]) * g_ref[...]).astype(BF16)


def _rwkv_post(y, r, k, v, g, vecs):
    B, S, _ = y.shape
    ts = min(512, S)
    tok_spec = pl.BlockSpec((None, ts, GROUP_W), lambda b, i: (b, i, 0))
    return pl.pallas_call(
        _rwkv_post_kernel,
        name="rwkv_post",
        grid=(B, S // ts),
        in_specs=[tok_spec] * 5 + [pl.BlockSpec((8, GROUP_W), lambda b, i: (0, 0))],
        out_specs=tok_spec,
        out_shape=jax.ShapeDtypeStruct((B, S, GROUP_W), BF16),
        compiler_params=_cparams(("parallel", "parallel")),
    )(y, r, k, v, g, vecs)


def _outproj_kernel(h_ref, mod_ref, oa_ref, ob_ref, oc_ref, od_ref, w_ref, out_ref):
    acc = jnp.dot(oa_ref[...], w_ref[0:GROUP_W, :], preferred_element_type=F32)
    acc += jnp.dot(ob_ref[...], w_ref[GROUP_W:2 * GROUP_W, :], preferred_element_type=F32)
    acc += jnp.dot(oc_ref[...], w_ref[2 * GROUP_W:3 * GROUP_W, :], preferred_element_type=F32)
    acc += jnp.dot(od_ref[...], w_ref[3 * GROUP_W:4 * GROUP_W, :], preferred_element_type=F32)
    out_ref[...] = h_ref[...] + mod_ref[5:6, :] * acc


def _outproj(h, mod_l, o_a, o_b, o_c, o_d, w_out, l, S):
    T, D = h.shape
    tm = min(1024, S)
    tpb = S // tm
    mix = pl.BlockSpec((tm, GROUP_W), lambda i: (i, 0))
    return pl.pallas_call(
        _outproj_kernel,
        name="outproj",
        grid=(T // tm,),
        in_specs=[
            pl.BlockSpec((tm, D), lambda i: (i, 0)),
            pl.BlockSpec((None, N_MOD, D), lambda i: (i // tpb, 0, 0)),
            mix, mix, mix, mix,
            pl.BlockSpec((None, D, D), lambda i: (l, 0, 0)),
        ],
        out_specs=pl.BlockSpec((tm, D), lambda i: (i, 0)),
        out_shape=jax.ShapeDtypeStruct((T, D), F32),
        compiler_params=_cparams(("parallel",)),
    )(h, mod_l, o_a, o_b, o_c, o_d, w_out)


def _pad_cols(w, width):
    pad = width - w.shape[-1]
    return jnp.concatenate([w, jnp.zeros(w.shape[:-1] + (pad,), w.dtype)], axis=-1)


def _rope_perm():
    p = np.arange(128)
    first = (p // 16) * DIFF_DIM + p % 16
    return np.concatenate([first, first + 16])


def _permute_w_in(w_in, vmix_down):
    L, D, _ = w_in.shape
    perm = _rope_perm()
    g0 = 9 * GROUP_W
    r0 = RWKV_START
    vd = jnp.concatenate([jnp.zeros((1, D, vmix_down.shape[-1]), w_in.dtype), vmix_down], axis=0)
    cols = [
        w_in[..., perm], w_in[..., GROUP_W + perm], w_in[..., 2 * GROUP_W:3 * GROUP_W],
        w_in[..., 3 * GROUP_W:6 * GROUP_W],
        w_in[..., 6 * GROUP_W:g0], w_in[..., g0 + 8:g0 + 8 + GROUP_W], _pad_cols(w_in[..., g0:g0 + 8], 128),
        w_in[..., r0:r0 + 768], _pad_cols(w_in[..., r0 + 768:r0 + 832], 128),
        _pad_cols(w_in[..., r0 + 832:r0 + 896], 128), _pad_cols(w_in[..., r0 + 896:r0 + 1056], 256),
        _pad_cols(vd, 128),
    ]
    return jnp.concatenate(cols, axis=-1).astype(BF16)


def _permute_mu(mu, vmix_mu):
    L = mu.shape[0]
    vm = jnp.concatenate([jnp.zeros((1, vmix_mu.shape[-1]), mu.dtype), vmix_mu], axis=0)
    cols = [mu[:, 0:768], _pad_cols(mu[:, 768:832], 128), _pad_cols(mu[:, 832:896], 128),
            _pad_cols(mu[:, 896:1056], 256), _pad_cols(vm, 128)]
    return jnp.concatenate(cols, axis=-1).reshape(L, 1, RWKV_W)


def _to_lanes(x):
    B, S, _ = x.shape
    return x.reshape(B, S, N_HEADS, HEAD_DIM).transpose(1, 3, 0, 2).reshape(S, HEAD_DIM, B * N_HEADS)


def _from_lanes(y, B):
    S = y.shape[0]
    return y.reshape(S, HEAD_DIM, B, N_HEADS).transpose(2, 0, 3, 1).reshape(B, S, GROUP_W)


def _head_scalars_to_lanes(x):
    B, S, _ = x.shape
    return x.transpose(1, 0, 2).reshape(S, B * N_HEADS)


def kernel(x, c, ada_w, ada_b, norm_g, ffn_w13, ffn_w2, w_in, w_out, diff_qk_g, diff_lambda, diff_subln_g, gdn_conv_w, gdn_a_log, gdn_dt_bias, gdn_norm_g, rwkv_mu, rwkv_w0, rwkv_w_up, rwkv_a0, rwkv_a_up, rwkv_g_up, rwkv_k_k, rwkv_k_a, rwkv_r_k, rwkv_lnx_w, rwkv_lnx_b, rwkv_vmix_down, rwkv_vmix_mu, rwkv_v0, rwkv_vmix_up):
    B, S, D = x.shape
    L = ada_w.shape[0]
    T = B * S

    mod = _mod_all(c, ada_w, ada_b)
    w13 = ffn_w13.astype(BF16)
    w2 = ffn_w2.astype(BF16)
    w_perm = _permute_w_in(w_in, rwkv_vmix_down)
    w_o = w_out.astype(BF16)
    mu = _permute_mu(rwkv_mu, rwkv_vmix_mu)

    inv = ROPE_THETA ** (-jnp.arange(0, DIFF_DIM, 2, dtype=F32) / DIFF_DIM)
    ang = jnp.arange(S, dtype=F32)[:, None] * inv[None, :]
    cos128 = jnp.tile(jnp.cos(ang), (1, 8))
    sin128 = jnp.tile(jnp.sin(ang), (1, 8))

    h = x.reshape(T, D)
    v_first = None
    for l in range(L):
        lam_init = 0.8 - 0.6 * math.exp(-0.3 * l)
        h = _ffn(h, mod[l], norm_g[l, 0], w13, w2, l, 0, 0, S)

        qkg = jnp.stack([jnp.tile(diff_qk_g[l, 0, :16], 8), jnp.tile(diff_qk_g[l, 0, 16:], 8),
                         jnp.tile(diff_qk_g[l, 1, :16], 8), jnp.tile(diff_qk_g[l, 1, 16:], 8)])
        attn, gdn, rw = _inproj(h, mod[l], norm_g[l, 1], w_perm, l, qkg, cos128, sin128, S)
        attn = attn.reshape(B, S, ATTN_W)
        gdn = gdn.reshape(B, S, GDN_W)
        rw = rw.reshape(B, S, RWKV_W)

        o_a = _diff_attention(attn, diff_lambda[l], diff_subln_g[l], lam_init)
        o_b = _sb_attention(attn)

        gq, gk, gv, gab = _gdn_prep(gdn, gdn_conv_w[l], gdn_a_log[l], gdn_dt_bias[l])
        go = _gdn_scan(_to_lanes(gq), _to_lanes(gk), _to_lanes(gv),
                       _head_scalars_to_lanes(gab[..., N_HEADS:2 * N_HEADS]),
                       _head_scalars_to_lanes(gab[..., 0:N_HEADS]))
        o_c = _gdn_post(_from_lanes(go, B), gdn, gdn_norm_g[l])

        zeros = jnp.zeros((GROUP_W,), F32)
        vecs = jnp.stack([rwkv_w0[l], rwkv_a0[l], rwkv_k_k[l], rwkv_k_a[l],
                          rwkv_v0[l - 1] if l > 0 else zeros, zeros, zeros, zeros])
        outs = _rwkv_prep(rw, mu[l], vecs, rwkv_w_up[l], rwkv_a_up[l], rwkv_g_up[l],
                          rwkv_vmix_up[l - 1] if l > 0 else None, v_first)
        rr, rw_, rk, rv, ra, rb, rg = outs[:7]
        if l == 0:
            v_first = outs[7]
        ry = _wkv_scan(*[_to_lanes(t) for t in (rr, rw_, rk, rv, ra, rb)])
        pvecs = jnp.stack([rwkv_lnx_w[l], rwkv_lnx_b[l], rwkv_r_k[l].reshape(GROUP_W),
                           zeros, zeros, zeros, zeros, zeros])
        o_d = _rwkv_post(_from_lanes(ry, B), rr, rk, rv, rg, pvecs)

        flat = lambda t: t.reshape(T, GROUP_W)
        h = _outproj(h, mod[l], flat(o_a), flat(o_b), flat(o_c), flat(o_d), w_o, l, S)
        h = _ffn(h, mod[l], norm_g[l, 2], w13, w2, l, 1, 6, S)
    return h.reshape(B, S, D)
```

```python
import functools
import math

import numpy as np
import jax
import jax.numpy as jnp
from jax import lax
from jax.experimental import pallas as pl
from jax.experimental.pallas import tpu as pltpu

F32 = jnp.float32
BF16 = jnp.bfloat16
HIGHEST = lax.Precision.HIGHEST

D_MODEL = 1024
N_HEADS = 4
HEAD_DIM = 64
GROUP_W = N_HEADS * HEAD_DIM
DIFF_DIM = HEAD_DIM // 2
D_FF = 2816
N_MOD = 9
EPS = 1e-6
LNX_EPS = 64e-5
ROPE_THETA = 10000.0
LOG2E = 1.4426950408889634
CONV_K = 4
RWKV_START = 9 * GROUP_W + 2 * N_HEADS + GROUP_W

ATTN_W = 6 * GROUP_W
GDN_W = 3 * GROUP_W + GROUP_W + 128
RWKV_W = 3 * GROUP_W + 128 + 128 + 256 + 128
PROJ_W = ATTN_W + GDN_W + RWKV_W
VMEM_LIMIT = 56 * 1024 * 1024


def _cparams(sem):
    return pltpu.CompilerParams(dimension_semantics=sem, vmem_limit_bytes=VMEM_LIMIT)


def _sigmoid(x):
    return 1.0 / (1.0 + jnp.exp(-x))


def _softplus(x):
    return jnp.maximum(x, 0.0) + jnp.log(1.0 + jnp.exp(-jnp.abs(x)))


def _group_ones(n, group):
    r = lax.broadcasted_iota(jnp.int32, (n, n), 0) // group
    c = lax.broadcasted_iota(jnp.int32, (n, n), 1) // group
    return (r == c).astype(BF16)


def _group_sum(x, group):
    ones = _group_ones(x.shape[-1], group)
    hi = x.astype(BF16)
    lo = (x - hi.astype(F32)).astype(BF16)
    return (jnp.dot(hi, ones, preferred_element_type=F32)
            + jnp.dot(lo, ones, preferred_element_type=F32))


def _modulate(x, gain, shift, scale):
    ms = jnp.mean(x * x, axis=-1, keepdims=True)
    return (x * lax.rsqrt(ms + EPS) * gain) * (1.0 + scale) + shift


def _mod_kernel(c_ref, w_ref, b_ref, o_ref):
    c = c_ref[...]
    cond = c * _sigmoid(c)
    o_ref[...] = jnp.dot(cond, w_ref[...], precision=HIGHEST,
                         preferred_element_type=F32) + b_ref[...]


def _mod_all(c, ada_w, ada_b):
    L, D, N = ada_w.shape
    B = c.shape[0]
    tn = 1024
    out = pl.pallas_call(
        _mod_kernel,
        name="adaln_mod",
        grid=(L, N // tn),
        in_specs=[
            pl.BlockSpec((B, D), lambda l, n: (0, 0)),
            pl.BlockSpec((None, D, tn), lambda l, n: (l, 0, n)),
            pl.BlockSpec((None, 1, tn), lambda l, n: (l, 0, n)),
        ],
        out_specs=pl.BlockSpec((None, B, tn), lambda l, n: (l, 0, n)),
        out_shape=jax.ShapeDtypeStruct((L, B, N), F32),
        compiler_params=_cparams(("parallel", "parallel")),
    )(c, ada_w, ada_b.reshape(L, 1, N))
    return out.reshape(L, B, N_MOD, D)


def _ffn_kernel(x_ref, mod_ref, g_ref, w13_ref, w2_ref, o_ref, act_ref, *, mb, tf):
    hn = _modulate(x_ref[...], g_ref[...], mod_ref[mb:mb + 1, :], mod_ref[mb + 1:mb + 2, :])
    hn = hn.astype(BF16)
    for f in range(D_FF // tf):
        gate = jnp.dot(hn, w13_ref[:, f * tf:(f + 1) * tf], preferred_element_type=F32)
        up = jnp.dot(hn, w13_ref[:, D_FF + f * tf:D_FF + (f + 1) * tf], preferred_element_type=F32)
        act_ref[:, f * tf:(f + 1) * tf] = (gate * _sigmoid(gate) * up).astype(BF16)
    acc = jnp.dot(act_ref[...], w2_ref[...], preferred_element_type=F32)
    o_ref[...] = x_ref[...] + 0.5 * mod_ref[mb + 2:mb + 3, :] * acc


def _ffn(h, mod_l, gain, w13, w2, l, j, mb, S):
    T, D = h.shape
    tm = min(1024, S)
    tf = 256
    tpb = S // tm
    resident = dict(pipeline_mode=pl.Buffered(1))
    return pl.pallas_call(
        functools.partial(_ffn_kernel, mb=mb, tf=tf),
        name="ffn",
        grid=(T // tm,),
        in_specs=[
            pl.BlockSpec((tm, D), lambda i: (i, 0)),
            pl.BlockSpec((None, N_MOD, D), lambda i: (i // tpb, 0, 0)),
            pl.BlockSpec((1, D), lambda i: (0, 0)),
            pl.BlockSpec((None, None, D, 2 * D_FF), lambda i: (l, j, 0, 0), **resident),
            pl.BlockSpec((None, None, D_FF, D), lambda i: (l, j, 0, 0), **resident),
        ],
        out_specs=pl.BlockSpec((tm, D), lambda i: (i, 0)),
        out_shape=jax.ShapeDtypeStruct((T, D), F32),
        scratch_shapes=[pltpu.VMEM((tm, D_FF), BF16)],
        compiler_params=_cparams(("parallel",)),
    )(h, mod_l, gain.reshape(1, D), w13, w2)


def _inproj_kernel(x_ref, mod_ref, g_ref, w_ref, qkg_ref, cos_ref, sin_ref,
                   attn_ref, gdn_ref, rwkv_ref):
    hn = _modulate(x_ref[...], g_ref[...], mod_ref[3:4, :], mod_ref[4:5, :]).astype(BF16)

    pa = jnp.dot(hn, w_ref[:, 0:3 * GROUP_W], preferred_element_type=F32)
    cos = cos_ref[...]
    sin = sin_ref[...]
    for idx, scale in ((0, DIFF_DIM ** -0.5 * LOG2E), (1, 1.0)):
        x1 = pa[:, idx * 256:idx * 256 + 128]
        x2 = pa[:, idx * 256 + 128:idx * 256 + 256]
        ms = _group_sum(x1 * x1 + x2 * x2, 16) * (1.0 / DIFF_DIM)
        inv = lax.rsqrt(ms + EPS)
        n1 = x1 * inv * qkg_ref[2 * idx:2 * idx + 1, :]
        n2 = x2 * inv * qkg_ref[2 * idx + 1:2 * idx + 2, :]
        attn_ref[:, idx * 256:idx * 256 + 128] = ((n1 * cos - n2 * sin) * scale).astype(BF16)
        attn_ref[:, idx * 256 + 128:idx * 256 + 256] = ((n2 * cos + n1 * sin) * scale).astype(BF16)
    attn_ref[:, 512:768] = pa[:, 512:768].astype(BF16)

    pb = jnp.dot(hn, w_ref[:, 3 * GROUP_W:ATTN_W], preferred_element_type=F32)
    attn_ref[:, 768:1024] = (pb[:, 0:256] * (HEAD_DIM ** -0.5)).astype(BF16)
    attn_ref[:, 1024:1536] = pb[:, 256:768].astype(BF16)

    gdn_ref[...] = jnp.dot(hn, w_ref[:, ATTN_W:ATTN_W + GDN_W], preferred_element_type=F32)
    rwkv_ref[...] = jnp.dot(hn, w_ref[:, ATTN_W + GDN_W:PROJ_W], preferred_element_type=F32)


def _inproj(h, mod_l, gain, w_perm, l, qkg, cos128, sin128, S):
    T, D = h.shape
    tm = min(512, S)
    tpb = S // tm
    return pl.pallas_call(
        _inproj_kernel,
        name="inproj",
        grid=(T // tm,),
        in_specs=[
            pl.BlockSpec((tm, D), lambda i: (i, 0)),
            pl.BlockSpec((None, N_MOD, D), lambda i: (i // tpb, 0, 0)),
            pl.BlockSpec((1, D), lambda i: (0, 0)),
            pl.BlockSpec((None, D, PROJ_W), lambda i: (l, 0, 0)),
            pl.BlockSpec((4, 128), lambda i: (0, 0)),
            pl.BlockSpec((tm, 128), lambda i: (i % tpb, 0)),
            pl.BlockSpec((tm, 128), lambda i: (i % tpb, 0)),
        ],
        out_specs=[
            pl.BlockSpec((tm, ATTN_W), lambda i: (i, 0)),
            pl.BlockSpec((tm, GDN_W), lambda i: (i, 0)),
            pl.BlockSpec((tm, RWKV_W), lambda i: (i, 0)),
        ],
        out_shape=[
            jax.ShapeDtypeStruct((T, ATTN_W), BF16),
            jax.ShapeDtypeStruct((T, GDN_W), F32),
            jax.ShapeDtypeStruct((T, RWKV_W), F32),
        ],
        compiler_params=_cparams(("parallel",)),
    )(h, mod_l, gain.reshape(1, D), w_perm, qkg, cos128, sin128)


def _diff_kernel(q_ref, k_ref, vt_ref, lamv_ref, g_ref, o_ref, qc_ref, acc_ref,
                 *, lam_init, tq, tk):
    i = pl.program_id(1)
    q = q_ref[...]
    lane = lax.broadcasted_iota(jnp.int32, (1, GROUP_W), 1)
    sub = (lane % 128) // 16
    for h in range(N_HEADS):
        qc_ref[h, 0:tq, :] = jnp.where(sub == 2 * h, q, jnp.zeros_like(q))
        qc_ref[h, tq:2 * tq, :] = jnp.where(sub == 2 * h + 1, q, jnp.zeros_like(q))
    acc_ref[...] = jnp.zeros_like(acc_ref)

    def tile(kt, carry, masked):
        start = pl.multiple_of(kt * tk, tk)
        k = k_ref[pl.ds(start, tk), :]
        heads = range(N_HEADS)
        ss = [lax.dot_general(k, qc_ref[h], (((1,), (1,)), ((), ())), preferred_element_type=F32)
              for h in heads]
        out, ps, alphas = [], [], []
        for h in heads:
            m, l = carry[h]
            s = ss[h]
            if masked:
                key = kt * tk + lax.broadcasted_iota(jnp.int32, (tk, 2 * tq), 0)
                qpos = i * tq + lax.broadcasted_iota(jnp.int32, (tk, 2 * tq), 1) % tq
                s = jnp.where(key <= qpos, s, -jnp.inf)
            m_new = jnp.maximum(m, jnp.max(s, axis=0, keepdims=True))
            alpha = jnp.exp2(m - m_new)
            p = jnp.exp2(s - m_new)
            out.append((m_new, alpha * l + jnp.sum(p, axis=0, keepdims=True)))
            ps.append(p.astype(BF16))
            alphas.append(alpha)
        for h in heads:
            vt = vt_ref[kt, h * HEAD_DIM:(h + 1) * HEAD_DIM, :]
            acc_ref[h] = alphas[h] * acc_ref[h] + jnp.dot(vt, ps[h], preferred_element_type=F32)
        return tuple(out)

    init = tuple((jnp.full((1, 2 * tq), -jnp.inf, F32), jnp.zeros((1, 2 * tq), F32))
                 for _ in range(N_HEADS))
    kd = (i * tq) // tk
    carry = lax.fori_loop(0, kd, lambda kt, c: tile(kt, c, False), init)
    carry = tile(kd, carry, True)

    lv = lamv_ref[...]
    lam = (jnp.exp(jnp.sum(lv[0:1, :] * lv[1:2, :], axis=1, keepdims=True))
           - jnp.exp(jnp.sum(lv[2:3, :] * lv[3:4, :], axis=1, keepdims=True)) + lam_init)
    rows = []
    for h in range(N_HEADS):
        o = acc_ref[h] / carry[h][1]
        oh = o[:, :tq] - lam * o[:, tq:]
        ms = jnp.mean(oh * oh, axis=0, keepdims=True)
        rows.append(oh * lax.rsqrt(ms + EPS) * g_ref[h * HEAD_DIM:(h + 1) * HEAD_DIM, :]
                    * (1.0 - lam_init))
    o_ref[...] = jnp.concatenate(rows, axis=0).T.astype(BF16)


def _kv_tiles_transposed(v, tk):
    B, S, C = v.shape
    return v.reshape(B, S // tk, tk, C).transpose(0, 1, 3, 2)


def _diff_attention(attn, lamv, subln_g, lam_init):
    B, S, _ = attn.shape
    tq = min(256, S)
    tk = min(256, S)
    vt = _kv_tiles_transposed(attn[..., 2 * GROUP_W:3 * GROUP_W], tk)
    gain = jnp.broadcast_to(jnp.tile(subln_g, N_HEADS)[:, None], (GROUP_W, tq))
    return pl.pallas_call(
        functools.partial(_diff_kernel, lam_init=lam_init, tq=tq, tk=tk),
        name="diff_attn",
        grid=(B, S // tq),
        in_specs=[
            pl.BlockSpec((None, tq, GROUP_W), lambda b, i: (b, i, 0)),
            pl.BlockSpec((None, S, GROUP_W), lambda b, i: (b, 0, 1)),
            pl.BlockSpec((None, S // tk, GROUP_W, tk), lambda b, i: (b, 0, 0, 0)),
            pl.BlockSpec((4, DIFF_DIM), lambda b, i: (0, 0)),
            pl.BlockSpec((GROUP_W, tq), lambda b, i: (0, 0)),
        ],
        out_specs=pl.BlockSpec((None, tq, GROUP_W), lambda b, i: (b, i, 0)),
        out_shape=jax.ShapeDtypeStruct((B, S, GROUP_W), BF16),
        scratch_shapes=[pltpu.VMEM((N_HEADS, 2 * tq, GROUP_W), BF16),
                        pltpu.VMEM((N_HEADS, HEAD_DIM, 2 * tq), F32)],
        compiler_params=_cparams(("parallel", "parallel")),
    )(attn, attn, vt, lamv, gain)


def _sb_kernel(q_ref, k_ref, vt_ref, o_ref, qh_ref, tri_ref, acc_ref, *, t):
    i = pl.program_id(1)
    q = q_ref[...]
    lane = lax.broadcasted_iota(jnp.int32, (1, GROUP_W), 1)
    r_i = lax.broadcasted_iota(jnp.int32, (t, t), 0)
    c_i = lax.broadcasted_iota(jnp.int32, (t, t), 1)
    tri = (c_i > r_i).astype(BF16)
    tri_ref[:, 0:t] = tri
    tri_ref[:, t:2 * t] = tri
    for h in range(N_HEADS):
        qh_ref[h] = jnp.where(lane // HEAD_DIM == h, q, jnp.zeros_like(q))
    acc_ref[...] = jnp.zeros_like(acc_ref)

    def tile(kt, carry, diag):
        start = pl.multiple_of(kt * t, t)
        k = k_ref[pl.ds(start, t), :]
        heads = range(N_HEADS)
        strict = r_i < c_i
        zs = [lax.dot_general(k, qh_ref[h], (((1,), (1,)), ((), ())), preferred_element_type=F32)
              for h in heads]
        lss, hilos, sums = [], [], []
        for h in heads:
            z = zs[h]
            neg_abs = lax.bitcast_convert_type(
                lax.bitcast_convert_type(z, jnp.uint32) | jnp.uint32(0x80000000), F32)
            ls = jnp.minimum(z, 0.0) - jnp.log(1.0 + jnp.exp(neg_abs))
            lk = ls - z
            if diag:
                lk = jnp.where(strict, lk, 0.0)
            hi = lk.astype(BF16)
            lo = (lk - hi.astype(F32)).astype(BF16)
            lss.append(ls)
            hilos.append(jnp.concatenate([hi, lo], axis=0))
            sums.append(jnp.sum(lk, axis=0, keepdims=True))
        bts = [jnp.dot(tri_ref[...], hilos[h], preferred_element_type=F32) for h in heads]
        pas = []
        for h in heads:
            a = jnp.exp(lss[h] + (bts[h] + carry[h]))
            if diag:
                a = jnp.where(strict, a, 0.0)
            pas.append(a.astype(BF16))
        for h in heads:
            vt = vt_ref[kt, h * HEAD_DIM:(h + 1) * HEAD_DIM, :]
            acc_ref[h] += jnp.dot(vt, pas[h], preferred_element_type=F32)
        return tuple(carry[h] + sums[h] for h in heads)

    carry = tile(i, tuple(jnp.zeros((1, t), F32) for _ in range(N_HEADS)), True)
    lax.fori_loop(0, i, lambda n, c: tile(i - 1 - n, c, False), carry)
    o_ref[...] = jnp.concatenate([acc_ref[h] for h in range(N_HEADS)], axis=0).T.astype(BF16)


def _sb_attention(attn):
    B, S, _ = attn.shape
    t = min(256, S)
    vt = _kv_tiles_transposed(attn[..., 5 * GROUP_W:6 * GROUP_W], t)
    return pl.pallas_call(
        functools.partial(_sb_kernel, t=t),
        name="sb_attn",
        grid=(B, S // t),
        in_specs=[
            pl.BlockSpec((None, t, GROUP_W), lambda b, i: (b, i, 3)),
            pl.BlockSpec((None, S, GROUP_W), lambda b, i: (b, 0, 4)),
            pl.BlockSpec((None, S // t, GROUP_W, t), lambda b, i: (b, 0, 0, 0)),
        ],
        out_specs=pl.BlockSpec((None, t, GROUP_W), lambda b, i: (b, i, 0)),
        out_shape=jax.ShapeDtypeStruct((B, S, GROUP_W), BF16),
        scratch_shapes=[pltpu.VMEM((N_HEADS, t, GROUP_W), BF16),
                        pltpu.VMEM((t, 2 * t), BF16),
                        pltpu.VMEM((N_HEADS, HEAD_DIM, t), F32)],
        compiler_params=_cparams(("parallel", "parallel")),
    )(attn, attn, vt)


def _gdn_prep_kernel(x_ref, prev_ref, small_ref, cw_ref, alog_ref, dtb_ref,
                     q_ref, k_ref, v_ref, ab_ref, xs_ref, *, ts):
    i = pl.program_id(1)
    prev = prev_ref[...]
    xs_ref[0:8, :] = jnp.where(i == 0, jnp.zeros_like(prev), prev)
    xs_ref[8:8 + ts, :] = x_ref[...]
    conv = jnp.zeros((ts, 3 * GROUP_W), F32)
    for j in range(CONV_K):
        conv = conv + xs_ref[pl.ds(8 - (CONV_K - 1) + j, ts), :] * cw_ref[j:j + 1, :]
    y = conv * _sigmoid(conv)
    q = y[:, 0:GROUP_W]
    k = y[:, GROUP_W:2 * GROUP_W]
    q_ref[...] = q * lax.rsqrt(_group_sum(q * q, HEAD_DIM) + EPS) * (HEAD_DIM ** -0.5)
    k_ref[...] = k * lax.rsqrt(_group_sum(k * k, HEAD_DIM) + EPS)
    v_ref[...] = y[:, 2 * GROUP_W:3 * GROUP_W]
    small = small_ref[...]
    beta = _sigmoid(small)
    alpha = jnp.exp(-jnp.exp(alog_ref[...]) * _softplus(small + dtb_ref[...]))
    lane = lax.broadcasted_iota(jnp.int32, (1, 128), 1)
    ab_ref[...] = jnp.where(lane < N_HEADS, beta, alpha)


def _gdn_prep(gdn, conv_w, a_log, dt_bias):
    B, S, _ = gdn.shape
    ts = min(512, S)
    pad = jnp.zeros((N_HEADS,), F32)
    alog = jnp.concatenate([pad, a_log, jnp.zeros((120,), F32)]).reshape(1, 128)
    dtb = jnp.concatenate([pad, dt_bias, jnp.zeros((120,), F32)]).reshape(1, 128)
    tok = jax.ShapeDtypeStruct((B, S, GROUP_W), F32)
    return pl.pallas_call(
        functools.partial(_gdn_prep_kernel, ts=ts),
        name="gdn_prep",
        grid=(B, S // ts),
        in_specs=[
            pl.BlockSpec((None, ts, 3 * GROUP_W), lambda b, i: (b, i, 0)),
            pl.BlockSpec((None, 8, 3 * GROUP_W), lambda b, i: (b, jnp.maximum(i * (ts // 8) - 1, 0), 0)),
            pl.BlockSpec((None, ts, 128), lambda b, i: (b, i, 8)),
            pl.BlockSpec((CONV_K, 3 * GROUP_W), lambda b, i: (0, 0)),
            pl.BlockSpec((1, 128), lambda b, i: (0, 0)),
            pl.BlockSpec((1, 128), lambda b, i: (0, 0)),
        ],
        out_specs=[
            pl.BlockSpec((None, ts, GROUP_W), lambda b, i: (b, i, 0)),
            pl.BlockSpec((None, ts, GROUP_W), lambda b, i: (b, i, 0)),
            pl.BlockSpec((None, ts, GROUP_W), lambda b, i: (b, i, 0)),
            pl.BlockSpec((None, ts, 128), lambda b, i: (b, i, 0)),
        ],
        out_shape=[tok, tok, tok, jax.ShapeDtypeStruct((B, S, 128), F32)],
        scratch_shapes=[pltpu.VMEM((ts + 8, 3 * GROUP_W), F32)],
        compiler_params=_cparams(("parallel", "parallel")),
    )(gdn, gdn, gdn, conv_w, alog, dtb)


def _gdn_scan_kernel(q_ref, k_ref, v_ref, al_ref, be_ref, kn_ref, o_ref, s_ref, ks_ref, *, tt):
    @pl.when(pl.program_id(0) == 0)
    def _():
        s_ref[...] = jnp.zeros_like(s_ref)
        ks_ref[...] = jnp.zeros_like(ks_ref)

    def step(t, ks, k_next):
        al = al_ref[pl.ds(t, 1), :]
        d = be_ref[pl.ds(t, 1), :] * (v_ref[t] - al * ks)
        o_parts = [None] * 2
        n_parts = [None] * 2
        for kk in range(HEAD_DIM):
            s_new = al * s_ref[kk] + k_ref[t, pl.ds(kk, 1), :] * d
            s_ref[kk] = s_new
            t_o = s_new * q_ref[t, pl.ds(kk, 1), :]
            t_n = s_new * k_next(kk)
            o_parts[kk % 2] = t_o if o_parts[kk % 2] is None else o_parts[kk % 2] + t_o
            n_parts[kk % 2] = t_n if n_parts[kk % 2] is None else n_parts[kk % 2] + t_n
        o_ref[t] = o_parts[0] + o_parts[1]
        return n_parts[0] + n_parts[1]

    ks = lax.fori_loop(0, tt - 1,
                       lambda t, ks: step(t, ks, lambda kk: k_ref[t + 1, pl.ds(kk, 1), :]),
                       ks_ref[...])
    ks_ref[...] = step(tt - 1, ks, lambda kk: kn_ref[0, pl.ds(kk, 1), :])


def _next_row_spec(tt, S, N, Ln):
    return pl.BlockSpec((1, N, Ln), lambda i: (jnp.minimum((i + 1) * tt, S - 1), 0, 0))


def _gdn_scan(q, k, v, al, be):
    S, N, Ln = q.shape
    tt = min(32, S)
    seq = pl.BlockSpec((tt, N, Ln), lambda i: (i, 0, 0))
    vec = pl.BlockSpec((tt, Ln), lambda i: (i, 0))
    return pl.pallas_call(
        functools.partial(_gdn_scan_kernel, tt=tt),
        name="gdn_scan",
        grid=(S // tt,),
        in_specs=[seq, seq, seq, vec, vec, _next_row_spec(tt, S, N, Ln)],
        out_specs=seq,
        out_shape=jax.ShapeDtypeStruct((S, N, Ln), F32),
        scratch_shapes=[pltpu.VMEM((N, N, Ln), F32), pltpu.VMEM((N, Ln), F32)],
        compiler_params=_cparams(("arbitrary",)),
    )(q, k, v, al, be, k)


def _gdn_post_kernel(o_ref, gate_ref, g_ref, out_ref):
    o = o_ref[...]
    gate = gate_ref[...]
    ms = _group_sum(o * o, HEAD_DIM) * (1.0 / HEAD_DIM)
    out_ref[...] = (o * lax.rsqrt(ms + EPS) * g_ref[...] * (gate * _sigmoid(gate))).astype(BF16)


def _gdn_post(o, gdn, norm_g):
    B, S, _ = o.shape
    ts = min(512, S)
    return pl.pallas_call(
        _gdn_post_kernel,
        name="gdn_post",
        grid=(B, S // ts),
        in_specs=[
            pl.BlockSpec((None, ts, GROUP_W), lambda b, i: (b, i, 0)),
            pl.BlockSpec((None, ts, GROUP_W), lambda b, i: (b, i, 3)),
            pl.BlockSpec((1, GROUP_W), lambda b, i: (0, 0)),
        ],
        out_specs=pl.BlockSpec((None, ts, GROUP_W), lambda b, i: (b, i, 0)),
        out_shape=jax.ShapeDtypeStruct((B, S, GROUP_W), BF16),
        compiler_params=_cparams(("parallel", "parallel")),
    )(o, gdn, jnp.tile(norm_g, N_HEADS).reshape(1, GROUP_W))


def _rwkv_prep_kernel(*refs, ts, first):
    if first:
        (x_ref, prev_ref, mu_ref, vecs_ref, wup_ref, aup_ref, gup_ref,
         r_ref, w_ref, k_ref, v_ref, a_ref, b_ref, g_ref, vf_ref, xs_ref) = refs
    else:
        (x_ref, prev_ref, mu_ref, vecs_ref, wup_ref, aup_ref, gup_ref, vup_ref, vfin_ref,
         r_ref, w_ref, k_ref, v_ref, a_ref, b_ref, g_ref, xs_ref) = refs
    i = pl.program_id(1)
    prev = prev_ref[...]
    xs_ref[0:8, :] = jnp.where(i == 0, jnp.zeros_like(prev), prev)
    x = x_ref[...]
    xs_ref[8:8 + ts, :] = x
    xm = x + (xs_ref[pl.ds(7, ts), :] - x) * mu_ref[...]
    r = xm[:, 0:256]
    k = xm[:, 256:512]
    v = xm[:, 512:768]
    wd = xm[:, 768:896]
    ad = xm[:, 896:1024]
    gd = xm[:, 1024:1280]
    w0, a0, k_k, k_a = vecs_ref[0:1, :], vecs_ref[1:2, :], vecs_ref[2:3, :], vecs_ref[3:4, :]
    dot = lambda x, w: jnp.dot(x.astype(BF16), w, preferred_element_type=F32)
    w_log = -_softplus(-(w0 + dot(jnp.tanh(wd), wup_ref[...]))) - 0.5
    a = _sigmoid(a0 + dot(ad, aup_ref[...]))
    g_ref[...] = dot(_sigmoid(gd), gup_ref[...])
    if first:
        vf_ref[...] = v
    else:
        vd = xm[:, 1280:1408]
        v = v + (vfin_ref[...] - v) * _sigmoid(vecs_ref[4:5, :] + dot(vd, vup_ref[...]))
    kk = k * k_k
    kk = kk * lax.rsqrt(_group_sum(kk * kk, HEAD_DIM) + EPS)
    r_ref[...] = r
    w_ref[...] = jnp.exp(-jnp.exp(w_log))
    k_ref[...] = k * (1.0 + (a - 1.0) * k_a)
    v_ref[...] = v
    a_ref[...] = -kk
    b_ref[...] = kk * a


def _pad_rows(w, rows):
    return jnp.concatenate([w, jnp.zeros((rows - w.shape[0], w.shape[1]), w.dtype)],
                           axis=0).astype(BF16)


def _rwkv_prep(rw, mu, vecs, w_up, a_up, g_up, vmix_up, v_first):
    B, S, _ = rw.shape
    ts = min(256, S)
    first = v_first is None
    tok_spec = pl.BlockSpec((None, ts, GROUP_W), lambda b, i: (b, i, 0))
    const = lambda shape: pl.BlockSpec(shape, lambda b, i: (0,) * len(shape))
    in_specs = [
        pl.BlockSpec((None, ts, RWKV_W), lambda b, i: (b, i, 0)),
        pl.BlockSpec((None, 8, RWKV_W), lambda b, i: (b, jnp.maximum(i * (ts // 8) - 1, 0), 0)),
        const((1, RWKV_W)), const((8, GROUP_W)),
        const((128, GROUP_W)), const((128, GROUP_W)), const((256, GROUP_W)),
    ]
    args = [rw, rw, mu, vecs, _pad_rows(w_up, 128), _pad_rows(a_up, 128), _pad_rows(g_up, 256)]
    if not first:
        in_specs += [const((128, GROUP_W)), tok_spec]
        args += [_pad_rows(vmix_up, 128), v_first]
    tok = jax.ShapeDtypeStruct((B, S, GROUP_W), F32)
    n_out = 8 if first else 7
    return pl.pallas_call(
        functools.partial(_rwkv_prep_kernel, ts=ts, first=first),
        name="rwkv_prep",
        grid=(B, S // ts),
        in_specs=in_specs,
        out_specs=[tok_spec] * n_out,
        out_shape=[tok] * n_out,
        scratch_shapes=[pltpu.VMEM((ts + 8, RWKV_W), F32)],
        compiler_params=_cparams(("parallel", "parallel")),
    )(*args)


def _wkv_scan_kernel(r_ref, w_ref, k_ref, v_ref, a_ref, b_ref, an_ref, y_ref, s_ref, sa_ref, *, tt):
    @pl.when(pl.program_id(0) == 0)
    def _():
        s_ref[...] = jnp.zeros_like(s_ref)
        sa_ref[...] = jnp.zeros_like(sa_ref)

    def step(t, sa, a_next):
        vt = v_ref[t]
        y_parts = [None] * 2
        n_parts = [None] * 2
        for j in range(HEAD_DIM):
            s_new = (s_ref[j] * w_ref[t, pl.ds(j, 1), :]
                     + (sa * b_ref[t, pl.ds(j, 1), :] + vt * k_ref[t, pl.ds(j, 1), :]))
            s_ref[j] = s_new
            t_y = s_new * r_ref[t, pl.ds(j, 1), :]
            t_n = s_new * a_next(j)
            y_parts[j % 2] = t_y if y_parts[j % 2] is None else y_parts[j % 2] + t_y
            n_parts[j % 2] = t_n if n_parts[j % 2] is None else n_parts[j % 2] + t_n
        y_ref[t] = y_parts[0] + y_parts[1]
        return n_parts[0] + n_parts[1]

    sa = lax.fori_loop(0, tt - 1,
                       lambda t, sa: step(t, sa, lambda j: a_ref[t + 1, pl.ds(j, 1), :]),
                       sa_ref[...])
    sa_ref[...] = step(tt - 1, sa, lambda j: an_ref[0, pl.ds(j, 1), :])


def _wkv_scan(r, w, k, v, a, b):
    S, N, Ln = r.shape
    tt = min(32, S)
    seq = pl.BlockSpec((tt, N, Ln), lambda i: (i, 0, 0))
    return pl.pallas_call(
        functools.partial(_wkv_scan_kernel, tt=tt),
        name="wkv_scan",
        grid=(S // tt,),
        in_specs=[seq] * 6 + [_next_row_spec(tt, S, N, Ln)],
        out_specs=seq,
        out_shape=jax.ShapeDtypeStruct((S, N, Ln), F32),
        scratch_shapes=[pltpu.VMEM((N, N, Ln), F32), pltpu.VMEM((N, Ln), F32)],
        compiler_params=_cparams(("arbitrary",)),
    )(r, w, k, v, a, b, a)


def _rwkv_post_kernel(y_ref, r_ref, k_ref, v_ref, g_ref, vecs_ref, out_ref):
    y = y_ref[...]
    inv_n = 1.0 / HEAD_DIM
    yc = y - _group_sum(y, HEAD_DIM) * inv_n
    var = _group_sum(yc * yc, HEAD_DIM) * inv_n
    yn = yc * lax.rsqrt(var + LNX_EPS) * vecs_ref[0:1, :] + vecs_ref[1:2, :]
    bonus = _group_sum(r_ref[...] * k_ref[...] * vecs_ref[2:3, :], HEAD_DIM)
    out_ref[...] = ((yn + bonus * v_ref[...]) * g_ref[...]).astype(BF16)


def _rwkv_post(y, r, k, v, g, vecs):
    B, S, _ = y.shape
    ts = min(512, S)
    tok_spec = pl.BlockSpec((None, ts, GROUP_W), lambda b, i: (b, i, 0))
    return pl.pallas_call(
        _rwkv_post_kernel,
        name="rwkv_post",
        grid=(B, S // ts),
        in_specs=[tok_spec] * 5 + [pl.BlockSpec((8, GROUP_W), lambda b, i: (0, 0))],
        out_specs=tok_spec,
        out_shape=jax.ShapeDtypeStruct((B, S, GROUP_W), BF16),
        compiler_params=_cparams(("parallel", "parallel")),
    )(y, r, k, v, g, vecs)


def _outproj_kernel(h_ref, mod_ref, oa_ref, ob_ref, oc_ref, od_ref, w_ref, out_ref):
    acc = jnp.dot(oa_ref[...], w_ref[0:GROUP_W, :], preferred_element_type=F32)
    acc += jnp.dot(ob_ref[...], w_ref[GROUP_W:2 * GROUP_W, :], preferred_element_type=F32)
    acc += jnp.dot(oc_ref[...], w_ref[2 * GROUP_W:3 * GROUP_W, :], preferred_element_type=F32)
    acc += jnp.dot(od_ref[...], w_ref[3 * GROUP_W:4 * GROUP_W, :], preferred_element_type=F32)
    out_ref[...] = h_ref[...] + mod_ref[5:6, :] * acc


def _outproj(h, mod_l, o_a, o_b, o_c, o_d, w_out, l, S):
    T, D = h.shape
    tm = min(1024, S)
    tpb = S // tm
    mix = pl.BlockSpec((tm, GROUP_W), lambda i: (i, 0))
    return pl.pallas_call(
        _outproj_kernel,
        name="outproj",
        grid=(T // tm,),
        in_specs=[
            pl.BlockSpec((tm, D), lambda i: (i, 0)),
            pl.BlockSpec((None, N_MOD, D), lambda i: (i // tpb, 0, 0)),
            mix, mix, mix, mix,
            pl.BlockSpec((None, D, D), lambda i: (l, 0, 0)),
        ],
        out_specs=pl.BlockSpec((tm, D), lambda i: (i, 0)),
        out_shape=jax.ShapeDtypeStruct((T, D), F32),
        compiler_params=_cparams(("parallel",)),
    )(h, mod_l, o_a, o_b, o_c, o_d, w_out)


def _pad_cols(w, width):
    pad = width - w.shape[-1]
    return jnp.concatenate([w, jnp.zeros(w.shape[:-1] + (pad,), w.dtype)], axis=-1)


def _rope_perm():
    p = np.arange(128)
    first = (p // 16) * DIFF_DIM + p % 16
    return np.concatenate([first, first + 16])


def _permute_w_in(w_in, vmix_down):
    L, D, _ = w_in.shape
    perm = _rope_perm()
    g0 = 9 * GROUP_W
    r0 = RWKV_START
    vd = jnp.concatenate([jnp.zeros((1, D, vmix_down.shape[-1]), w_in.dtype), vmix_down], axis=0)
    cols = [
        w_in[..., perm], w_in[..., GROUP_W + perm], w_in[..., 2 * GROUP_W:3 * GROUP_W],
        w_in[..., 3 * GROUP_W:6 * GROUP_W],
        w_in[..., 6 * GROUP_W:g0], w_in[..., g0 + 8:g0 + 8 + GROUP_W], _pad_cols(w_in[..., g0:g0 + 8], 128),
        w_in[..., r0:r0 + 768], _pad_cols(w_in[..., r0 + 768:r0 + 832], 128),
        _pad_cols(w_in[..., r0 + 832:r0 + 896], 128), _pad_cols(w_in[..., r0 + 896:r0 + 1056], 256),
        _pad_cols(vd, 128),
    ]
    return jnp.concatenate(cols, axis=-1).astype(BF16)


def _permute_mu(mu, vmix_mu):
    L = mu.shape[0]
    vm = jnp.concatenate([jnp.zeros((1, vmix_mu.shape[-1]), mu.dtype), vmix_mu], axis=0)
    cols = [mu[:, 0:768], _pad_cols(mu[:, 768:832], 128), _pad_cols(mu[:, 832:896], 128),
            _pad_cols(mu[:, 896:1056], 256), _pad_cols(vm, 128)]
    return jnp.concatenate(cols, axis=-1).reshape(L, 1, RWKV_W)


def _to_lanes(x):
    B, S, _ = x.shape
    return x.reshape(B, S, N_HEADS, HEAD_DIM).transpose(1, 3, 0, 2).reshape(S, HEAD_DIM, B * N_HEADS)


def _from_lanes(y, B):
    S = y.shape[0]
    return y.reshape(S, HEAD_DIM, B, N_HEADS).transpose(2, 0, 3, 1).reshape(B, S, GROUP_W)


def _head_scalars_to_lanes(x):
    B, S, _ = x.shape
    return x.transpose(1, 0, 2).reshape(S, B * N_HEADS)


def kernel(x, c, ada_w, ada_b, norm_g, ffn_w13, ffn_w2, w_in, w_out, diff_qk_g, diff_lambda, diff_subln_g, gdn_conv_w, gdn_a_log, gdn_dt_bias, gdn_norm_g, rwkv_mu, rwkv_w0, rwkv_w_up, rwkv_a0, rwkv_a_up, rwkv_g_up, rwkv_k_k, rwkv_k_a, rwkv_r_k, rwkv_lnx_w, rwkv_lnx_b, rwkv_vmix_down, rwkv_vmix_mu, rwkv_v0, rwkv_vmix_up):
    B, S, D = x.shape
    L = ada_w.shape[0]
    T = B * S

    mod = _mod_all(c, ada_w, ada_b)
    w13 = ffn_w13.astype(BF16)
    w2 = ffn_w2.astype(BF16)
    w_perm = _permute_w_in(w_in, rwkv_vmix_down)
    w_o = w_out.astype(BF16)
    mu = _permute_mu(rwkv_mu, rwkv_vmix_mu)

    inv = ROPE_THETA ** (-jnp.arange(0, DIFF_DIM, 2, dtype=F32) / DIFF_DIM)
    ang = jnp.arange(S, dtype=F32)[:, None] * inv[None, :]
    cos128 = jnp.tile(jnp.cos(ang), (1, 8))
    sin128 = jnp.tile(jnp.sin(ang), (1, 8))

    h = x.reshape(T, D)
    v_first = None
    for l in range(L):
        lam_init = 0.8 - 0.6 * math.exp(-0.3 * l)
        h = _ffn(h, mod[l], norm_g[l, 0], w13, w2, l, 0, 0, S)

        qkg = jnp.stack([jnp.tile(diff_qk_g[l, 0, :16], 8), jnp.tile(diff_qk_g[l, 0, 16:], 8),
                         jnp.tile(diff_qk_g[l, 1, :16], 8), jnp.tile(diff_qk_g[l, 1, 16:], 8)])
        attn, gdn, rw = _inproj(h, mod[l], norm_g[l, 1], w_perm, l, qkg, cos128, sin128, S)
        attn = attn.reshape(B, S, ATTN_W)
        gdn = gdn.reshape(B, S, GDN_W)
        rw = rw.reshape(B, S, RWKV_W)

        o_a = _diff_attention(attn, diff_lambda[l], diff_subln_g[l], lam_init)
        o_b = _sb_attention(attn)

        gq, gk, gv, gab = _gdn_prep(gdn, gdn_conv_w[l], gdn_a_log[l], gdn_dt_bias[l])
        go = _gdn_scan(_to_lanes(gq), _to_lanes(gk), _to_lanes(gv),
                       _head_scalars_to_lanes(gab[..., N_HEADS:2 * N_HEADS]),
                       _head_scalars_to_lanes(gab[..., 0:N_HEADS]))
        o_c = _gdn_post(_from_lanes(go, B), gdn, gdn_norm_g[l])

        zeros = jnp.zeros((GROUP_W,), F32)
        vecs = jnp.stack([rwkv_w0[l], rwkv_a0[l], rwkv_k_k[l], rwkv_k_a[l],
                          rwkv_v0[l - 1] if l > 0 else zeros, zeros, zeros, zeros])
        outs = _rwkv_prep(rw, mu[l], vecs, rwkv_w_up[l], rwkv_a_up[l], rwkv_g_up[l],
                          rwkv_vmix_up[l - 1] if l > 0 else None, v_first)
        rr, rw_, rk, rv, ra, rb, rg = outs[:7]
        if l == 0:
            v_first = outs[7]
        ry = _wkv_scan(*[_to_lanes(t) for t in (rr, rw_, rk, rv, ra, rb)])
        pvecs = jnp.stack([rwkv_lnx_w[l], rwkv_lnx_b[l], rwkv_r_k[l].reshape(GROUP_W),
                           zeros, zeros, zeros, zeros, zeros])
        o_d = _rwkv_post(_from_lanes(ry, B), rr, rk, rv, rg, pvecs)

        flat = lambda t: t.reshape(T, GROUP_W)
        h = _outproj(h, mod[l], flat(o_a), flat(o_b), flat(o_c), flat(o_d), w_o, l, S)
        h = _ffn(h, mod[l], norm_g[l, 2], w13, w2, l, 1, 6, S)
    return h.reshape(B, S, D)
```

```python
import functools
import math

import numpy as np
import jax
import jax.numpy as jnp
from jax import lax
from jax.experimental import pallas as pl
from jax.experimental.pallas import tpu as pltpu

F32 = jnp.float32
BF16 = jnp.bfloat16
HIGHEST = lax.Precision.HIGHEST

D_MODEL = 1024
N_HEADS = 4
HEAD_DIM = 64
GROUP_W = N_HEADS * HEAD_DIM
DIFF_DIM = HEAD_DIM // 2
D_FF = 2816
N_MOD = 9
EPS = 1e-6
LNX_EPS = 64e-5
ROPE_THETA = 10000.0
LOG2E = 1.4426950408889634
CONV_K = 4
RWKV_START = 9 * GROUP_W + 2 * N_HEADS + GROUP_W

ATTN_W = 6 * GROUP_W
GDN_W = 3 * GROUP_W + GROUP_W + 128
RWKV_W = 3 * GROUP_W + 128 + 128 + 256 + 128
PROJ_W = ATTN_W + GDN_W + RWKV_W
VMEM_LIMIT = 56 * 1024 * 1024


def _cparams(sem):
    return pltpu.CompilerParams(dimension_semantics=sem, vmem_limit_bytes=VMEM_LIMIT)


def _sigmoid(x):
    return 1.0 / (1.0 + jnp.exp(-x))


def _softplus(x):
    return jnp.maximum(x, 0.0) + jnp.log(1.0 + jnp.exp(-jnp.abs(x)))


def _group_ones(n, group):
    r = lax.broadcasted_iota(jnp.int32, (n, n), 0) // group
    c = lax.broadcasted_iota(jnp.int32, (n, n), 1) // group
    return (r == c).astype(BF16)


def _group_sum(x, group):
    ones = _group_ones(x.shape[-1], group)
    hi = x.astype(BF16)
    lo = (x - hi.astype(F32)).astype(BF16)
    return (jnp.dot(hi, ones, preferred_element_type=F32)
            + jnp.dot(lo, ones, preferred_element_type=F32))


def _modulate(x, gain, shift, scale):
    ms = jnp.mean(x * x, axis=-1, keepdims=True)
    return (x * lax.rsqrt(ms + EPS) * gain) * (1.0 + scale) + shift


def _mod_kernel(c_ref, w_ref, b_ref, o_ref):
    c = c_ref[...]
    cond = c * _sigmoid(c)
    o_ref[...] = jnp.dot(cond, w_ref[...], precision=HIGHEST,
                         preferred_element_type=F32) + b_ref[...]


def _mod_all(c, ada_w, ada_b):
    L, D, N = ada_w.shape
    B = c.shape[0]
    tn = 1024
    out = pl.pallas_call(
        _mod_kernel,
        name="adaln_mod",
        grid=(L, N // tn),
        in_specs=[
            pl.BlockSpec((B, D), lambda l, n: (0, 0)),
            pl.BlockSpec((None, D, tn), lambda l, n: (l, 0, n)),
            pl.BlockSpec((None, 1, tn), lambda l, n: (l, 0, n)),
        ],
        out_specs=pl.BlockSpec((None, B, tn), lambda l, n: (l, 0, n)),
        out_shape=jax.ShapeDtypeStruct((L, B, N), F32),
        compiler_params=_cparams(("parallel", "parallel")),
    )(c, ada_w, ada_b.reshape(L, 1, N))
    return out.reshape(L, B, N_MOD, D)


def _ffn_kernel(*refs, mb, tf, mix):
    x_ref, mod_ref, g_ref, w13_ref, w2_ref = refs[:5]
    o_ref, act_ref = refs[-2:]
    x = x_ref[...]
    if mix:
        w_ref = refs[9]
        proj = jnp.dot(refs[5][...], w_ref[0:GROUP_W, :], preferred_element_type=F32)
        for g in range(1, 4):
            proj += jnp.dot(refs[5 + g][...], w_ref[g * GROUP_W:(g + 1) * GROUP_W, :],
                            preferred_element_type=F32)
        x = x + mod_ref[5:6, :] * proj
    hn = _modulate(x, g_ref[...], mod_ref[mb:mb + 1, :], mod_ref[mb + 1:mb + 2, :]).astype(BF16)
    for f in range(D_FF // tf):
        gate = jnp.dot(hn, w13_ref[:, f * tf:(f + 1) * tf], preferred_element_type=F32)
        up = jnp.dot(hn, w13_ref[:, D_FF + f * tf:D_FF + (f + 1) * tf], preferred_element_type=F32)
        act_ref[:, f * tf:(f + 1) * tf] = (gate * _sigmoid(gate) * up).astype(BF16)
    acc = jnp.dot(act_ref[...], w2_ref[...], preferred_element_type=F32)
    o_ref[...] = x + 0.5 * mod_ref[mb + 2:mb + 3, :] * acc


def _ffn(h, mod_l, gain, w13, w2, l, j, mb, S, mixed=None, w_out=None):
    T, D = h.shape
    tm = min(1024, S)
    tf = 256
    tpb = S // tm
    resident = dict(pipeline_mode=pl.Buffered(1))
    in_specs = [
        pl.BlockSpec((tm, D), lambda i: (i, 0)),
        pl.BlockSpec((None, N_MOD, D), lambda i: (i // tpb, 0, 0)),
        pl.BlockSpec((1, D), lambda i: (0, 0)),
        pl.BlockSpec((None, None, D, 2 * D_FF), lambda i: (l, j, 0, 0), **resident),
        pl.BlockSpec((None, None, D_FF, D), lambda i: (l, j, 0, 0), **resident),
    ]
    args = [h, mod_l, gain.reshape(1, D), w13, w2]
    if mixed is not None:
        in_specs += [pl.BlockSpec((tm, GROUP_W), lambda i: (i, 0))] * 4
        in_specs += [pl.BlockSpec((None, D, D), lambda i: (l, 0, 0), **resident)]
        args += list(mixed) + [w_out]
    return pl.pallas_call(
        functools.partial(_ffn_kernel, mb=mb, tf=tf, mix=mixed is not None),
        name="ffn",
        grid=(T // tm,),
        in_specs=in_specs,
        out_specs=pl.BlockSpec((tm, D), lambda i: (i, 0)),
        out_shape=jax.ShapeDtypeStruct((T, D), F32),
        scratch_shapes=[pltpu.VMEM((tm, D_FF), BF16)],
        compiler_params=_cparams(("parallel",)),
    )(*args)


def _inproj_kernel(x_ref, mod_ref, g_ref, w_ref, qkg_ref, cos_ref, sin_ref,
                   attn_ref, gdn_ref, rwkv_ref):
    hn = _modulate(x_ref[...], g_ref[...], mod_ref[3:4, :], mod_ref[4:5, :]).astype(BF16)

    pa = jnp.dot(hn, w_ref[:, 0:3 * GROUP_W], preferred_element_type=F32)
    cos = cos_ref[...]
    sin = sin_ref[...]
    for idx, scale in ((0, DIFF_DIM ** -0.5 * LOG2E), (1, 1.0)):
        x1 = pa[:, idx * 256:idx * 256 + 128]
        x2 = pa[:, idx * 256 + 128:idx * 256 + 256]
        ms = _group_sum(x1 * x1 + x2 * x2, 16) * (1.0 / DIFF_DIM)
        inv = lax.rsqrt(ms + EPS)
        n1 = x1 * inv * qkg_ref[2 * idx:2 * idx + 1, :]
        n2 = x2 * inv * qkg_ref[2 * idx + 1:2 * idx + 2, :]
        attn_ref[:, idx * 256:idx * 256 + 128] = ((n1 * cos - n2 * sin) * scale).astype(BF16)
        attn_ref[:, idx * 256 + 128:idx * 256 + 256] = ((n2 * cos + n1 * sin) * scale).astype(BF16)
    attn_ref[:, 512:768] = pa[:, 512:768].astype(BF16)

    pb = jnp.dot(hn, w_ref[:, 3 * GROUP_W:ATTN_W], preferred_element_type=F32)
    attn_ref[:, 768:1024] = (pb[:, 0:256] * (HEAD_DIM ** -0.5)).astype(BF16)
    attn_ref[:, 1024:1536] = pb[:, 256:768].astype(BF16)

    gdn_ref[...] = jnp.dot(hn, w_ref[:, ATTN_W:ATTN_W + GDN_W], preferred_element_type=F32)
    rwkv_ref[...] = jnp.dot(hn, w_ref[:, ATTN_W + GDN_W:PROJ_W], preferred_element_type=F32)


def _inproj(h, mod_l, gain, w_perm, l, qkg, cos128, sin128, S):
    T, D = h.shape
    tm = min(512, S)
    tpb = S // tm
    return pl.pallas_call(
        _inproj_kernel,
        name="inproj",
        grid=(T // tm,),
        in_specs=[
            pl.BlockSpec((tm, D), lambda i: (i, 0)),
            pl.BlockSpec((None, N_MOD, D), lambda i: (i // tpb, 0, 0)),
            pl.BlockSpec((1, D), lambda i: (0, 0)),
            pl.BlockSpec((None, D, PROJ_W), lambda i: (l, 0, 0)),
            pl.BlockSpec((4, 128), lambda i: (0, 0)),
            pl.BlockSpec((tm, 128), lambda i: (i % tpb, 0)),
            pl.BlockSpec((tm, 128), lambda i: (i % tpb, 0)),
        ],
        out_specs=[
            pl.BlockSpec((tm, ATTN_W), lambda i: (i, 0)),
            pl.BlockSpec((tm, GDN_W), lambda i: (i, 0)),
            pl.BlockSpec((tm, RWKV_W), lambda i: (i, 0)),
        ],
        out_shape=[
            jax.ShapeDtypeStruct((T, ATTN_W), BF16),
            jax.ShapeDtypeStruct((T, GDN_W), F32),
            jax.ShapeDtypeStruct((T, RWKV_W), F32),
        ],
        compiler_params=_cparams(("parallel",)),
    )(h, mod_l, gain.reshape(1, D), w_perm, qkg, cos128, sin128)


def _diff_kernel(q_ref, k_ref, vt_ref, lamv_ref, g_ref, o_ref, qc_ref, acc_ref,
                 *, lam_init, tq, tk):
    i = pl.program_id(1)
    q = q_ref[...]
    lane = lax.broadcasted_iota(jnp.int32, (1, GROUP_W), 1)
    sub = (lane % 128) // 16
    for h in range(N_HEADS):
        qc_ref[h, 0:tq, :] = jnp.where(sub == 2 * h, q, jnp.zeros_like(q))
        qc_ref[h, tq:2 * tq, :] = jnp.where(sub == 2 * h + 1, q, jnp.zeros_like(q))
    acc_ref[...] = jnp.zeros_like(acc_ref)

    def tile(kt, carry, masked):
        start = pl.multiple_of(kt * tk, tk)
        k = k_ref[pl.ds(start, tk), :]
        heads = range(N_HEADS)
        ss = [lax.dot_general(k, qc_ref[h], (((1,), (1,)), ((), ())), preferred_element_type=F32)
              for h in heads]
        out, ps, alphas = [], [], []
        for h in heads:
            m, l = carry[h]
            s = ss[h]
            if masked:
                key = kt * tk + lax.broadcasted_iota(jnp.int32, (tk, 2 * tq), 0)
                qpos = i * tq + lax.broadcasted_iota(jnp.int32, (tk, 2 * tq), 1) % tq
                s = jnp.where(key <= qpos, s, -jnp.inf)
            m_new = jnp.maximum(m, jnp.max(s, axis=0, keepdims=True))
            alpha = jnp.exp2(m - m_new)
            p = jnp.exp2(s - m_new)
            out.append((m_new, alpha * l + jnp.sum(p, axis=0, keepdims=True)))
            ps.append(p.astype(BF16))
            alphas.append(alpha)
        for h in heads:
            vt = vt_ref[kt, h * HEAD_DIM:(h + 1) * HEAD_DIM, :]
            acc_ref[h] = alphas[h] * acc_ref[h] + jnp.dot(vt, ps[h], preferred_element_type=F32)
        return tuple(out)

    init = tuple((jnp.full((1, 2 * tq), -jnp.inf, F32), jnp.zeros((1, 2 * tq), F32))
                 for _ in range(N_HEADS))
    kd = (i * tq) // tk
    carry = lax.fori_loop(0, kd, lambda kt, c: tile(kt, c, False), init)
    carry = tile(kd, carry, True)

    lv = lamv_ref[...]
    lam = (jnp.exp(jnp.sum(lv[0:1, :] * lv[1:2, :], axis=1, keepdims=True))
           - jnp.exp(jnp.sum(lv[2:3, :] * lv[3:4, :], axis=1, keepdims=True)) + lam_init)
    rows = []
    for h in range(N_HEADS):
        o = acc_ref[h] / carry[h][1]
        oh = o[:, :tq] - lam * o[:, tq:]
        ms = jnp.mean(oh * oh, axis=0, keepdims=True)
        rows.append(oh * lax.rsqrt(ms + EPS) * g_ref[h * HEAD_DIM:(h + 1) * HEAD_DIM, :]
                    * (1.0 - lam_init))
    o_ref[...] = jnp.concatenate(rows, axis=0).T.astype(BF16)


def _kv_tiles_transposed(v, tk):
    B, S, C = v.shape
    return v.reshape(B, S // tk, tk, C).transpose(0, 1, 3, 2)


def _diff_attention(attn, lamv, subln_g, lam_init):
    B, S, _ = attn.shape
    tq = min(256, S)
    tk = min(256, S)
    vt = _kv_tiles_transposed(attn[..., 2 * GROUP_W:3 * GROUP_W], tk)
    gain = jnp.broadcast_to(jnp.tile(subln_g, N_HEADS)[:, None], (GROUP_W, tq))
    return pl.pallas_call(
        functools.partial(_diff_kernel, lam_init=lam_init, tq=tq, tk=tk),
        name="diff_attn",
        grid=(B, S // tq),
        in_specs=[
            pl.BlockSpec((None, tq, GROUP_W), lambda b, i: (b, i, 0)),
            pl.BlockSpec((None, S, GROUP_W), lambda b, i: (b, 0, 1)),
            pl.BlockSpec((None, S // tk, GROUP_W, tk), lambda b, i: (b, 0, 0, 0)),
            pl.BlockSpec((4, DIFF_DIM), lambda b, i: (0, 0)),
            pl.BlockSpec((GROUP_W, tq), lambda b, i: (0, 0)),
        ],
        out_specs=pl.BlockSpec((None, tq, GROUP_W), lambda b, i: (b, i, 0)),
        out_shape=jax.ShapeDtypeStruct((B, S, GROUP_W), BF16),
        scratch_shapes=[pltpu.VMEM((N_HEADS, 2 * tq, GROUP_W), BF16),
                        pltpu.VMEM((N_HEADS, HEAD_DIM, 2 * tq), F32)],
        compiler_params=_cparams(("parallel", "parallel")),
    )(attn, attn, vt, lamv, gain)


def _sb_kernel(q_ref, k_ref, vt_ref, o_ref, qh_ref, tri_ref, acc_ref, *, t):
    i = pl.program_id(1)
    q = q_ref[...]
    lane = lax.broadcasted_iota(jnp.int32, (1, GROUP_W), 1)
    r_i = lax.broadcasted_iota(jnp.int32, (t, t), 0)
    c_i = lax.broadcasted_iota(jnp.int32, (t, t), 1)
    tri = (c_i > r_i).astype(BF16)
    tri_ref[...] = tri
    for h in range(N_HEADS):
        qh_ref[h] = jnp.where(lane // HEAD_DIM == h, q, jnp.zeros_like(q))
    acc_ref[...] = jnp.zeros_like(acc_ref)

    def tile(kt, carry, diag):
        start = pl.multiple_of(kt * t, t)
        k = k_ref[pl.ds(start, t), :]
        heads = range(N_HEADS)
        strict = r_i < c_i
        zs = [lax.dot_general(k, qh_ref[h], (((1,), (1,)), ((), ())), preferred_element_type=F32)
              for h in heads]
        lss, lks, sums = [], [], []
        for h in heads:
            z = zs[h]
            neg_abs = lax.bitcast_convert_type(
                lax.bitcast_convert_type(z, jnp.uint32) | jnp.uint32(0x80000000), F32)
            ls = jnp.minimum(z, 0.0) - jnp.log(1.0 + jnp.exp(neg_abs))
            lk = ls - z
            if diag:
                lk = jnp.where(strict, lk, 0.0)
            lss.append(ls)
            lks.append(lk.astype(BF16))
            sums.append(jnp.sum(lk, axis=0, keepdims=True))
        bts = [jnp.dot(tri_ref[...], lks[h], preferred_element_type=F32) for h in heads]
        pas = []
        for h in heads:
            a = jnp.exp(lss[h] + (bts[h] + carry[h]))
            if diag:
                a = jnp.where(strict, a, 0.0)
            pas.append(a.astype(BF16))
        for h in heads:
            vt = vt_ref[kt, h * HEAD_DIM:(h + 1) * HEAD_DIM, :]
            acc_ref[h] += jnp.dot(vt, pas[h], preferred_element_type=F32)
        return tuple(carry[h] + sums[h] for h in heads)

    carry = tile(i, tuple(jnp.zeros((1, t), F32) for _ in range(N_HEADS)), True)
    lax.fori_loop(0, i, lambda n, c: tile(i - 1 - n, c, False), carry)
    o_ref[...] = jnp.concatenate([acc_ref[h] for h in range(N_HEADS)], axis=0).T.astype(BF16)


def _sb_attention(attn):
    B, S, _ = attn.shape
    t = min(256, S)
    vt = _kv_tiles_transposed(attn[..., 5 * GROUP_W:6 * GROUP_W], t)
    return pl.pallas_call(
        functools.partial(_sb_kernel, t=t),
        name="sb_attn",
        grid=(B, S // t),
        in_specs=[
            pl.BlockSpec((None, t, GROUP_W), lambda b, i: (b, i, 3)),
            pl.BlockSpec((None, S, GROUP_W), lambda b, i: (b, 0, 4)),
            pl.BlockSpec((None, S // t, GROUP_W, t), lambda b, i: (b, 0, 0, 0)),
        ],
        out_specs=pl.BlockSpec((None, t, GROUP_W), lambda b, i: (b, i, 0)),
        out_shape=jax.ShapeDtypeStruct((B, S, GROUP_W), BF16),
        scratch_shapes=[pltpu.VMEM((N_HEADS, t, GROUP_W), BF16),
                        pltpu.VMEM((t, t), BF16),
                        pltpu.VMEM((N_HEADS, HEAD_DIM, t), F32)],
        compiler_params=_cparams(("parallel", "parallel")),
    )(attn, attn, vt)


def _gdn_prep_kernel(x_ref, prev_ref, small_ref, cw_ref, alog_ref, dtb_ref,
                     q_ref, k_ref, v_ref, ab_ref, xs_ref, *, ts):
    i = pl.program_id(1)
    prev = prev_ref[...]
    xs_ref[0:8, :] = jnp.where(i == 0, jnp.zeros_like(prev), prev)
    xs_ref[8:8 + ts, :] = x_ref[...]
    conv = jnp.zeros((ts, 3 * GROUP_W), F32)
    for j in range(CONV_K):
        conv = conv + xs_ref[pl.ds(8 - (CONV_K - 1) + j, ts), :] * cw_ref[j:j + 1, :]
    y = conv * _sigmoid(conv)
    q = y[:, 0:GROUP_W]
    k = y[:, GROUP_W:2 * GROUP_W]
    q_ref[...] = q * lax.rsqrt(_group_sum(q * q, HEAD_DIM) + EPS) * (HEAD_DIM ** -0.5)
    k_ref[...] = k * lax.rsqrt(_group_sum(k * k, HEAD_DIM) + EPS)
    v_ref[...] = y[:, 2 * GROUP_W:3 * GROUP_W]
    small = small_ref[...]
    beta = _sigmoid(small)
    alpha = jnp.exp(-jnp.exp(alog_ref[...]) * _softplus(small + dtb_ref[...]))
    lane = lax.broadcasted_iota(jnp.int32, (1, 128), 1)
    ab_ref[...] = jnp.where(lane < N_HEADS, beta, alpha)[:, 0:2 * N_HEADS]


def _gdn_prep(gdn, conv_w, a_log, dt_bias):
    B, S, _ = gdn.shape
    ts = min(512, S)
    pad = jnp.zeros((N_HEADS,), F32)
    alog = jnp.concatenate([pad, a_log, jnp.zeros((120,), F32)]).reshape(1, 128)
    dtb = jnp.concatenate([pad, dt_bias, jnp.zeros((120,), F32)]).reshape(1, 128)
    tok = jax.ShapeDtypeStruct((B, S, GROUP_W), F32)
    return pl.pallas_call(
        functools.partial(_gdn_prep_kernel, ts=ts),
        name="gdn_prep",
        grid=(B, S // ts),
        in_specs=[
            pl.BlockSpec((None, ts, 3 * GROUP_W), lambda b, i: (b, i, 0)),
            pl.BlockSpec((None, 8, 3 * GROUP_W), lambda b, i: (b, jnp.maximum(i * (ts // 8) - 1, 0), 0)),
            pl.BlockSpec((None, ts, 128), lambda b, i: (b, i, 8)),
            pl.BlockSpec((CONV_K, 3 * GROUP_W), lambda b, i: (0, 0)),
            pl.BlockSpec((1, 128), lambda b, i: (0, 0)),
            pl.BlockSpec((1, 128), lambda b, i: (0, 0)),
        ],
        out_specs=[
            pl.BlockSpec((None, ts, GROUP_W), lambda b, i: (b, i, 0)),
            pl.BlockSpec((None, ts, GROUP_W), lambda b, i: (b, i, 0)),
            pl.BlockSpec((None, ts, GROUP_W), lambda b, i: (b, i, 0)),
            pl.BlockSpec((None, ts, 2 * N_HEADS), lambda b, i: (b, i, 0)),
        ],
        out_shape=[tok, tok, tok, jax.ShapeDtypeStruct((B, S, 2 * N_HEADS), F32)],
        scratch_shapes=[pltpu.VMEM((ts + 8, 3 * GROUP_W), F32)],
        compiler_params=_cparams(("parallel", "parallel")),
    )(gdn, gdn, gdn, conv_w, alog, dtb)


def _gdn_scan_kernel(q_ref, k_ref, v_ref, al_ref, be_ref, kn_ref, o_ref, s_ref, ks_ref, *, tt):
    @pl.when(pl.program_id(0) == 0)
    def _():
        s_ref[...] = jnp.zeros_like(s_ref)
        ks_ref[...] = jnp.zeros_like(ks_ref)

    def step(t, ks, k_next):
        al = al_ref[pl.ds(t, 1), :]
        d = be_ref[pl.ds(t, 1), :] * (v_ref[t] - al * ks)
        o_parts = [None] * 2
        n_parts = [None] * 2
        for kk in range(HEAD_DIM):
            s_new = al * s_ref[kk] + k_ref[t, pl.ds(kk, 1), :] * d
            s_ref[kk] = s_new
            t_o = s_new * q_ref[t, pl.ds(kk, 1), :]
            t_n = s_new * k_next(kk)
            o_parts[kk % 2] = t_o if o_parts[kk % 2] is None else o_parts[kk % 2] + t_o
            n_parts[kk % 2] = t_n if n_parts[kk % 2] is None else n_parts[kk % 2] + t_n
        o_ref[t] = o_parts[0] + o_parts[1]
        return n_parts[0] + n_parts[1]

    ks = lax.fori_loop(0, tt - 1,
                       lambda t, ks: step(t, ks, lambda kk: k_ref[t + 1, pl.ds(kk, 1), :]),
                       ks_ref[...])
    ks_ref[...] = step(tt - 1, ks, lambda kk: kn_ref[0, pl.ds(kk, 1), :])


def _next_row_spec(tt, S, N, Ln):
    return pl.BlockSpec((1, N, Ln), lambda i: (jnp.minimum((i + 1) * tt, S - 1), 0, 0))


def _gdn_scan(q, k, v, al, be):
    S, N, Ln = q.shape
    tt = min(32, S)
    seq = pl.BlockSpec((tt, N, Ln), lambda i: (i, 0, 0))
    vec = pl.BlockSpec((tt, Ln), lambda i: (i, 0))
    return pl.pallas_call(
        functools.partial(_gdn_scan_kernel, tt=tt),
        name="gdn_scan",
        grid=(S // tt,),
        in_specs=[seq, seq, seq, vec, vec, _next_row_spec(tt, S, N, Ln)],
        out_specs=seq,
        out_shape=jax.ShapeDtypeStruct((S, N, Ln), F32),
        scratch_shapes=[pltpu.VMEM((N, N, Ln), F32), pltpu.VMEM((N, Ln), F32)],
        compiler_params=_cparams(("arbitrary",)),
    )(q, k, v, al, be, k)


def _gdn_post_kernel(o_ref, gate_ref, g_ref, out_ref):
    o = o_ref[...]
    gate = gate_ref[...]
    ms = _group_sum(o * o, HEAD_DIM) * (1.0 / HEAD_DIM)
    out_ref[...] = (o * lax.rsqrt(ms + EPS) * g_ref[...] * (gate * _sigmoid(gate))).astype(BF16)


def _gdn_post(o, gdn, norm_g):
    B, S, _ = o.shape
    ts = min(512, S)
    return pl.pallas_call(
        _gdn_post_kernel,
        name="gdn_post",
        grid=(B, S // ts),
        in_specs=[
            pl.BlockSpec((None, ts, GROUP_W), lambda b, i: (b, i, 0)),
            pl.BlockSpec((None, ts, GROUP_W), lambda b, i: (b, i, 3)),
            pl.BlockSpec((1, GROUP_W), lambda b, i: (0, 0)),
        ],
        out_specs=pl.BlockSpec((None, ts, GROUP_W), lambda b, i: (b, i, 0)),
        out_shape=jax.ShapeDtypeStruct((B, S, GROUP_W), BF16),
        compiler_params=_cparams(("parallel", "parallel")),
    )(o, gdn, jnp.tile(norm_g, N_HEADS).reshape(1, GROUP_W))


def _rwkv_prep_kernel(*refs, ts, first):
    if first:
        (x_ref, prev_ref, mu_ref, vecs_ref, wup_ref, aup_ref, gup_ref,
         r_ref, w_ref, k_ref, v_ref, a_ref, b_ref, g_ref, vf_ref, xs_ref) = refs
    else:
        (x_ref, prev_ref, mu_ref, vecs_ref, wup_ref, aup_ref, gup_ref, vup_ref, vfin_ref,
         r_ref, w_ref, k_ref, v_ref, a_ref, b_ref, g_ref, xs_ref) = refs
    i = pl.program_id(1)
    prev = prev_ref[...]
    xs_ref[0:8, :] = jnp.where(i == 0, jnp.zeros_like(prev), prev)
    x = x_ref[...]
    xs_ref[8:8 + ts, :] = x
    xm = x + (xs_ref[pl.ds(7, ts), :] - x) * mu_ref[...]
    r = xm[:, 0:256]
    k = xm[:, 256:512]
    v = xm[:, 512:768]
    wd = xm[:, 768:896]
    ad = xm[:, 896:1024]
    gd = xm[:, 1024:1280]
    w0, a0, k_k, k_a = vecs_ref[0:1, :], vecs_ref[1:2, :], vecs_ref[2:3, :], vecs_ref[3:4, :]
    dot = lambda x, w: jnp.dot(x.astype(BF16), w, preferred_element_type=F32)
    w_log = -_softplus(-(w0 + dot(jnp.tanh(wd), wup_ref[...]))) - 0.5
    a = _sigmoid(a0 + dot(ad, aup_ref[...]))
    g_ref[...] = dot(_sigmoid(gd), gup_ref[...])
    if first:
        vf_ref[...] = v
    else:
        vd = xm[:, 1280:1408]
        v = v + (vfin_ref[...] - v) * _sigmoid(vecs_ref[4:5, :] + dot(vd, vup_ref[...]))
    kk = k * k_k
    kk = kk * lax.rsqrt(_group_sum(kk * kk, HEAD_DIM) + EPS)
    r_ref[...] = r
    w_ref[...] = jnp.exp(-jnp.exp(w_log))
    k_ref[...] = k * (1.0 + (a - 1.0) * k_a)
    v_ref[...] = v
    a_ref[...] = -kk
    b_ref[...] = kk * a


def _pad_rows(w, rows):
    return jnp.concatenate([w, jnp.zeros((rows - w.shape[0], w.shape[1]), w.dtype)],
                           axis=0).astype(BF16)


def _rwkv_prep(rw, mu, vecs, w_up, a_up, g_up, vmix_up, v_first):
    B, S, _ = rw.shape
    ts = min(512, S)
    first = v_first is None
    tok_spec = pl.BlockSpec((None, ts, GROUP_W), lambda b, i: (b, i, 0))
    const = lambda shape: pl.BlockSpec(shape, lambda b, i: (0,) * len(shape))
    in_specs = [
        pl.BlockSpec((None, ts, RWKV_W), lambda b, i: (b, i, 0)),
        pl.BlockSpec((None, 8, RWKV_W), lambda b, i: (b, jnp.maximum(i * (ts // 8) - 1, 0), 0)),
        const((1, RWKV_W)), const((8, GROUP_W)),
        const((128, GROUP_W)), const((128, GROUP_W)), const((256, GROUP_W)),
    ]
    args = [rw, rw, mu, vecs, _pad_rows(w_up, 128), _pad_rows(a_up, 128), _pad_rows(g_up, 256)]
    if not first:
        in_specs += [const((128, GROUP_W)), tok_spec]
        args += [_pad_rows(vmix_up, 128), v_first]
    tok = jax.ShapeDtypeStruct((B, S, GROUP_W), F32)
    n_out = 8 if first else 7
    return pl.pallas_call(
        functools.partial(_rwkv_prep_kernel, ts=ts, first=first),
        name="rwkv_prep",
        grid=(B, S // ts),
        in_specs=in_specs,
        out_specs=[tok_spec] * n_out,
        out_shape=[tok] * n_out,
        scratch_shapes=[pltpu.VMEM((ts + 8, RWKV_W), F32)],
        compiler_params=_cparams(("parallel", "parallel")),
    )(*args)


def _wkv_scan_kernel(r_ref, w_ref, k_ref, v_ref, a_ref, b_ref, an_ref, y_ref, s_ref, sa_ref, *, tt):
    @pl.when(pl.program_id(0) == 0)
    def _():
        s_ref[...] = jnp.zeros_like(s_ref)
        sa_ref[...] = jnp.zeros_like(sa_ref)

    def step(t, sa, a_next):
        vt = v_ref[t]
        y_parts = [None] * 2
        n_parts = [None] * 2
        for j in range(HEAD_DIM):
            s_new = (s_ref[j] * w_ref[t, pl.ds(j, 1), :]
                     + (sa * b_ref[t, pl.ds(j, 1), :] + vt * k_ref[t, pl.ds(j, 1), :]))
            s_ref[j] = s_new
            t_y = s_new * r_ref[t, pl.ds(j, 1), :]
            t_n = s_new * a_next(j)
            y_parts[j % 2] = t_y if y_parts[j % 2] is None else y_parts[j % 2] + t_y
            n_parts[j % 2] = t_n if n_parts[j % 2] is None else n_parts[j % 2] + t_n
        y_ref[t] = y_parts[0] + y_parts[1]
        return n_parts[0] + n_parts[1]

    sa = lax.fori_loop(0, tt - 1,
                       lambda t, sa: step(t, sa, lambda j: a_ref[t + 1, pl.ds(j, 1), :]),
                       sa_ref[...])
    sa_ref[...] = step(tt - 1, sa, lambda j: an_ref[0, pl.ds(j, 1), :])


def _wkv_scan(r, w, k, v, a, b):
    S, N, Ln = r.shape
    tt = min(32, S)
    seq = pl.BlockSpec((tt, N, Ln), lambda i: (i, 0, 0))
    return pl.pallas_call(
        functools.partial(_wkv_scan_kernel, tt=tt),
        name="wkv_scan",
        grid=(S // tt,),
        in_specs=[seq] * 6 + [_next_row_spec(tt, S, N, Ln)],
        out_specs=seq,
        out_shape=jax.ShapeDtypeStruct((S, N, Ln), F32),
        scratch_shapes=[pltpu.VMEM((N, N, Ln), F32), pltpu.VMEM((N, Ln), F32)],
        compiler_params=_cparams(("arbitrary",)),
    )(r, w, k, v, a, b, a)


def _rwkv_post_kernel(y_ref, r_ref, k_ref, v_ref, g_ref, vecs_ref, out_ref):
    y = y_ref[...]
    inv_n = 1.0 / HEAD_DIM
    yc = y - _group_sum(y, HEAD_DIM) * inv_n
    var = _group_sum(yc * yc, HEAD_DIM) * inv_n
    yn = yc * lax.rsqrt(var + LNX_EPS) * vecs_ref[0:1, :] + vecs_ref[1:2, :]
    bonus = _group_sum(r_ref[...] * k_ref[...] * vecs_ref[2:3, :], HEAD_DIM)
    out_ref[...] = ((yn + bonus * v_ref[...]) * g_ref[...]).astype(BF16)


def _rwkv_post(y, r, k, v, g, vecs):
    B, S, _ = y.shape
    ts = min(512, S)
    tok_spec = pl.BlockSpec((None, ts, GROUP_W), lambda b, i: (b, i, 0))
    return pl.pallas_call(
        _rwkv_post_kernel,
        name="rwkv_post",
        grid=(B, S // ts),
        in_specs=[tok_spec] * 5 + [pl.BlockSpec((8, GROUP_W), lambda b, i: (0, 0))],
        out_specs=tok_spec,
        out_shape=jax.ShapeDtypeStruct((B, S, GROUP_W), BF16),
        compiler_params=_cparams(("parallel", "parallel")),
    )(y, r, k, v, g, vecs)


def _pad_cols(w, width):
    pad = width - w.shape[-1]
    return jnp.concatenate([w, jnp.zeros(w.shape[:-1] + (pad,), w.dtype)], axis=-1)


def _rope_perm():
    p = np.arange(128)
    first = (p // 16) * DIFF_DIM + p % 16
    return np.concatenate([first, first + 16])


def _permute_w_in(w_in, vmix_down):
    L, D, _ = w_in.shape
    perm = _rope_perm()
    g0 = 9 * GROUP_W
    r0 = RWKV_START
    vd = jnp.concatenate([jnp.zeros((1, D, vmix_down.shape[-1]), w_in.dtype), vmix_down], axis=0)
    cols = [
        w_in[..., perm], w_in[..., GROUP_W + perm], w_in[..., 2 * GROUP_W:3 * GROUP_W],
        w_in[..., 3 * GROUP_W:6 * GROUP_W],
        w_in[..., 6 * GROUP_W:g0], w_in[..., g0 + 8:g0 + 8 + GROUP_W], _pad_cols(w_in[..., g0:g0 + 8], 128),
        w_in[..., r0:r0 + 768], _pad_cols(w_in[..., r0 + 768:r0 + 832], 128),
        _pad_cols(w_in[..., r0 + 832:r0 + 896], 128), _pad_cols(w_in[..., r0 + 896:r0 + 1056], 256),
        _pad_cols(vd, 128),
    ]
    return jnp.concatenate(cols, axis=-1).astype(BF16)


def _permute_mu(mu, vmix_mu):
    L = mu.shape[0]
    vm = jnp.concatenate([jnp.zeros((1, vmix_mu.shape[-1]), mu.dtype), vmix_mu], axis=0)
    cols = [mu[:, 0:768], _pad_cols(mu[:, 768:832], 128), _pad_cols(mu[:, 832:896], 128),
            _pad_cols(mu[:, 896:1056], 256), _pad_cols(vm, 128)]
    return jnp.concatenate(cols, axis=-1).reshape(L, 1, RWKV_W)


def _to_lanes(x):
    B, S, _ = x.shape
    return x.reshape(B, S, N_HEADS, HEAD_DIM).transpose(1, 3, 0, 2).reshape(S, HEAD_DIM, B * N_HEADS)


def _from_lanes(y, B):
    S = y.shape[0]
    return y.reshape(S, HEAD_DIM, B, N_HEADS).transpose(2, 0, 3, 1).reshape(B, S, GROUP_W)


def _head_scalars_to_lanes(x):
    B, S, _ = x.shape
    return x.transpose(1, 0, 2).reshape(S, B * N_HEADS)


def kernel(x, c, ada_w, ada_b, norm_g, ffn_w13, ffn_w2, w_in, w_out, diff_qk_g, diff_lambda, diff_subln_g, gdn_conv_w, gdn_a_log, gdn_dt_bias, gdn_norm_g, rwkv_mu, rwkv_w0, rwkv_w_up, rwkv_a0, rwkv_a_up, rwkv_g_up, rwkv_k_k, rwkv_k_a, rwkv_r_k, rwkv_lnx_w, rwkv_lnx_b, rwkv_vmix_down, rwkv_vmix_mu, rwkv_v0, rwkv_vmix_up):
    B, S, D = x.shape
    L = ada_w.shape[0]
    T = B * S

    mod = _mod_all(c, ada_w, ada_b)
    w13 = ffn_w13.astype(BF16)
    w2 = ffn_w2.astype(BF16)
    w_perm = _permute_w_in(w_in, rwkv_vmix_down)
    w_o = w_out.astype(BF16)
    mu = _permute_mu(rwkv_mu, rwkv_vmix_mu)

    inv = ROPE_THETA ** (-jnp.arange(0, DIFF_DIM, 2, dtype=F32) / DIFF_DIM)
    ang = jnp.arange(S, dtype=F32)[:, None] * inv[None, :]
    cos128 = jnp.tile(jnp.cos(ang), (1, 8))
    sin128 = jnp.tile(jnp.sin(ang), (1, 8))

    h = x.reshape(T, D)
    v_first = None
    for l in range(L):
        lam_init = 0.8 - 0.6 * math.exp(-0.3 * l)
        h = _ffn(h, mod[l], norm_g[l, 0], w13, w2, l, 0, 0, S)

        qkg = jnp.stack([jnp.tile(diff_qk_g[l, 0, :16], 8), jnp.tile(diff_qk_g[l, 0, 16:], 8),
                         jnp.tile(diff_qk_g[l, 1, :16], 8), jnp.tile(diff_qk_g[l, 1, 16:], 8)])
        attn, gdn, rw = _inproj(h, mod[l], norm_g[l, 1], w_perm, l, qkg, cos128, sin128, S)
        attn = attn.reshape(B, S, ATTN_W)
        gdn = gdn.reshape(B, S, GDN_W)
        rw = rw.reshape(B, S, RWKV_W)

        o_a = _diff_attention(attn, diff_lambda[l], diff_subln_g[l], lam_init)
        o_b = _sb_attention(attn)

        gq, gk, gv, gab = _gdn_prep(gdn, gdn_conv_w[l], gdn_a_log[l], gdn_dt_bias[l])
        go = _gdn_scan(_to_lanes(gq), _to_lanes(gk), _to_lanes(gv),
                       _head_scalars_to_lanes(gab[..., N_HEADS:2 * N_HEADS]),
                       _head_scalars_to_lanes(gab[..., 0:N_HEADS]))
        o_c = _gdn_post(_from_lanes(go, B), gdn, gdn_norm_g[l])

        zeros = jnp.zeros((GROUP_W,), F32)
        vecs = jnp.stack([rwkv_w0[l], rwkv_a0[l], rwkv_k_k[l], rwkv_k_a[l],
                          rwkv_v0[l - 1] if l > 0 else zeros, zeros, zeros, zeros])
        outs = _rwkv_prep(rw, mu[l], vecs, rwkv_w_up[l], rwkv_a_up[l], rwkv_g_up[l],
                          rwkv_vmix_up[l - 1] if l > 0 else None, v_first)
        rr, rw_, rk, rv, ra, rb, rg = outs[:7]
        if l == 0:
            v_first = outs[7]
        ry = _wkv_scan(*[_to_lanes(t) for t in (rr, rw_, rk, rv, ra, rb)])
        pvecs = jnp.stack([rwkv_lnx_w[l], rwkv_lnx_b[l], rwkv_r_k[l].reshape(GROUP_W),
                           zeros, zeros, zeros, zeros, zeros])
        o_d = _rwkv_post(_from_lanes(ry, B), rr, rk, rv, rg, pvecs)

        flat = lambda t: t.reshape(T, GROUP_W)
        h = _ffn(h, mod[l], norm_g[l, 2], w13, w2, l, 1, 6, S,
                 mixed=(flat(o_a), flat(o_b), flat(o_c), flat(o_d)), w_out=w_o)
    return h.reshape(B, S, D)
```

```python
import functools
import math

import numpy as np
import jax
import jax.numpy as jnp
from jax import lax
from jax.experimental import pallas as pl
from jax.experimental.pallas import tpu as pltpu

F32 = jnp.float32
BF16 = jnp.bfloat16
HIGHEST = lax.Precision.HIGHEST

D_MODEL = 1024
N_HEADS = 4
HEAD_DIM = 64
GROUP_W = N_HEADS * HEAD_DIM
DIFF_DIM = HEAD_DIM // 2
D_FF = 2816
N_MOD = 9
EPS = 1e-6
LNX_EPS = 64e-5
ROPE_THETA = 10000.0
LOG2E = 1.4426950408889634
CONV_K = 4
RWKV_START = 9 * GROUP_W + 2 * N_HEADS + GROUP_W

ATTN_W = 6 * GROUP_W
GDN_W = 3 * GROUP_W + GROUP_W + 128
RWKV_W = 3 * GROUP_W + 128 + 128 + 256 + 128
PROJ_W = ATTN_W + GDN_W + RWKV_W
VMEM_LIMIT = 56 * 1024 * 1024


def _cparams(sem):
    return pltpu.CompilerParams(dimension_semantics=sem, vmem_limit_bytes=VMEM_LIMIT)


def _sigmoid(x):
    return 1.0 / (1.0 + jnp.exp(-x))


def _softplus(x):
    return jnp.maximum(x, 0.0) + jnp.log(1.0 + jnp.exp(-jnp.abs(x)))


def _group_ones(n, group):
    r = lax.broadcasted_iota(jnp.int32, (n, n), 0) // group
    c = lax.broadcasted_iota(jnp.int32, (n, n), 1) // group
    return (r == c).astype(BF16)


def _group_sum(x, group):
    ones = _group_ones(x.shape[-1], group)
    hi = x.astype(BF16)
    lo = (x - hi.astype(F32)).astype(BF16)
    return (jnp.dot(hi, ones, preferred_element_type=F32)
            + jnp.dot(lo, ones, preferred_element_type=F32))


def _modulate(x, gain, shift, scale):
    ms = jnp.mean(x * x, axis=-1, keepdims=True)
    return (x * lax.rsqrt(ms + EPS) * gain) * (1.0 + scale) + shift


def _mod_kernel(c_ref, w_ref, b_ref, o_ref):
    c = c_ref[...]
    cond = c * _sigmoid(c)
    o_ref[...] = jnp.dot(cond, w_ref[...], precision=HIGHEST,
                         preferred_element_type=F32) + b_ref[...]


def _mod_all(c, ada_w, ada_b):
    L, D, N = ada_w.shape
    B = c.shape[0]
    tn = 1024
    out = pl.pallas_call(
        _mod_kernel,
        name="adaln_mod",
        grid=(L, N // tn),
        in_specs=[
            pl.BlockSpec((B, D), lambda l, n: (0, 0)),
            pl.BlockSpec((None, D, tn), lambda l, n: (l, 0, n)),
            pl.BlockSpec((None, 1, tn), lambda l, n: (l, 0, n)),
        ],
        out_specs=pl.BlockSpec((None, B, tn), lambda l, n: (l, 0, n)),
        out_shape=jax.ShapeDtypeStruct((L, B, N), F32),
        compiler_params=_cparams(("parallel", "parallel")),
    )(c, ada_w, ada_b.reshape(L, 1, N))
    return out.reshape(L, B, N_MOD, D)


def _ffn_kernel(*refs, mb, tf, mix):
    x_ref, mod_ref, g_ref, w13_ref, w2_ref = refs[:5]
    o_ref, act_ref = refs[-2:]
    x = x_ref[...]
    if mix:
        oa_ref, ob_ref, oc_ref, od_ref, w_ref, gc_ref, gd_ref = refs[5:12]
        gc = gc_ref[...]
        o_c = (oc_ref[...].astype(F32) * (gc * _sigmoid(gc))).astype(BF16)
        o_d = (od_ref[...].astype(F32) * gd_ref[...]).astype(BF16)
        proj = jnp.dot(oa_ref[...], w_ref[0:GROUP_W, :], preferred_element_type=F32)
        for g, o_g in ((1, ob_ref[...]), (2, o_c), (3, o_d)):
            proj += jnp.dot(o_g, w_ref[g * GROUP_W:(g + 1) * GROUP_W, :],
                            preferred_element_type=F32)
        x = x + mod_ref[5:6, :] * proj
    hn = _modulate(x, g_ref[...], mod_ref[mb:mb + 1, :], mod_ref[mb + 1:mb + 2, :]).astype(BF16)
    for f in range(D_FF // tf):
        gate = jnp.dot(hn, w13_ref[:, f * tf:(f + 1) * tf], preferred_element_type=F32)
        up = jnp.dot(hn, w13_ref[:, D_FF + f * tf:D_FF + (f + 1) * tf], preferred_element_type=F32)
        act_ref[:, f * tf:(f + 1) * tf] = (gate * _sigmoid(gate) * up).astype(BF16)
    acc = jnp.dot(act_ref[...], w2_ref[...], preferred_element_type=F32)
    o_ref[...] = x + 0.5 * mod_ref[mb + 2:mb + 3, :] * acc


def _ffn(h, mod_l, gain, w13, w2, l, j, mb, S, mixed=None, w_out=None, gates=None):
    T, D = h.shape
    tm = min(1024, S)
    tf = 256
    tpb = S // tm
    resident = dict(pipeline_mode=pl.Buffered(1))
    in_specs = [
        pl.BlockSpec((tm, D), lambda i: (i, 0)),
        pl.BlockSpec((None, N_MOD, D), lambda i: (i // tpb, 0, 0)),
        pl.BlockSpec((1, D), lambda i: (0, 0)),
        pl.BlockSpec((None, None, D, 2 * D_FF), lambda i: (l, j, 0, 0), **resident),
        pl.BlockSpec((None, None, D_FF, D), lambda i: (l, j, 0, 0), **resident),
    ]
    args = [h, mod_l, gain.reshape(1, D), w13, w2]
    if mixed is not None:
        in_specs += [pl.BlockSpec((tm, GROUP_W), lambda i: (i, 0))] * 4
        in_specs += [pl.BlockSpec((None, D, D), lambda i: (l, 0, 0), **resident)]
        in_specs += [pl.BlockSpec((tm, GROUP_W), lambda i: (i, 3)),
                     pl.BlockSpec((tm, GROUP_W), lambda i: (i, 0))]
        args += list(mixed) + [w_out] + list(gates)
    return pl.pallas_call(
        functools.partial(_ffn_kernel, mb=mb, tf=tf, mix=mixed is not None),
        name="ffn",
        grid=(T // tm,),
        in_specs=in_specs,
        out_specs=pl.BlockSpec((tm, D), lambda i: (i, 0)),
        out_shape=jax.ShapeDtypeStruct((T, D), F32),
        scratch_shapes=[pltpu.VMEM((tm, D_FF), BF16)],
        compiler_params=_cparams(("parallel",)),
    )(*args)


def _inproj_kernel(x_ref, mod_ref, g_ref, w_ref, qkg_ref, cos_ref, sin_ref,
                   attn_ref, gdn_ref, rwkv_ref):
    hn = _modulate(x_ref[...], g_ref[...], mod_ref[3:4, :], mod_ref[4:5, :]).astype(BF16)

    pa = jnp.dot(hn, w_ref[:, 0:3 * GROUP_W], preferred_element_type=F32)
    cos = cos_ref[...]
    sin = sin_ref[...]
    for idx, scale in ((0, DIFF_DIM ** -0.5 * LOG2E), (1, 1.0)):
        x1 = pa[:, idx * 256:idx * 256 + 128]
        x2 = pa[:, idx * 256 + 128:idx * 256 + 256]
        ms = _group_sum(x1 * x1 + x2 * x2, 16) * (1.0 / DIFF_DIM)
        inv = lax.rsqrt(ms + EPS)
        n1 = x1 * inv * qkg_ref[2 * idx:2 * idx + 1, :]
        n2 = x2 * inv * qkg_ref[2 * idx + 1:2 * idx + 2, :]
        attn_ref[:, idx * 256:idx * 256 + 128] = ((n1 * cos - n2 * sin) * scale).astype(BF16)
        attn_ref[:, idx * 256 + 128:idx * 256 + 256] = ((n2 * cos + n1 * sin) * scale).astype(BF16)
    attn_ref[:, 512:768] = pa[:, 512:768].astype(BF16)

    pb = jnp.dot(hn, w_ref[:, 3 * GROUP_W:ATTN_W], preferred_element_type=F32)
    attn_ref[:, 768:1024] = (pb[:, 0:256] * (HEAD_DIM ** -0.5)).astype(BF16)
    attn_ref[:, 1024:1536] = pb[:, 256:768].astype(BF16)

    gdn_ref[...] = jnp.dot(hn, w_ref[:, ATTN_W:ATTN_W + GDN_W], preferred_element_type=F32)
    rwkv_ref[...] = jnp.dot(hn, w_ref[:, ATTN_W + GDN_W:PROJ_W], preferred_element_type=F32)


def _inproj(h, mod_l, gain, w_perm, l, qkg, cos128, sin128, S):
    T, D = h.shape
    tm = min(512, S)
    tpb = S // tm
    return pl.pallas_call(
        _inproj_kernel,
        name="inproj",
        grid=(T // tm,),
        in_specs=[
            pl.BlockSpec((tm, D), lambda i: (i, 0)),
            pl.BlockSpec((None, N_MOD, D), lambda i: (i // tpb, 0, 0)),
            pl.BlockSpec((1, D), lambda i: (0, 0)),
            pl.BlockSpec((None, D, PROJ_W), lambda i: (l, 0, 0)),
            pl.BlockSpec((4, 128), lambda i: (0, 0)),
            pl.BlockSpec((tm, 128), lambda i: (i % tpb, 0)),
            pl.BlockSpec((tm, 128), lambda i: (i % tpb, 0)),
        ],
        out_specs=[
            pl.BlockSpec((tm, ATTN_W), lambda i: (i, 0)),
            pl.BlockSpec((tm, GDN_W), lambda i: (i, 0)),
            pl.BlockSpec((tm, RWKV_W), lambda i: (i, 0)),
        ],
        out_shape=[
            jax.ShapeDtypeStruct((T, ATTN_W), BF16),
            jax.ShapeDtypeStruct((T, GDN_W), F32),
            jax.ShapeDtypeStruct((T, RWKV_W), F32),
        ],
        compiler_params=_cparams(("parallel",)),
    )(h, mod_l, gain.reshape(1, D), w_perm, qkg, cos128, sin128)


def _diff_kernel(q_ref, k_ref, vt_ref, lamv_ref, g_ref, o_ref, qc_ref, acc_ref,
                 *, lam_init, tq, tk):
    i = pl.program_id(1)
    q = q_ref[...]
    lane = lax.broadcasted_iota(jnp.int32, (1, GROUP_W), 1)
    sub = (lane % 128) // 16
    for h in range(N_HEADS):
        qc_ref[h, 0:tq, :] = jnp.where(sub == 2 * h, q, jnp.zeros_like(q))
        qc_ref[h, tq:2 * tq, :] = jnp.where(sub == 2 * h + 1, q, jnp.zeros_like(q))
    acc_ref[...] = jnp.zeros_like(acc_ref)

    def tile(kt, carry, masked):
        start = pl.multiple_of(kt * tk, tk)
        k = k_ref[pl.ds(start, tk), :]
        heads = range(N_HEADS)
        ss = [lax.dot_general(k, qc_ref[h], (((1,), (1,)), ((), ())), preferred_element_type=F32)
              for h in heads]
        out, ps, alphas = [], [], []
        for h in heads:
            m, l = carry[h]
            s = ss[h]
            if masked:
                key = kt * tk + lax.broadcasted_iota(jnp.int32, (tk, 2 * tq), 0)
                qpos = i * tq + lax.broadcasted_iota(jnp.int32, (tk, 2 * tq), 1) % tq
                s = jnp.where(key <= qpos, s, -jnp.inf)
            m_new = jnp.maximum(m, jnp.max(s, axis=0, keepdims=True))
            alpha = jnp.exp2(m - m_new)
            p = jnp.exp2(s - m_new)
            out.append((m_new, alpha * l + jnp.sum(p, axis=0, keepdims=True)))
            ps.append(p.astype(BF16))
            alphas.append(alpha)
        for h in heads:
            vt = vt_ref[kt, h * HEAD_DIM:(h + 1) * HEAD_DIM, :]
            acc_ref[h] = alphas[h] * acc_ref[h] + jnp.dot(vt, ps[h], preferred_element_type=F32)
        return tuple(out)

    init = tuple((jnp.full((1, 2 * tq), -jnp.inf, F32), jnp.zeros((1, 2 * tq), F32))
                 for _ in range(N_HEADS))
    kd = (i * tq) // tk
    carry = lax.fori_loop(0, kd, lambda kt, c: tile(kt, c, False), init)
    carry = tile(kd, carry, True)

    lv = lamv_ref[...]
    lam = (jnp.exp(jnp.sum(lv[0:1, :] * lv[1:2, :], axis=1, keepdims=True))
           - jnp.exp(jnp.sum(lv[2:3, :] * lv[3:4, :], axis=1, keepdims=True)) + lam_init)
    rows = []
    for h in range(N_HEADS):
        o = acc_ref[h] / carry[h][1]
        oh = o[:, :tq] - lam * o[:, tq:]
        ms = jnp.mean(oh * oh, axis=0, keepdims=True)
        rows.append(oh * lax.rsqrt(ms + EPS) * g_ref[h * HEAD_DIM:(h + 1) * HEAD_DIM, :]
                    * (1.0 - lam_init))
    o_ref[...] = jnp.concatenate(rows, axis=0).T.astype(BF16)


def _kv_tiles_transposed(v, tk):
    B, S, C = v.shape
    return v.reshape(B, S // tk, tk, C).transpose(0, 1, 3, 2)


def _diff_attention(attn, lamv, subln_g, lam_init):
    B, S, _ = attn.shape
    tq = min(256, S)
    tk = min(256, S)
    vt = _kv_tiles_transposed(attn[..., 2 * GROUP_W:3 * GROUP_W], tk)
    gain = jnp.broadcast_to(jnp.tile(subln_g, N_HEADS)[:, None], (GROUP_W, tq))
    return pl.pallas_call(
        functools.partial(_diff_kernel, lam_init=lam_init, tq=tq, tk=tk),
        name="diff_attn",
        grid=(B, S // tq),
        in_specs=[
            pl.BlockSpec((None, tq, GROUP_W), lambda b, i: (b, i, 0)),
            pl.BlockSpec((None, S, GROUP_W), lambda b, i: (b, 0, 1)),
            pl.BlockSpec((None, S // tk, GROUP_W, tk), lambda b, i: (b, 0, 0, 0)),
            pl.BlockSpec((4, DIFF_DIM), lambda b, i: (0, 0)),
            pl.BlockSpec((GROUP_W, tq), lambda b, i: (0, 0)),
        ],
        out_specs=pl.BlockSpec((None, tq, GROUP_W), lambda b, i: (b, i, 0)),
        out_shape=jax.ShapeDtypeStruct((B, S, GROUP_W), BF16),
        scratch_shapes=[pltpu.VMEM((N_HEADS, 2 * tq, GROUP_W), BF16),
                        pltpu.VMEM((N_HEADS, HEAD_DIM, 2 * tq), F32)],
        compiler_params=_cparams(("parallel", "parallel")),
    )(attn, attn, vt, lamv, gain)


def _sb_kernel(q_ref, k_ref, vt_ref, o_ref, qh_ref, tri_ref, acc_ref, *, t):
    i = pl.program_id(1)
    q = q_ref[...]
    lane = lax.broadcasted_iota(jnp.int32, (1, GROUP_W), 1)
    r_i = lax.broadcasted_iota(jnp.int32, (t, t), 0)
    c_i = lax.broadcasted_iota(jnp.int32, (t, t), 1)
    tri = (c_i > r_i).astype(BF16)
    tri_ref[...] = tri
    for h in range(N_HEADS):
        qh_ref[h] = jnp.where(lane // HEAD_DIM == h, q, jnp.zeros_like(q))
    acc_ref[...] = jnp.zeros_like(acc_ref)

    def tile(kt, carry, diag):
        start = pl.multiple_of(kt * t, t)
        k = k_ref[pl.ds(start, t), :]
        heads = range(N_HEADS)
        strict = r_i < c_i
        zs = [lax.dot_general(k, qh_ref[h], (((1,), (1,)), ((), ())), preferred_element_type=F32)
              for h in heads]
        lss, lks, sums = [], [], []
        for h in heads:
            z = zs[h]
            neg_abs = lax.bitcast_convert_type(
                lax.bitcast_convert_type(z, jnp.uint32) | jnp.uint32(0x80000000), F32)
            ls = jnp.minimum(z, 0.0) - jnp.log(1.0 + jnp.exp(neg_abs))
            lk = ls - z
            if diag:
                lk = jnp.where(strict, lk, 0.0)
            lss.append(ls)
            lks.append(lk.astype(BF16))
            sums.append(jnp.sum(lk, axis=0, keepdims=True))
        bts = [jnp.dot(tri_ref[...], lks[h], preferred_element_type=F32) for h in heads]
        pas = []
        for h in heads:
            a = jnp.exp(lss[h] + (bts[h] + carry[h]))
            if diag:
                a = jnp.where(strict, a, 0.0)
            pas.append(a.astype(BF16))
        for h in heads:
            vt = vt_ref[kt, h * HEAD_DIM:(h + 1) * HEAD_DIM, :]
            acc_ref[h] += jnp.dot(vt, pas[h], preferred_element_type=F32)
        return tuple(carry[h] + sums[h] for h in heads)

    carry = tile(i, tuple(jnp.zeros((1, t), F32) for _ in range(N_HEADS)), True)
    lax.fori_loop(0, i, lambda n, c: tile(i - 1 - n, c, False), carry)
    o_ref[...] = jnp.concatenate([acc_ref[h] for h in range(N_HEADS)], axis=0).T.astype(BF16)


def _sb_attention(attn):
    B, S, _ = attn.shape
    t = min(256, S)
    vt = _kv_tiles_transposed(attn[..., 5 * GROUP_W:6 * GROUP_W], t)
    return pl.pallas_call(
        functools.partial(_sb_kernel, t=t),
        name="sb_attn",
        grid=(B, S // t),
        in_specs=[
            pl.BlockSpec((None, t, GROUP_W), lambda b, i: (b, i, 3)),
            pl.BlockSpec((None, S, GROUP_W), lambda b, i: (b, 0, 4)),
            pl.BlockSpec((None, S // t, GROUP_W, t), lambda b, i: (b, 0, 0, 0)),
        ],
        out_specs=pl.BlockSpec((None, t, GROUP_W), lambda b, i: (b, i, 0)),
        out_shape=jax.ShapeDtypeStruct((B, S, GROUP_W), BF16),
        scratch_shapes=[pltpu.VMEM((N_HEADS, t, GROUP_W), BF16),
                        pltpu.VMEM((t, t), BF16),
                        pltpu.VMEM((N_HEADS, HEAD_DIM, t), F32)],
        compiler_params=_cparams(("parallel", "parallel")),
    )(attn, attn, vt)


def _gdn_prep_kernel(x_ref, prev_ref, small_ref, cw_ref, alog_ref, dtb_ref,
                     q_ref, k_ref, v_ref, ab_ref, xs_ref, *, ts):
    i = pl.program_id(1)
    prev = prev_ref[...]
    xs_ref[0:8, :] = jnp.where(i == 0, jnp.zeros_like(prev), prev)
    xs_ref[8:8 + ts, :] = x_ref[...]
    conv = jnp.zeros((ts, 3 * GROUP_W), F32)
    for j in range(CONV_K):
        conv = conv + xs_ref[pl.ds(8 - (CONV_K - 1) + j, ts), :] * cw_ref[j:j + 1, :]
    y = conv * _sigmoid(conv)
    q = y[:, 0:GROUP_W]
    k = y[:, GROUP_W:2 * GROUP_W]
    q_ref[...] = (q * lax.rsqrt(_group_sum(q * q, HEAD_DIM) + EPS) * (HEAD_DIM ** -0.5)).astype(BF16)
    k_ref[...] = (k * lax.rsqrt(_group_sum(k * k, HEAD_DIM) + EPS)).astype(BF16)
    v_ref[...] = y[:, 2 * GROUP_W:3 * GROUP_W].astype(BF16)
    small = small_ref[...]
    beta = _sigmoid(small)
    alpha = jnp.exp(-jnp.exp(alog_ref[...]) * _softplus(small + dtb_ref[...]))
    lane = lax.broadcasted_iota(jnp.int32, (1, 128), 1)
    ab_ref[...] = jnp.where(lane < N_HEADS, beta, alpha)[:, 0:2 * N_HEADS]


def _gdn_prep(gdn, conv_w, a_log, dt_bias):
    B, S, _ = gdn.shape
    ts = min(512, S)
    pad = jnp.zeros((N_HEADS,), F32)
    alog = jnp.concatenate([pad, a_log, jnp.zeros((120,), F32)]).reshape(1, 128)
    dtb = jnp.concatenate([pad, dt_bias, jnp.zeros((120,), F32)]).reshape(1, 128)
    tok = jax.ShapeDtypeStruct((B, S, GROUP_W), F32)
    return pl.pallas_call(
        functools.partial(_gdn_prep_kernel, ts=ts),
        name="gdn_prep",
        grid=(B, S // ts),
        in_specs=[
            pl.BlockSpec((None, ts, 3 * GROUP_W), lambda b, i: (b, i, 0)),
            pl.BlockSpec((None, 8, 3 * GROUP_W), lambda b, i: (b, jnp.maximum(i * (ts // 8) - 1, 0), 0)),
            pl.BlockSpec((None, ts, 128), lambda b, i: (b, i, 8)),
            pl.BlockSpec((CONV_K, 3 * GROUP_W), lambda b, i: (0, 0)),
            pl.BlockSpec((1, 128), lambda b, i: (0, 0)),
            pl.BlockSpec((1, 128), lambda b, i: (0, 0)),
        ],
        out_specs=[
            pl.BlockSpec((None, ts, GROUP_W), lambda b, i: (b, i, 0)),
            pl.BlockSpec((None, ts, GROUP_W), lambda b, i: (b, i, 0)),
            pl.BlockSpec((None, ts, GROUP_W), lambda b, i: (b, i, 0)),
            pl.BlockSpec((None, ts, 2 * N_HEADS), lambda b, i: (b, i, 0)),
        ],
        out_shape=[tok.update(dtype=BF16)] * 3 + [jax.ShapeDtypeStruct((B, S, 2 * N_HEADS), F32)],
        scratch_shapes=[pltpu.VMEM((ts + 8, 3 * GROUP_W), F32)],
        compiler_params=_cparams(("parallel", "parallel")),
    )(gdn, gdn, gdn, conv_w, alog, dtb)


def _gdn_scan_kernel(qb_ref, kb_ref, vb_ref, al_ref, be_ref, knb_ref, ng_ref, o_ref,
                     s_ref, ks_ref, q_ref, k_ref, v_ref, kn_ref, *, tt):
    @pl.when(pl.program_id(0) == 0)
    def _():
        s_ref[...] = jnp.zeros_like(s_ref)
        ks_ref[...] = jnp.zeros_like(ks_ref)

    q_ref[...] = qb_ref[...].astype(F32)
    k_ref[...] = kb_ref[...].astype(F32)
    v_ref[...] = vb_ref[...].astype(F32)
    kn_ref[...] = knb_ref[...].astype(F32)

    def step(t, ks, k_next):
        al = al_ref[pl.ds(t, 1), :]
        d = be_ref[pl.ds(t, 1), :] * (v_ref[t] - al * ks)
        o_parts = [None] * 2
        n_parts = [None] * 2
        for kk in range(HEAD_DIM):
            s_new = al * s_ref[kk] + k_ref[t, pl.ds(kk, 1), :] * d
            s_ref[kk] = s_new
            t_o = s_new * q_ref[t, pl.ds(kk, 1), :]
            t_n = s_new * k_next(kk)
            o_parts[kk % 2] = t_o if o_parts[kk % 2] is None else o_parts[kk % 2] + t_o
            n_parts[kk % 2] = t_n if n_parts[kk % 2] is None else n_parts[kk % 2] + t_n
        o = o_parts[0] + o_parts[1]
        ms = jnp.mean(o * o, axis=0, keepdims=True)
        o_ref[t] = (o * lax.rsqrt(ms + EPS) * ng_ref[...]).astype(BF16)
        return n_parts[0] + n_parts[1]

    ks = lax.fori_loop(0, tt - 1,
                       lambda t, ks: step(t, ks, lambda kk: k_ref[t + 1, pl.ds(kk, 1), :]),
                       ks_ref[...])
    ks_ref[...] = step(tt - 1, ks, lambda kk: kn_ref[0, pl.ds(kk, 1), :])


def _next_row_spec(tt, S, N, Ln):
    return pl.BlockSpec((1, N, Ln), lambda i: (jnp.minimum((i + 1) * tt, S - 1), 0, 0))


def _gdn_scan(q, k, v, al, be, norm_g):
    S, N, Ln = q.shape
    tt = min(32, S)
    seq = pl.BlockSpec((tt, N, Ln), lambda i: (i, 0, 0))
    vec = pl.BlockSpec((tt, Ln), lambda i: (i, 0))
    return pl.pallas_call(
        functools.partial(_gdn_scan_kernel, tt=tt),
        name="gdn_scan",
        grid=(S // tt,),
        in_specs=[seq, seq, seq, vec, vec, _next_row_spec(tt, S, N, Ln),
                  pl.BlockSpec((N, Ln), lambda i: (0, 0))],
        out_specs=seq,
        out_shape=jax.ShapeDtypeStruct((S, N, Ln), BF16),
        scratch_shapes=[pltpu.VMEM((N, N, Ln), F32), pltpu.VMEM((N, Ln), F32)]
        + [pltpu.VMEM((tt, N, Ln), F32)] * 3 + [pltpu.VMEM((1, N, Ln), F32)],
        compiler_params=_cparams(("arbitrary",)),
    )(q, k, v, al, be, k, jnp.broadcast_to(norm_g[:, None], (N, Ln)))


def _rwkv_prep_kernel(*refs, ts, first):
    if first:
        (x_ref, prev_ref, mu_ref, vecs_ref, wup_ref, aup_ref, gup_ref,
         r_ref, w_ref, k_ref, v_ref, a_ref, b_ref, g_ref, vf_ref, xs_ref) = refs
    else:
        (x_ref, prev_ref, mu_ref, vecs_ref, wup_ref, aup_ref, gup_ref, vup_ref, vfin_ref,
         r_ref, w_ref, k_ref, v_ref, a_ref, b_ref, g_ref, xs_ref) = refs
    i = pl.program_id(1)
    prev = prev_ref[...]
    xs_ref[0:8, :] = jnp.where(i == 0, jnp.zeros_like(prev), prev)
    x = x_ref[...]
    xs_ref[8:8 + ts, :] = x
    xm = x + (xs_ref[pl.ds(7, ts), :] - x) * mu_ref[...]
    r = xm[:, 0:256]
    k = xm[:, 256:512]
    v = xm[:, 512:768]
    wd = xm[:, 768:896]
    ad = xm[:, 896:1024]
    gd = xm[:, 1024:1280]
    w0, a0, k_k, k_a = vecs_ref[0:1, :], vecs_ref[1:2, :], vecs_ref[2:3, :], vecs_ref[3:4, :]
    dot = lambda x, w: jnp.dot(x.astype(BF16), w, preferred_element_type=F32)
    w_log = -_softplus(-(w0 + dot(jnp.tanh(wd), wup_ref[...]))) - 0.5
    a = _sigmoid(a0 + dot(ad, aup_ref[...]))
    g_ref[...] = dot(_sigmoid(gd), gup_ref[...])
    if first:
        vf_ref[...] = v
    else:
        vd = xm[:, 1280:1408]
        v = v + (vfin_ref[...] - v) * _sigmoid(vecs_ref[4:5, :] + dot(vd, vup_ref[...]))
    kk = k * k_k
    kk = kk * lax.rsqrt(_group_sum(kk * kk, HEAD_DIM) + EPS)
    r_ref[...] = r.astype(BF16)
    w_ref[...] = jnp.exp(-jnp.exp(w_log))
    k_ref[...] = (k * (1.0 + (a - 1.0) * k_a)).astype(BF16)
    v_ref[...] = v.astype(BF16)
    a_ref[...] = (-kk).astype(BF16)
    b_ref[...] = (kk * a).astype(BF16)


def _pad_rows(w, rows):
    return jnp.concatenate([w, jnp.zeros((rows - w.shape[0], w.shape[1]), w.dtype)],
                           axis=0).astype(BF16)


def _rwkv_prep(rw, mu, vecs, w_up, a_up, g_up, vmix_up, v_first):
    B, S, _ = rw.shape
    ts = min(512, S)
    first = v_first is None
    tok_spec = pl.BlockSpec((None, ts, GROUP_W), lambda b, i: (b, i, 0))
    const = lambda shape: pl.BlockSpec(shape, lambda b, i: (0,) * len(shape))
    in_specs = [
        pl.BlockSpec((None, ts, RWKV_W), lambda b, i: (b, i, 0)),
        pl.BlockSpec((None, 8, RWKV_W), lambda b, i: (b, jnp.maximum(i * (ts // 8) - 1, 0), 0)),
        const((1, RWKV_W)), const((8, GROUP_W)),
        const((128, GROUP_W)), const((128, GROUP_W)), const((256, GROUP_W)),
    ]
    args = [rw, rw, mu, vecs, _pad_rows(w_up, 128), _pad_rows(a_up, 128), _pad_rows(g_up, 256)]
    if not first:
        in_specs += [const((128, GROUP_W)), tok_spec]
        args += [_pad_rows(vmix_up, 128), v_first]
    tok = jax.ShapeDtypeStruct((B, S, GROUP_W), F32)
    n_out = 8 if first else 7
    return pl.pallas_call(
        functools.partial(_rwkv_prep_kernel, ts=ts, first=first),
        name="rwkv_prep",
        grid=(B, S // ts),
        in_specs=in_specs,
        out_specs=[tok_spec] * n_out,
        out_shape=[tok.update(dtype=BF16 if n in (0, 2, 3, 4, 5) else F32) for n in range(n_out)],
        scratch_shapes=[pltpu.VMEM((ts + 8, RWKV_W), F32)],
        compiler_params=_cparams(("parallel", "parallel")),
    )(*args)


def _wkv_scan_kernel(rb_ref, w_ref, kb_ref, vb_ref, ab_ref, bb_ref, anb_ref, tab_ref, y_ref,
                     s_ref, sa_ref, r_ref, k_ref, v_ref, a_ref, b_ref, an_ref, *, tt):
    @pl.when(pl.program_id(0) == 0)
    def _():
        s_ref[...] = jnp.zeros_like(s_ref)
        sa_ref[...] = jnp.zeros_like(sa_ref)

    for dst, src in ((r_ref, rb_ref), (k_ref, kb_ref), (v_ref, vb_ref), (a_ref, ab_ref),
                     (b_ref, bb_ref), (an_ref, anb_ref)):
        dst[...] = src[...].astype(F32)

    def step(t, sa, a_next):
        vt = v_ref[t]
        y_parts = [None] * 2
        n_parts = [None] * 2
        for j in range(HEAD_DIM):
            s_new = (s_ref[j] * w_ref[t, pl.ds(j, 1), :]
                     + (sa * b_ref[t, pl.ds(j, 1), :] + vt * k_ref[t, pl.ds(j, 1), :]))
            s_ref[j] = s_new
            t_y = s_new * r_ref[t, pl.ds(j, 1), :]
            t_n = s_new * a_next(j)
            y_parts[j % 2] = t_y if y_parts[j % 2] is None else y_parts[j % 2] + t_y
            n_parts[j % 2] = t_n if n_parts[j % 2] is None else n_parts[j % 2] + t_n
        y = y_parts[0] + y_parts[1]
        yc = y - jnp.mean(y, axis=0, keepdims=True)
        var = jnp.mean(yc * yc, axis=0, keepdims=True)
        yn = yc * lax.rsqrt(var + LNX_EPS) * tab_ref[0] + tab_ref[1]
        bonus = jnp.sum(r_ref[t] * k_ref[t] * tab_ref[2], axis=0, keepdims=True)
        y_ref[t] = (yn + bonus * vt).astype(BF16)
        return n_parts[0] + n_parts[1]

    sa = lax.fori_loop(0, tt - 1,
                       lambda t, sa: step(t, sa, lambda j: a_ref[t + 1, pl.ds(j, 1), :]),
                       sa_ref[...])
    sa_ref[...] = step(tt - 1, sa, lambda j: an_ref[0, pl.ds(j, 1), :])


def _wkv_scan(r, w, k, v, a, b, tables):
    S, N, Ln = r.shape
    tt = min(32, S)
    seq = pl.BlockSpec((tt, N, Ln), lambda i: (i, 0, 0))
    return pl.pallas_call(
        functools.partial(_wkv_scan_kernel, tt=tt),
        name="wkv_scan",
        grid=(S // tt,),
        in_specs=[seq] * 6 + [_next_row_spec(tt, S, N, Ln),
                              pl.BlockSpec((3, N, Ln), lambda i: (0, 0, 0))],
        out_specs=seq,
        out_shape=jax.ShapeDtypeStruct((S, N, Ln), BF16),
        scratch_shapes=[pltpu.VMEM((N, N, Ln), F32), pltpu.VMEM((N, Ln), F32)]
        + [pltpu.VMEM((tt, N, Ln), F32)] * 5 + [pltpu.VMEM((1, N, Ln), F32)],
        compiler_params=_cparams(("arbitrary",)),
    )(r, w, k, v, a, b, a, tables)


def _pad_cols(w, width):
    pad = width - w.shape[-1]
    return jnp.concatenate([w, jnp.zeros(w.shape[:-1] + (pad,), w.dtype)], axis=-1)


def _rope_perm():
    p = np.arange(128)
    first = (p // 16) * DIFF_DIM + p % 16
    return np.concatenate([first, first + 16])


def _permute_w_in(w_in, vmix_down):
    L, D, _ = w_in.shape
    perm = _rope_perm()
    g0 = 9 * GROUP_W
    r0 = RWKV_START
    vd = jnp.concatenate([jnp.zeros((1, D, vmix_down.shape[-1]), w_in.dtype), vmix_down], axis=0)
    cols = [
        w_in[..., perm], w_in[..., GROUP_W + perm], w_in[..., 2 * GROUP_W:3 * GROUP_W],
        w_in[..., 3 * GROUP_W:6 * GROUP_W],
        w_in[..., 6 * GROUP_W:g0], w_in[..., g0 + 8:g0 + 8 + GROUP_W], _pad_cols(w_in[..., g0:g0 + 8], 128),
        w_in[..., r0:r0 + 768], _pad_cols(w_in[..., r0 + 768:r0 + 832], 128),
        _pad_cols(w_in[..., r0 + 832:r0 + 896], 128), _pad_cols(w_in[..., r0 + 896:r0 + 1056], 256),
        _pad_cols(vd, 128),
    ]
    return jnp.concatenate(cols, axis=-1).astype(BF16)


def _permute_mu(mu, vmix_mu):
    L = mu.shape[0]
    vm = jnp.concatenate([jnp.zeros((1, vmix_mu.shape[-1]), mu.dtype), vmix_mu], axis=0)
    cols = [mu[:, 0:768], _pad_cols(mu[:, 768:832], 128), _pad_cols(mu[:, 832:896], 128),
            _pad_cols(mu[:, 896:1056], 256), _pad_cols(vm, 128)]
    return jnp.concatenate(cols, axis=-1).reshape(L, 1, RWKV_W)


def _to_lanes(x):
    B, S, _ = x.shape
    return x.reshape(B, S, N_HEADS, HEAD_DIM).transpose(1, 3, 0, 2).reshape(S, HEAD_DIM, B * N_HEADS)


def _from_lanes(y, B):
    S = y.shape[0]
    return y.reshape(S, HEAD_DIM, B, N_HEADS).transpose(2, 0, 3, 1).reshape(B, S, GROUP_W)


def _per_head_table(x, B):
    return jnp.tile(x.T, (1, B))


def _head_scalars_to_lanes(x):
    B, S, _ = x.shape
    return x.transpose(1, 0, 2).reshape(S, B * N_HEADS)


def kernel(x, c, ada_w, ada_b, norm_g, ffn_w13, ffn_w2, w_in, w_out, diff_qk_g, diff_lambda, diff_subln_g, gdn_conv_w, gdn_a_log, gdn_dt_bias, gdn_norm_g, rwkv_mu, rwkv_w0, rwkv_w_up, rwkv_a0, rwkv_a_up, rwkv_g_up, rwkv_k_k, rwkv_k_a, rwkv_r_k, rwkv_lnx_w, rwkv_lnx_b, rwkv_vmix_down, rwkv_vmix_mu, rwkv_v0, rwkv_vmix_up):
    B, S, D = x.shape
    L = ada_w.shape[0]
    T = B * S

    mod = _mod_all(c, ada_w, ada_b)
    w13 = ffn_w13.astype(BF16)
    w2 = ffn_w2.astype(BF16)
    w_perm = _permute_w_in(w_in, rwkv_vmix_down)
    w_o = w_out.astype(BF16)
    mu = _permute_mu(rwkv_mu, rwkv_vmix_mu)

    inv = ROPE_THETA ** (-jnp.arange(0, DIFF_DIM, 2, dtype=F32) / DIFF_DIM)
    ang = jnp.arange(S, dtype=F32)[:, None] * inv[None, :]
    cos128 = jnp.tile(jnp.cos(ang), (1, 8))
    sin128 = jnp.tile(jnp.sin(ang), (1, 8))

    h = x.reshape(T, D)
    v_first = None
    for l in range(L):
        lam_init = 0.8 - 0.6 * math.exp(-0.3 * l)
        h = _ffn(h, mod[l], norm_g[l, 0], w13, w2, l, 0, 0, S)

        qkg = jnp.stack([jnp.tile(diff_qk_g[l, 0, :16], 8), jnp.tile(diff_qk_g[l, 0, 16:], 8),
                         jnp.tile(diff_qk_g[l, 1, :16], 8), jnp.tile(diff_qk_g[l, 1, 16:], 8)])
        attn, gdn, rw = _inproj(h, mod[l], norm_g[l, 1], w_perm, l, qkg, cos128, sin128, S)
        attn = attn.reshape(B, S, ATTN_W)
        gdn = gdn.reshape(B, S, GDN_W)
        rw = rw.reshape(B, S, RWKV_W)

        o_a = _diff_attention(attn, diff_lambda[l], diff_subln_g[l], lam_init)
        o_b = _sb_attention(attn)

        gq, gk, gv, gab = _gdn_prep(gdn, gdn_conv_w[l], gdn_a_log[l], gdn_dt_bias[l])
        go = _gdn_scan(_to_lanes(gq), _to_lanes(gk), _to_lanes(gv),
                       _head_scalars_to_lanes(gab[..., N_HEADS:2 * N_HEADS]),
                       _head_scalars_to_lanes(gab[..., 0:N_HEADS]), gdn_norm_g[l])
        o_c = _from_lanes(go, B)

        zeros = jnp.zeros((GROUP_W,), F32)
        vecs = jnp.stack([rwkv_w0[l], rwkv_a0[l], rwkv_k_k[l], rwkv_k_a[l],
                          rwkv_v0[l - 1] if l > 0 else zeros, zeros, zeros, zeros])
        outs = _rwkv_prep(rw, mu[l], vecs, rwkv_w_up[l], rwkv_a_up[l], rwkv_g_up[l],
                          rwkv_vmix_up[l - 1] if l > 0 else None, v_first)
        rr, rw_, rk, rv, ra, rb, rg = outs[:7]
        if l == 0:
            v_first = outs[7]
        tables = jnp.stack([_per_head_table(t.reshape(N_HEADS, HEAD_DIM), B)
                            for t in (rwkv_lnx_w[l], rwkv_lnx_b[l], rwkv_r_k[l])])
        ry = _wkv_scan(*[_to_lanes(t) for t in (rr, rw_, rk, rv, ra, rb)], tables)
        o_d = _from_lanes(ry, B)

        flat = lambda t: t.reshape(T, GROUP_W)
        h = _ffn(h, mod[l], norm_g[l, 2], w13, w2, l, 1, 6, S,
                 mixed=(flat(o_a), flat(o_b), flat(o_c), flat(o_d)), w_out=w_o,
                 gates=(gdn.reshape(T, GDN_W), flat(rg)))
    return h.reshape(B, S, D)
```

```python
import functools
import math

import numpy as np
import jax
import jax.numpy as jnp
from jax import lax
from jax.experimental import pallas as pl
from jax.experimental.pallas import tpu as pltpu

F32 = jnp.float32
BF16 = jnp.bfloat16
HIGHEST = lax.Precision.HIGHEST

D_MODEL = 1024
N_HEADS = 4
HEAD_DIM = 64
GROUP_W = N_HEADS * HEAD_DIM
DIFF_DIM = HEAD_DIM // 2
D_FF = 2816
N_MOD = 9
EPS = 1e-6
LNX_EPS = 64e-5
ROPE_THETA = 10000.0
LOG2E = 1.4426950408889634
CONV_K = 4
RWKV_START = 9 * GROUP_W + 2 * N_HEADS + GROUP_W

ATTN_W = 6 * GROUP_W
QK_W = 4 * GROUP_W
KV_TILE = 256
GDN_W = 3 * GROUP_W + GROUP_W + 128
RWKV_W = 3 * GROUP_W + 128 + 128 + 256 + 128
PROJ_W = ATTN_W + GDN_W + RWKV_W
VMEM_LIMIT = 56 * 1024 * 1024


def _cparams(sem):
    return pltpu.CompilerParams(dimension_semantics=sem, vmem_limit_bytes=VMEM_LIMIT)


def _sigmoid(x):
    return 1.0 / (1.0 + jnp.exp(-x))


def _softplus(x):
    return jnp.maximum(x, 0.0) + jnp.log(1.0 + jnp.exp(-jnp.abs(x)))


def _group_ones(n, group):
    r = lax.broadcasted_iota(jnp.int32, (n, n), 0) // group
    c = lax.broadcasted_iota(jnp.int32, (n, n), 1) // group
    return (r == c).astype(BF16)


def _group_sum(x, group):
    ones = _group_ones(x.shape[-1], group)
    hi = x.astype(BF16)
    lo = (x - hi.astype(F32)).astype(BF16)
    return (jnp.dot(hi, ones, preferred_element_type=F32)
            + jnp.dot(lo, ones, preferred_element_type=F32))


def _modulate(x, gain, shift, scale):
    ms = jnp.mean(x * x, axis=-1, keepdims=True)
    return (x * lax.rsqrt(ms + EPS) * gain) * (1.0 + scale) + shift


def _mod_kernel(c_ref, w_ref, b_ref, o_ref):
    c = c_ref[...]
    cond = c * _sigmoid(c)
    o_ref[...] = jnp.dot(cond, w_ref[...], precision=HIGHEST,
                         preferred_element_type=F32) + b_ref[...]


def _mod_all(c, ada_w, ada_b):
    L, D, N = ada_w.shape
    B = c.shape[0]
    tn = 1024
    out = pl.pallas_call(
        _mod_kernel,
        name="adaln_mod",
        grid=(L, N // tn),
        in_specs=[
            pl.BlockSpec((B, D), lambda l, n: (0, 0)),
            pl.BlockSpec((None, D, tn), lambda l, n: (l, 0, n)),
            pl.BlockSpec((None, 1, tn), lambda l, n: (l, 0, n)),
        ],
        out_specs=pl.BlockSpec((None, B, tn), lambda l, n: (l, 0, n)),
        out_shape=jax.ShapeDtypeStruct((L, B, N), F32),
        compiler_params=_cparams(("parallel", "parallel")),
    )(c, ada_w, ada_b.reshape(L, 1, N))
    return out.reshape(L, B, N_MOD, D)


def _ffn_kernel(*refs, mb, tf, mix):
    x_ref, mod_ref, g_ref, w13_ref, w2_ref = refs[:5]
    o_ref, act_ref = refs[-2:]
    x = x_ref[...]
    if mix:
        oa_ref, ob_ref, oc_ref, od_ref, w_ref, gc_ref, gd_ref = refs[5:12]
        gc = gc_ref[...]
        o_c = (oc_ref[...].astype(F32) * (gc * _sigmoid(gc))).astype(BF16)
        o_d = (od_ref[...].astype(F32) * gd_ref[...]).astype(BF16)
        proj = jnp.dot(oa_ref[...], w_ref[0:GROUP_W, :], preferred_element_type=F32)
        for g, o_g in ((1, ob_ref[...]), (2, o_c), (3, o_d)):
            proj += jnp.dot(o_g, w_ref[g * GROUP_W:(g + 1) * GROUP_W, :],
                            preferred_element_type=F32)
        x = x + mod_ref[5:6, :] * proj
    hn = _modulate(x, g_ref[...], mod_ref[mb:mb + 1, :], mod_ref[mb + 1:mb + 2, :]).astype(BF16)
    for f in range(D_FF // tf):
        gate = jnp.dot(hn, w13_ref[:, f * tf:(f + 1) * tf], preferred_element_type=F32)
        up = jnp.dot(hn, w13_ref[:, D_FF + f * tf:D_FF + (f + 1) * tf], preferred_element_type=F32)
        act_ref[:, f * tf:(f + 1) * tf] = (gate * _sigmoid(gate) * up).astype(BF16)
    acc = jnp.dot(act_ref[...], w2_ref[...], preferred_element_type=F32)
    o_ref[...] = x + 0.5 * mod_ref[mb + 2:mb + 3, :] * acc


def _ffn(h, mod_l, gain, w13, w2, l, j, mb, S, mixed=None, w_out=None, gates=None):
    T, D = h.shape
    tm = min(1024, S)
    tf = 256
    tpb = S // tm
    resident = dict(pipeline_mode=pl.Buffered(1))
    in_specs = [
        pl.BlockSpec((tm, D), lambda i: (i, 0)),
        pl.BlockSpec((None, N_MOD, D), lambda i: (i // tpb, 0, 0)),
        pl.BlockSpec((1, D), lambda i: (0, 0)),
        pl.BlockSpec((None, None, D, 2 * D_FF), lambda i: (l, j, 0, 0), **resident),
        pl.BlockSpec((None, None, D_FF, D), lambda i: (l, j, 0, 0), **resident),
    ]
    args = [h, mod_l, gain.reshape(1, D), w13, w2]
    if mixed is not None:
        in_specs += [pl.BlockSpec((tm, GROUP_W), lambda i: (i, 0))] * 4
        in_specs += [pl.BlockSpec((None, D, D), lambda i: (l, 0, 0), **resident)]
        in_specs += [pl.BlockSpec((tm, GROUP_W), lambda i: (i, 3)),
                     pl.BlockSpec((tm, GROUP_W), lambda i: (i, 0))]
        args += list(mixed) + [w_out] + list(gates)
    return pl.pallas_call(
        functools.partial(_ffn_kernel, mb=mb, tf=tf, mix=mixed is not None),
        name="ffn",
        grid=(T // tm,),
        in_specs=in_specs,
        out_specs=pl.BlockSpec((tm, D), lambda i: (i, 0)),
        out_shape=jax.ShapeDtypeStruct((T, D), F32),
        scratch_shapes=[pltpu.VMEM((tm, D_FF), BF16)],
        compiler_params=_cparams(("parallel",)),
    )(*args)


def _inproj_kernel(x_ref, mod_ref, g_ref, w_ref, qkg_ref, cos_ref, sin_ref,
                   attn_ref, vtd_ref, vts_ref, gdn_ref, rwkv_ref):
    hn = _modulate(x_ref[...], g_ref[...], mod_ref[3:4, :], mod_ref[4:5, :]).astype(BF16)

    pa = jnp.dot(hn, w_ref[:, 0:3 * GROUP_W], preferred_element_type=F32)
    cos = cos_ref[...]
    sin = sin_ref[...]
    for idx, scale in ((0, DIFF_DIM ** -0.5 * LOG2E), (1, 1.0)):
        x1 = pa[:, idx * 256:idx * 256 + 128]
        x2 = pa[:, idx * 256 + 128:idx * 256 + 256]
        ms = _group_sum(x1 * x1 + x2 * x2, 16) * (1.0 / DIFF_DIM)
        inv = lax.rsqrt(ms + EPS)
        n1 = x1 * inv * qkg_ref[2 * idx:2 * idx + 1, :]
        n2 = x2 * inv * qkg_ref[2 * idx + 1:2 * idx + 2, :]
        attn_ref[:, idx * 256:idx * 256 + 128] = ((n1 * cos - n2 * sin) * scale).astype(BF16)
        attn_ref[:, idx * 256 + 128:idx * 256 + 256] = ((n2 * cos + n1 * sin) * scale).astype(BF16)
    n_kv = x_ref.shape[0] // KV_TILE
    for c in range(n_kv):
        vtd_ref[c] = pa[c * KV_TILE:(c + 1) * KV_TILE, 512:768].T.astype(BF16)

    pb = jnp.dot(hn, w_ref[:, 3 * GROUP_W:ATTN_W], preferred_element_type=F32)
    attn_ref[:, 512:768] = (pb[:, 0:256] * (HEAD_DIM ** -0.5)).astype(BF16)
    attn_ref[:, 768:1024] = pb[:, 256:512].astype(BF16)
    for c in range(n_kv):
        vts_ref[c] = pb[c * KV_TILE:(c + 1) * KV_TILE, 512:768].T.astype(BF16)

    gdn_ref[...] = jnp.dot(hn, w_ref[:, ATTN_W:ATTN_W + GDN_W], preferred_element_type=F32)
    rwkv_ref[...] = jnp.dot(hn, w_ref[:, ATTN_W + GDN_W:PROJ_W], preferred_element_type=F32)


def _inproj(h, mod_l, gain, w_perm, l, qkg, cos128, sin128, S):
    T, D = h.shape
    tm = min(512, S)
    tpb = S // tm
    vt_spec = pl.BlockSpec((tm // KV_TILE, GROUP_W, KV_TILE), lambda i: (i, 0, 0))
    vt_shape = jax.ShapeDtypeStruct((T // KV_TILE, GROUP_W, KV_TILE), BF16)
    return pl.pallas_call(
        _inproj_kernel,
        name="inproj",
        grid=(T // tm,),
        in_specs=[
            pl.BlockSpec((tm, D), lambda i: (i, 0)),
            pl.BlockSpec((None, N_MOD, D), lambda i: (i // tpb, 0, 0)),
            pl.BlockSpec((1, D), lambda i: (0, 0)),
            pl.BlockSpec((None, D, PROJ_W), lambda i: (l, 0, 0)),
            pl.BlockSpec((4, 128), lambda i: (0, 0)),
            pl.BlockSpec((tm, 128), lambda i: (i % tpb, 0)),
            pl.BlockSpec((tm, 128), lambda i: (i % tpb, 0)),
        ],
        out_specs=[
            pl.BlockSpec((tm, QK_W), lambda i: (i, 0)),
            vt_spec, vt_spec,
            pl.BlockSpec((tm, GDN_W), lambda i: (i, 0)),
            pl.BlockSpec((tm, RWKV_W), lambda i: (i, 0)),
        ],
        out_shape=[
            jax.ShapeDtypeStruct((T, QK_W), BF16),
            vt_shape, vt_shape,
            jax.ShapeDtypeStruct((T, GDN_W), F32),
            jax.ShapeDtypeStruct((T, RWKV_W), F32),
        ],
        compiler_params=_cparams(("parallel",)),
    )(h, mod_l, gain.reshape(1, D), w_perm, qkg, cos128, sin128)


def _diff_kernel(q_ref, k_ref, vt_ref, lamv_ref, g_ref, o_ref, qc_ref, acc_ref,
                 *, lam_init, tq, tk):
    i = pl.program_id(1)
    q = q_ref[...]
    lane = lax.broadcasted_iota(jnp.int32, (1, GROUP_W), 1)
    sub = (lane % 128) // 16
    for h in range(N_HEADS):
        qc_ref[h, 0:tq, :] = jnp.where(sub == 2 * h, q, jnp.zeros_like(q))
        qc_ref[h, tq:2 * tq, :] = jnp.where(sub == 2 * h + 1, q, jnp.zeros_like(q))
    acc_ref[...] = jnp.zeros_like(acc_ref)

    def tile(kt, carry, masked):
        start = pl.multiple_of(kt * tk, tk)
        k = k_ref[pl.ds(start, tk), :]
        heads = range(N_HEADS)
        ss = [lax.dot_general(k, qc_ref[h], (((1,), (1,)), ((), ())), preferred_element_type=F32)
              for h in heads]
        out, ps, alphas = [], [], []
        for h in heads:
            m, l = carry[h]
            s = ss[h]
            if masked:
                key = kt * tk + lax.broadcasted_iota(jnp.int32, (tk, 2 * tq), 0)
                qpos = i * tq + lax.broadcasted_iota(jnp.int32, (tk, 2 * tq), 1) % tq
                s = jnp.where(key <= qpos, s, -jnp.inf)
            m_new = jnp.maximum(m, jnp.max(s, axis=0, keepdims=True))
            alpha = jnp.exp2(m - m_new)
            p = jnp.exp2(s - m_new)
            out.append((m_new, alpha * l + jnp.sum(p, axis=0, keepdims=True)))
            ps.append(p.astype(BF16))
            alphas.append(alpha)
        for h in heads:
            vt = vt_ref[kt, h * HEAD_DIM:(h + 1) * HEAD_DIM, :]
            acc_ref[h] = alphas[h] * acc_ref[h] + jnp.dot(vt, ps[h], preferred_element_type=F32)
        return tuple(out)

    init = tuple((jnp.full((1, 2 * tq), -jnp.inf, F32), jnp.zeros((1, 2 * tq), F32))
                 for _ in range(N_HEADS))
    kd = (i * tq) // tk
    carry = lax.fori_loop(0, kd, lambda kt, c: tile(kt, c, False), init)
    carry = tile(kd, carry, True)

    lv = lamv_ref[...]
    lam = (jnp.exp(jnp.sum(lv[0:1, :] * lv[1:2, :], axis=1, keepdims=True))
           - jnp.exp(jnp.sum(lv[2:3, :] * lv[3:4, :], axis=1, keepdims=True)) + lam_init)
    rows = []
    for h in range(N_HEADS):
        o = acc_ref[h] / carry[h][1]
        oh = o[:, :tq] - lam * o[:, tq:]
        ms = jnp.mean(oh * oh, axis=0, keepdims=True)
        rows.append(oh * lax.rsqrt(ms + EPS) * g_ref[h * HEAD_DIM:(h + 1) * HEAD_DIM, :]
                    * (1.0 - lam_init))
    o_ref[...] = jnp.concatenate(rows, axis=0).T.astype(BF16)


def _diff_attention(attn, vt, lamv, subln_g, lam_init):
    B, S, _ = attn.shape
    tq = min(256, S)
    tk = KV_TILE
    gain = jnp.broadcast_to(jnp.tile(subln_g, N_HEADS)[:, None], (GROUP_W, tq))
    return pl.pallas_call(
        functools.partial(_diff_kernel, lam_init=lam_init, tq=tq, tk=tk),
        name="diff_attn",
        grid=(B, S // tq),
        in_specs=[
            pl.BlockSpec((None, tq, GROUP_W), lambda b, i: (b, i, 0)),
            pl.BlockSpec((None, S, GROUP_W), lambda b, i: (b, 0, 1)),
            pl.BlockSpec((None, S // tk, GROUP_W, tk), lambda b, i: (b, 0, 0, 0)),
            pl.BlockSpec((4, DIFF_DIM), lambda b, i: (0, 0)),
            pl.BlockSpec((GROUP_W, tq), lambda b, i: (0, 0)),
        ],
        out_specs=pl.BlockSpec((None, tq, GROUP_W), lambda b, i: (b, i, 0)),
        out_shape=jax.ShapeDtypeStruct((B, S, GROUP_W), BF16),
        scratch_shapes=[pltpu.VMEM((N_HEADS, 2 * tq, GROUP_W), BF16),
                        pltpu.VMEM((N_HEADS, HEAD_DIM, 2 * tq), F32)],
        compiler_params=_cparams(("parallel", "parallel")),
    )(attn, attn, vt, lamv, gain)


def _sb_kernel(q_ref, k_ref, vt_ref, o_ref, qh_ref, tri_ref, acc_ref, *, t):
    i = pl.program_id(1)
    q = q_ref[...]
    lane = lax.broadcasted_iota(jnp.int32, (1, GROUP_W), 1)
    r_i = lax.broadcasted_iota(jnp.int32, (t, t), 0)
    c_i = lax.broadcasted_iota(jnp.int32, (t, t), 1)
    tri = (c_i > r_i).astype(BF16)
    tri_ref[...] = tri
    for h in range(N_HEADS):
        qh_ref[h] = jnp.where(lane // HEAD_DIM == h, q, jnp.zeros_like(q))
    acc_ref[...] = jnp.zeros_like(acc_ref)

    def tile(kt, carry, diag):
        start = pl.multiple_of(kt * t, t)
        k = k_ref[pl.ds(start, t), :]
        heads = range(N_HEADS)
        strict = r_i < c_i
        zs = [lax.dot_general(k, qh_ref[h], (((1,), (1,)), ((), ())), preferred_element_type=F32)
              for h in heads]
        lss, lks, sums = [], [], []
        for h in heads:
            z = zs[h]
            neg_abs = lax.bitcast_convert_type(
                lax.bitcast_convert_type(z, jnp.uint32) | jnp.uint32(0x80000000), F32)
            ls = jnp.minimum(z, 0.0) - jnp.log(1.0 + jnp.exp(neg_abs))
            lk = ls - z
            if diag:
                lk = jnp.where(strict, lk, 0.0)
            lss.append(ls)
            lks.append(lk.astype(BF16))
            sums.append(jnp.sum(lk, axis=0, keepdims=True))
        bts = [jnp.dot(tri_ref[...], lks[h], preferred_element_type=F32) for h in heads]
        pas = []
        for h in heads:
            a = jnp.exp(lss[h] + (bts[h] + carry[h]))
            if diag:
                a = jnp.where(strict, a, 0.0)
            pas.append(a.astype(BF16))
        for h in heads:
            vt = vt_ref[kt, h * HEAD_DIM:(h + 1) * HEAD_DIM, :]
            acc_ref[h] += jnp.dot(vt, pas[h], preferred_element_type=F32)
        return tuple(carry[h] + sums[h] for h in heads)

    carry = tile(i, tuple(jnp.zeros((1, t), F32) for _ in range(N_HEADS)), True)
    lax.fori_loop(0, i, lambda n, c: tile(i - 1 - n, c, False), carry)
    o_ref[...] = jnp.concatenate([acc_ref[h] for h in range(N_HEADS)], axis=0).T.astype(BF16)


def _sb_attention(attn, vt):
    B, S, _ = attn.shape
    t = KV_TILE
    return pl.pallas_call(
        functools.partial(_sb_kernel, t=t),
        name="sb_attn",
        grid=(B, S // t),
        in_specs=[
            pl.BlockSpec((None, t, GROUP_W), lambda b, i: (b, i, 2)),
            pl.BlockSpec((None, S, GROUP_W), lambda b, i: (b, 0, 3)),
            pl.BlockSpec((None, S // t, GROUP_W, t), lambda b, i: (b, 0, 0, 0)),
        ],
        out_specs=pl.BlockSpec((None, t, GROUP_W), lambda b, i: (b, i, 0)),
        out_shape=jax.ShapeDtypeStruct((B, S, GROUP_W), BF16),
        scratch_shapes=[pltpu.VMEM((N_HEADS, t, GROUP_W), BF16),
                        pltpu.VMEM((t, t), BF16),
                        pltpu.VMEM((N_HEADS, HEAD_DIM, t), F32)],
        compiler_params=_cparams(("parallel", "parallel")),
    )(attn, attn, vt)


def _gdn_prep_kernel(x_ref, prev_ref, small_ref, cw_ref, alog_ref, dtb_ref,
                     q_ref, k_ref, v_ref, ab_ref, xs_ref, *, ts):
    i = pl.program_id(1)
    prev = prev_ref[...]
    xs_ref[0:8, :] = jnp.where(i == 0, jnp.zeros_like(prev), prev)
    xs_ref[8:8 + ts, :] = x_ref[...]
    conv = jnp.zeros((ts, 3 * GROUP_W), F32)
    for j in range(CONV_K):
        conv = conv + xs_ref[pl.ds(8 - (CONV_K - 1) + j, ts), :] * cw_ref[j:j + 1, :]
    y = conv * _sigmoid(conv)
    q = y[:, 0:GROUP_W]
    k = y[:, GROUP_W:2 * GROUP_W]
    q_ref[...] = (q * lax.rsqrt(_group_sum(q * q, HEAD_DIM) + EPS) * (HEAD_DIM ** -0.5)).astype(BF16)
    k_ref[...] = (k * lax.rsqrt(_group_sum(k * k, HEAD_DIM) + EPS)).astype(BF16)
    v_ref[...] = y[:, 2 * GROUP_W:3 * GROUP_W].astype(BF16)
    small = small_ref[...]
    beta = _sigmoid(small)
    alpha = jnp.exp(-jnp.exp(alog_ref[...]) * _softplus(small + dtb_ref[...]))
    lane = lax.broadcasted_iota(jnp.int32, (1, 128), 1)
    ab_ref[...] = jnp.where(lane < N_HEADS, beta, alpha).T[0:2 * N_HEADS, :]


def _gdn_prep(gdn, conv_w, a_log, dt_bias):
    B, S, _ = gdn.shape
    ts = min(512, S)
    pad = jnp.zeros((N_HEADS,), F32)
    alog = jnp.concatenate([pad, a_log, jnp.zeros((120,), F32)]).reshape(1, 128)
    dtb = jnp.concatenate([pad, dt_bias, jnp.zeros((120,), F32)]).reshape(1, 128)
    tok = jax.ShapeDtypeStruct((B, S, GROUP_W), F32)
    return pl.pallas_call(
        functools.partial(_gdn_prep_kernel, ts=ts),
        name="gdn_prep",
        grid=(B, S // ts),
        in_specs=[
            pl.BlockSpec((None, ts, 3 * GROUP_W), lambda b, i: (b, i, 0)),
            pl.BlockSpec((None, 8, 3 * GROUP_W), lambda b, i: (b, jnp.maximum(i * (ts // 8) - 1, 0), 0)),
            pl.BlockSpec((None, ts, 128), lambda b, i: (b, i, 8)),
            pl.BlockSpec((CONV_K, 3 * GROUP_W), lambda b, i: (0, 0)),
            pl.BlockSpec((1, 128), lambda b, i: (0, 0)),
            pl.BlockSpec((1, 128), lambda b, i: (0, 0)),
        ],
        out_specs=[
            pl.BlockSpec((None, ts, GROUP_W), lambda b, i: (b, i, 0)),
            pl.BlockSpec((None, ts, GROUP_W), lambda b, i: (b, i, 0)),
            pl.BlockSpec((None, ts, GROUP_W), lambda b, i: (b, i, 0)),
            pl.BlockSpec((None, 2 * N_HEADS, ts), lambda b, i: (b, 0, i)),
        ],
        out_shape=[tok.update(dtype=BF16)] * 3 + [jax.ShapeDtypeStruct((B, 2 * N_HEADS, S), F32)],
        scratch_shapes=[pltpu.VMEM((ts + 8, 3 * GROUP_W), F32)],
        compiler_params=_cparams(("parallel", "parallel")),
    )(gdn, gdn, gdn, conv_w, alog, dtb)


def _gdn_scan_kernel(qb_ref, kb_ref, vb_ref, al_ref, be_ref, knb_ref, ng_ref, o_ref,
                     s_ref, ks_ref, q_ref, k_ref, v_ref, kn_ref, *, tt):
    @pl.when(pl.program_id(0) == 0)
    def _():
        s_ref[...] = jnp.zeros_like(s_ref)
        ks_ref[...] = jnp.zeros_like(ks_ref)

    q_ref[...] = qb_ref[...].astype(F32)
    k_ref[...] = kb_ref[...].astype(F32)
    v_ref[...] = vb_ref[...].astype(F32)
    kn_ref[...] = knb_ref[...].astype(F32)

    def step(t, ks, k_next):
        al = al_ref[pl.ds(t, 1), :]
        d = be_ref[pl.ds(t, 1), :] * (v_ref[t] - al * ks)
        o_parts = [None] * 2
        n_parts = [None] * 2
        for kk in range(HEAD_DIM):
            s_new = al * s_ref[kk] + k_ref[t, pl.ds(kk, 1), :] * d
            s_ref[kk] = s_new
            t_o = s_new * q_ref[t, pl.ds(kk, 1), :]
            t_n = s_new * k_next(kk)
            o_parts[kk % 2] = t_o if o_parts[kk % 2] is None else o_parts[kk % 2] + t_o
            n_parts[kk % 2] = t_n if n_parts[kk % 2] is None else n_parts[kk % 2] + t_n
        o = o_parts[0] + o_parts[1]
        ms = jnp.mean(o * o, axis=0, keepdims=True)
        o_ref[t] = o * lax.rsqrt(ms + EPS) * ng_ref[...]
        return n_parts[0] + n_parts[1]

    ks = lax.fori_loop(0, tt - 1,
                       lambda t, ks: step(t, ks, lambda kk: k_ref[t + 1, pl.ds(kk, 1), :]),
                       ks_ref[...])
    ks_ref[...] = step(tt - 1, ks, lambda kk: kn_ref[0, pl.ds(kk, 1), :])


def _next_row_spec(tt, S, N, Ln):
    return pl.BlockSpec((1, N, Ln), lambda i: (jnp.minimum((i + 1) * tt, S - 1), 0, 0))


def _gdn_scan(q, k, v, al, be, norm_g):
    S, N, Ln = q.shape
    tt = min(32, S)
    seq = pl.BlockSpec((tt, N, Ln), lambda i: (i, 0, 0))
    vec = pl.BlockSpec((tt, Ln), lambda i: (i, 0))
    return pl.pallas_call(
        functools.partial(_gdn_scan_kernel, tt=tt),
        name="gdn_scan",
        grid=(S // tt,),
        in_specs=[seq, seq, seq, vec, vec, _next_row_spec(tt, S, N, Ln),
                  pl.BlockSpec((N, Ln), lambda i: (0, 0))],
        out_specs=seq,
        out_shape=jax.ShapeDtypeStruct((S, N, Ln), F32),
        scratch_shapes=[pltpu.VMEM((N, N, Ln), F32), pltpu.VMEM((N, Ln), F32)]
        + [pltpu.VMEM((tt, N, Ln), F32)] * 3 + [pltpu.VMEM((1, N, Ln), F32)],
        compiler_params=_cparams(("arbitrary",)),
    )(q, k, v, al, be, k, jnp.broadcast_to(norm_g[:, None], (N, Ln)))


def _rwkv_prep_kernel(*refs, ts, first):
    if first:
        (x_ref, prev_ref, mu_ref, vecs_ref, wup_ref, aup_ref, gup_ref,
         r_ref, w_ref, k_ref, v_ref, a_ref, b_ref, g_ref, vf_ref, xs_ref) = refs
    else:
        (x_ref, prev_ref, mu_ref, vecs_ref, wup_ref, aup_ref, gup_ref, vup_ref, vfin_ref,
         r_ref, w_ref, k_ref, v_ref, a_ref, b_ref, g_ref, xs_ref) = refs
    i = pl.program_id(1)
    prev = prev_ref[...]
    xs_ref[0:8, :] = jnp.where(i == 0, jnp.zeros_like(prev), prev)
    x = x_ref[...]
    xs_ref[8:8 + ts, :] = x
    xm = x + (xs_ref[pl.ds(7, ts), :] - x) * mu_ref[...]
    r = xm[:, 0:256]
    k = xm[:, 256:512]
    v = xm[:, 512:768]
    wd = xm[:, 768:896]
    ad = xm[:, 896:1024]
    gd = xm[:, 1024:1280]
    w0, a0, k_k, k_a = vecs_ref[0:1, :], vecs_ref[1:2, :], vecs_ref[2:3, :], vecs_ref[3:4, :]
    dot = lambda x, w: jnp.dot(x.astype(BF16), w, preferred_element_type=F32)
    w_log = -_softplus(-(w0 + dot(jnp.tanh(wd), wup_ref[...]))) - 0.5
    a = _sigmoid(a0 + dot(ad, aup_ref[...]))
    g_ref[...] = dot(_sigmoid(gd), gup_ref[...])
    if first:
        vf_ref[...] = v
    else:
        vd = xm[:, 1280:1408]
        v = v + (vfin_ref[...] - v) * _sigmoid(vecs_ref[4:5, :] + dot(vd, vup_ref[...]))
    kk = k * k_k
    kk = kk * lax.rsqrt(_group_sum(kk * kk, HEAD_DIM) + EPS)
    r_ref[...] = r.astype(BF16)
    w_ref[...] = jnp.exp(-jnp.exp(w_log))
    k_ref[...] = (k * (1.0 + (a - 1.0) * k_a)).astype(BF16)
    v_ref[...] = v.astype(BF16)
    a_ref[...] = (-kk).astype(BF16)
    b_ref[...] = (kk * a).astype(BF16)


def _pad_rows(w, rows):
    return jnp.concatenate([w, jnp.zeros((rows - w.shape[0], w.shape[1]), w.dtype)],
                           axis=0).astype(BF16)


def _rwkv_prep(rw, mu, vecs, w_up, a_up, g_up, vmix_up, v_first):
    B, S, _ = rw.shape
    ts = min(512, S)
    first = v_first is None
    tok_spec = pl.BlockSpec((None, ts, GROUP_W), lambda b, i: (b, i, 0))
    const = lambda shape: pl.BlockSpec(shape, lambda b, i: (0,) * len(shape))
    in_specs = [
        pl.BlockSpec((None, ts, RWKV_W), lambda b, i: (b, i, 0)),
        pl.BlockSpec((None, 8, RWKV_W), lambda b, i: (b, jnp.maximum(i * (ts // 8) - 1, 0), 0)),
        const((1, RWKV_W)), const((8, GROUP_W)),
        const((128, GROUP_W)), const((128, GROUP_W)), const((256, GROUP_W)),
    ]
    args = [rw, rw, mu, vecs, _pad_rows(w_up, 128), _pad_rows(a_up, 128), _pad_rows(g_up, 256)]
    if not first:
        in_specs += [const((128, GROUP_W)), tok_spec]
        args += [_pad_rows(vmix_up, 128), v_first]
    tok = jax.ShapeDtypeStruct((B, S, GROUP_W), F32)
    n_out = 8 if first else 7
    return pl.pallas_call(
        functools.partial(_rwkv_prep_kernel, ts=ts, first=first),
        name="rwkv_prep",
        grid=(B, S // ts),
        in_specs=in_specs,
        out_specs=[tok_spec] * n_out,
        out_shape=[tok.update(dtype=BF16 if n in (0, 2, 3, 4, 5) else F32) for n in range(n_out)],
        scratch_shapes=[pltpu.VMEM((ts + 8, RWKV_W), F32)],
        compiler_params=_cparams(("parallel", "parallel")),
    )(*args)


def _wkv_scan_kernel(rb_ref, w_ref, kb_ref, vb_ref, ab_ref, bb_ref, anb_ref, tab_ref, y_ref,
                     s_ref, sa_ref, r_ref, k_ref, v_ref, a_ref, b_ref, an_ref, *, tt):
    @pl.when(pl.program_id(0) == 0)
    def _():
        s_ref[...] = jnp.zeros_like(s_ref)
        sa_ref[...] = jnp.zeros_like(sa_ref)

    for dst, src in ((r_ref, rb_ref), (k_ref, kb_ref), (v_ref, vb_ref), (a_ref, ab_ref),
                     (b_ref, bb_ref), (an_ref, anb_ref)):
        dst[...] = src[...].astype(F32)

    def step(t, sa, a_next):
        vt = v_ref[t]
        y_parts = [None] * 2
        n_parts = [None] * 2
        for j in range(HEAD_DIM):
            s_new = (s_ref[j] * w_ref[t, pl.ds(j, 1), :]
                     + (sa * b_ref[t, pl.ds(j, 1), :] + vt * k_ref[t, pl.ds(j, 1), :]))
            s_ref[j] = s_new
            t_y = s_new * r_ref[t, pl.ds(j, 1), :]
            t_n = s_new * a_next(j)
            y_parts[j % 2] = t_y if y_parts[j % 2] is None else y_parts[j % 2] + t_y
            n_parts[j % 2] = t_n if n_parts[j % 2] is None else n_parts[j % 2] + t_n
        y = y_parts[0] + y_parts[1]
        yc = y - jnp.mean(y, axis=0, keepdims=True)
        var = jnp.mean(yc * yc, axis=0, keepdims=True)
        yn = yc * lax.rsqrt(var + LNX_EPS) * tab_ref[0] + tab_ref[1]
        bonus = jnp.sum(r_ref[t] * k_ref[t] * tab_ref[2], axis=0, keepdims=True)
        y_ref[t] = yn + bonus * vt
        return n_parts[0] + n_parts[1]

    sa = lax.fori_loop(0, tt - 1,
                       lambda t, sa: step(t, sa, lambda j: a_ref[t + 1, pl.ds(j, 1), :]),
                       sa_ref[...])
    sa_ref[...] = step(tt - 1, sa, lambda j: an_ref[0, pl.ds(j, 1), :])


def _wkv_scan(r, w, k, v, a, b, tables):
    S, N, Ln = r.shape
    tt = min(32, S)
    seq = pl.BlockSpec((tt, N, Ln), lambda i: (i, 0, 0))
    return pl.pallas_call(
        functools.partial(_wkv_scan_kernel, tt=tt),
        name="wkv_scan",
        grid=(S // tt,),
        in_specs=[seq] * 6 + [_next_row_spec(tt, S, N, Ln),
                              pl.BlockSpec((3, N, Ln), lambda i: (0, 0, 0))],
        out_specs=seq,
        out_shape=jax.ShapeDtypeStruct((S, N, Ln), F32),
        scratch_shapes=[pltpu.VMEM((N, N, Ln), F32), pltpu.VMEM((N, Ln), F32)]
        + [pltpu.VMEM((tt, N, Ln), F32)] * 5 + [pltpu.VMEM((1, N, Ln), F32)],
        compiler_params=_cparams(("arbitrary",)),
    )(r, w, k, v, a, b, a, tables)


def _pad_cols(w, width):
    pad = width - w.shape[-1]
    return jnp.concatenate([w, jnp.zeros(w.shape[:-1] + (pad,), w.dtype)], axis=-1)


def _rope_perm():
    p = np.arange(128)
    first = (p // 16) * DIFF_DIM + p % 16
    return np.concatenate([first, first + 16])


def _permute_w_in(w_in, vmix_down):
    L, D, _ = w_in.shape
    perm = _rope_perm()
    g0 = 9 * GROUP_W
    r0 = RWKV_START
    vd = jnp.concatenate([jnp.zeros((1, D, vmix_down.shape[-1]), w_in.dtype), vmix_down], axis=0)
    cols = [
        w_in[..., perm], w_in[..., GROUP_W + perm], w_in[..., 2 * GROUP_W:3 * GROUP_W],
        w_in[..., 3 * GROUP_W:6 * GROUP_W],
        w_in[..., 6 * GROUP_W:g0], w_in[..., g0 + 8:g0 + 8 + GROUP_W], _pad_cols(w_in[..., g0:g0 + 8], 128),
        w_in[..., r0:r0 + 768], _pad_cols(w_in[..., r0 + 768:r0 + 832], 128),
        _pad_cols(w_in[..., r0 + 832:r0 + 896], 128), _pad_cols(w_in[..., r0 + 896:r0 + 1056], 256),
        _pad_cols(vd, 128),
    ]
    return jnp.concatenate(cols, axis=-1).astype(BF16)


def _permute_mu(mu, vmix_mu):
    L = mu.shape[0]
    vm = jnp.concatenate([jnp.zeros((1, vmix_mu.shape[-1]), mu.dtype), vmix_mu], axis=0)
    cols = [mu[:, 0:768], _pad_cols(mu[:, 768:832], 128), _pad_cols(mu[:, 832:896], 128),
            _pad_cols(mu[:, 896:1056], 256), _pad_cols(vm, 128)]
    return jnp.concatenate(cols, axis=-1).reshape(L, 1, RWKV_W)


def _to_lanes(x):
    B, S, _ = x.shape
    return x.reshape(B, S, N_HEADS, HEAD_DIM).transpose(1, 3, 0, 2).reshape(S, HEAD_DIM, B * N_HEADS)


def _from_lanes(y, B):
    S = y.shape[0]
    return y.reshape(S, HEAD_DIM, B, N_HEADS).transpose(2, 0, 3, 1).reshape(B, S, GROUP_W)


def _per_head_table(x, B):
    return jnp.tile(x.T, (1, B))


def _head_scalars_to_lanes(x):
    B, _, S = x.shape
    return x.transpose(2, 0, 1).reshape(S, B * N_HEADS)


def kernel(x, c, ada_w, ada_b, norm_g, ffn_w13, ffn_w2, w_in, w_out, diff_qk_g, diff_lambda, diff_subln_g, gdn_conv_w, gdn_a_log, gdn_dt_bias, gdn_norm_g, rwkv_mu, rwkv_w0, rwkv_w_up, rwkv_a0, rwkv_a_up, rwkv_g_up, rwkv_k_k, rwkv_k_a, rwkv_r_k, rwkv_lnx_w, rwkv_lnx_b, rwkv_vmix_down, rwkv_vmix_mu, rwkv_v0, rwkv_vmix_up):
    B, S, D = x.shape
    L = ada_w.shape[0]
    T = B * S

    mod = _mod_all(c, ada_w, ada_b)
    w13 = ffn_w13.astype(BF16)
    w2 = ffn_w2.astype(BF16)
    w_perm = _permute_w_in(w_in, rwkv_vmix_down)
    w_o = w_out.astype(BF16)
    mu = _permute_mu(rwkv_mu, rwkv_vmix_mu)

    inv = ROPE_THETA ** (-jnp.arange(0, DIFF_DIM, 2, dtype=F32) / DIFF_DIM)
    ang = jnp.arange(S, dtype=F32)[:, None] * inv[None, :]
    cos128 = jnp.tile(jnp.cos(ang), (1, 8))
    sin128 = jnp.tile(jnp.sin(ang), (1, 8))

    h = x.reshape(T, D)
    v_first = None
    for l in range(L):
        lam_init = 0.8 - 0.6 * math.exp(-0.3 * l)
        h = _ffn(h, mod[l], norm_g[l, 0], w13, w2, l, 0, 0, S)

        qkg = jnp.stack([jnp.tile(diff_qk_g[l, 0, :16], 8), jnp.tile(diff_qk_g[l, 0, 16:], 8),
                         jnp.tile(diff_qk_g[l, 1, :16], 8), jnp.tile(diff_qk_g[l, 1, 16:], 8)])
        attn, vt_d, vt_s, gdn, rw = _inproj(h, mod[l], norm_g[l, 1], w_perm, l, qkg,
                                            cos128, sin128, S)
        attn = attn.reshape(B, S, QK_W)
        kv_tiles = lambda t: t.reshape(B, S // KV_TILE, GROUP_W, KV_TILE)
        gdn = gdn.reshape(B, S, GDN_W)
        rw = rw.reshape(B, S, RWKV_W)

        o_a = _diff_attention(attn, kv_tiles(vt_d), diff_lambda[l], diff_subln_g[l], lam_init)
        o_b = _sb_attention(attn, kv_tiles(vt_s))

        gq, gk, gv, gab = _gdn_prep(gdn, gdn_conv_w[l], gdn_a_log[l], gdn_dt_bias[l])
        go = _gdn_scan(_to_lanes(gq), _to_lanes(gk), _to_lanes(gv),
                       _head_scalars_to_lanes(gab[:, N_HEADS:2 * N_HEADS]),
                       _head_scalars_to_lanes(gab[:, 0:N_HEADS]), gdn_norm_g[l])
        o_c = _from_lanes(go, B)

        zeros = jnp.zeros((GROUP_W,), F32)
        vecs = jnp.stack([rwkv_w0[l], rwkv_a0[l], rwkv_k_k[l], rwkv_k_a[l],
                          rwkv_v0[l - 1] if l > 0 else zeros, zeros, zeros, zeros])
        outs = _rwkv_prep(rw, mu[l], vecs, rwkv_w_up[l], rwkv_a_up[l], rwkv_g_up[l],
                          rwkv_vmix_up[l - 1] if l > 0 else None, v_first)
        rr, rw_, rk, rv, ra, rb, rg = outs[:7]
        if l == 0:
            v_first = outs[7]
        tables = jnp.stack([_per_head_table(t.reshape(N_HEADS, HEAD_DIM), B)
                            for t in (rwkv_lnx_w[l], rwkv_lnx_b[l], rwkv_r_k[l])])
        ry = _wkv_scan(*[_to_lanes(t) for t in (rr, rw_, rk, rv, ra, rb)], tables)
        o_d = _from_lanes(ry, B)

        flat = lambda t: t.reshape(T, GROUP_W)
        h = _ffn(h, mod[l], norm_g[l, 2], w13, w2, l, 1, 6, S,
                 mixed=(flat(o_a), flat(o_b), flat(o_c), flat(o_d)), w_out=w_o,
                 gates=(gdn.reshape(T, GDN_W), flat(rg)))
    return h.reshape(B, S, D)
```

```python
import functools
import math

import numpy as np
import jax
import jax.numpy as jnp
from jax import lax
from jax.experimental import pallas as pl
from jax.experimental.pallas import tpu as pltpu

F32 = jnp.float32
BF16 = jnp.bfloat16
HIGHEST = lax.Precision.HIGHEST

D_MODEL = 1024
N_HEADS = 4
HEAD_DIM = 64
GROUP_W = N_HEADS * HEAD_DIM
DIFF_DIM = HEAD_DIM // 2
D_FF = 2816
N_MOD = 9
EPS = 1e-6
LNX_EPS = 64e-5
ROPE_THETA = 10000.0
LOG2E = 1.4426950408889634
CONV_K = 4
RWKV_START = 9 * GROUP_W + 2 * N_HEADS + GROUP_W

ATTN_W = 6 * GROUP_W
QK_W = 4 * GROUP_W
KV_TILE = 256
GDN_W = 3 * GROUP_W + GROUP_W + 128
RWKV_W = 3 * GROUP_W + 128 + 128 + 256 + 128
PROJ_W = ATTN_W + GDN_W + RWKV_W
VMEM_LIMIT = 56 * 1024 * 1024


def _cparams(sem):
    return pltpu.CompilerParams(dimension_semantics=sem, vmem_limit_bytes=VMEM_LIMIT)


def _sigmoid(x):
    return 1.0 / (1.0 + jnp.exp(-x))


def _softplus(x):
    return jnp.maximum(x, 0.0) + jnp.log(1.0 + jnp.exp(-jnp.abs(x)))


def _group_ones(n, group):
    r = lax.broadcasted_iota(jnp.int32, (n, n), 0) // group
    c = lax.broadcasted_iota(jnp.int32, (n, n), 1) // group
    return (r == c).astype(BF16)


def _group_sum(x, group):
    ones = _group_ones(x.shape[-1], group)
    hi = x.astype(BF16)
    lo = (x - hi.astype(F32)).astype(BF16)
    return (jnp.dot(hi, ones, preferred_element_type=F32)
            + jnp.dot(lo, ones, preferred_element_type=F32))


def _modulate(x, gain, shift, scale):
    ms = jnp.mean(x * x, axis=-1, keepdims=True)
    return (x * lax.rsqrt(ms + EPS) * gain) * (1.0 + scale) + shift


def _mod_kernel(c_ref, w_ref, b_ref, o_ref):
    c = c_ref[...]
    cond = c * _sigmoid(c)
    o_ref[...] = jnp.dot(cond, w_ref[...], precision=HIGHEST,
                         preferred_element_type=F32) + b_ref[...]


def _mod_all(c, ada_w, ada_b):
    L, D, N = ada_w.shape
    B = c.shape[0]
    tn = 1024
    out = pl.pallas_call(
        _mod_kernel,
        name="adaln_mod",
        grid=(L, N // tn),
        in_specs=[
            pl.BlockSpec((B, D), lambda l, n: (0, 0)),
            pl.BlockSpec((None, D, tn), lambda l, n: (l, 0, n)),
            pl.BlockSpec((None, 1, tn), lambda l, n: (l, 0, n)),
        ],
        out_specs=pl.BlockSpec((None, B, tn), lambda l, n: (l, 0, n)),
        out_shape=jax.ShapeDtypeStruct((L, B, N), F32),
        compiler_params=_cparams(("parallel", "parallel")),
    )(c, ada_w, ada_b.reshape(L, 1, N))
    return out.reshape(L, B, N_MOD, D)


def _ffn_kernel(*refs, mb, tf, mix):
    x_ref, mod_ref, g_ref, w13_ref, w2_ref = refs[:5]
    o_ref, act_ref = refs[-2:]
    x = x_ref[...]
    if mix:
        oa_ref, ob_ref, oc_ref, od_ref, w_ref, gc_ref, gd_ref = refs[5:12]
        gc = gc_ref[...]
        o_c = (oc_ref[...].astype(F32) * (gc * _sigmoid(gc))).astype(BF16)
        o_d = (od_ref[...].astype(F32) * gd_ref[...]).astype(BF16)
        proj = jnp.dot(oa_ref[...], w_ref[0:GROUP_W, :], preferred_element_type=F32)
        for g, o_g in ((1, ob_ref[...]), (2, o_c), (3, o_d)):
            proj += jnp.dot(o_g, w_ref[g * GROUP_W:(g + 1) * GROUP_W, :],
                            preferred_element_type=F32)
        x = x + mod_ref[5:6, :] * proj
    hn = _modulate(x, g_ref[...], mod_ref[mb:mb + 1, :], mod_ref[mb + 1:mb + 2, :]).astype(BF16)
    for f in range(D_FF // tf):
        gate = jnp.dot(hn, w13_ref[:, f * tf:(f + 1) * tf], preferred_element_type=F32)
        up = jnp.dot(hn, w13_ref[:, D_FF + f * tf:D_FF + (f + 1) * tf], preferred_element_type=F32)
        act_ref[:, f * tf:(f + 1) * tf] = (gate * _sigmoid(gate) * up).astype(BF16)
    acc = jnp.dot(act_ref[...], w2_ref[...], preferred_element_type=F32)
    o_ref[...] = x + 0.5 * mod_ref[mb + 2:mb + 3, :] * acc


def _ffn(h, mod_l, gain, w13, w2, l, j, mb, S, mixed=None, w_out=None, gates=None):
    T, D = h.shape
    tm = min(1024, S)
    tf = 256
    tpb = S // tm
    resident = dict(pipeline_mode=pl.Buffered(1))
    in_specs = [
        pl.BlockSpec((tm, D), lambda i: (i, 0)),
        pl.BlockSpec((None, N_MOD, D), lambda i: (i // tpb, 0, 0)),
        pl.BlockSpec((1, D), lambda i: (0, 0)),
        pl.BlockSpec((None, None, D, 2 * D_FF), lambda i: (l, j, 0, 0), **resident),
        pl.BlockSpec((None, None, D_FF, D), lambda i: (l, j, 0, 0), **resident),
    ]
    args = [h, mod_l, gain.reshape(1, D), w13, w2]
    if mixed is not None:
        in_specs += [pl.BlockSpec((None, tm, GROUP_W), lambda i: (i // tpb, i % tpb, 0))] * 4
        in_specs += [pl.BlockSpec((None, D, D), lambda i: (l, 0, 0), **resident)]
        in_specs += [pl.BlockSpec((None, tm, GROUP_W), lambda i: (i // tpb, i % tpb, 3)),
                     pl.BlockSpec((None, tm, GROUP_W), lambda i: (i // tpb, i % tpb, 0))]
        args += list(mixed) + [w_out] + list(gates)
    return pl.pallas_call(
        functools.partial(_ffn_kernel, mb=mb, tf=tf, mix=mixed is not None),
        name="ffn",
        grid=(T // tm,),
        in_specs=in_specs,
        out_specs=pl.BlockSpec((tm, D), lambda i: (i, 0)),
        out_shape=jax.ShapeDtypeStruct((T, D), F32),
        scratch_shapes=[pltpu.VMEM((tm, D_FF), BF16)],
        compiler_params=_cparams(("parallel",)),
    )(*args)


def _inproj_kernel(x_ref, mod_ref, g_ref, w_ref, qkg_ref, cos_ref, sin_ref,
                   attn_ref, vtd_ref, vts_ref, gdn_ref, rwkv_ref):
    hn = _modulate(x_ref[...], g_ref[...], mod_ref[3:4, :], mod_ref[4:5, :]).astype(BF16)

    pa = jnp.dot(hn, w_ref[:, 0:3 * GROUP_W], preferred_element_type=F32)
    cos = cos_ref[...]
    sin = sin_ref[...]
    for idx, scale in ((0, DIFF_DIM ** -0.5 * LOG2E), (1, 1.0)):
        x1 = pa[:, idx * 256:idx * 256 + 128]
        x2 = pa[:, idx * 256 + 128:idx * 256 + 256]
        ms = _group_sum(x1 * x1 + x2 * x2, 16) * (1.0 / DIFF_DIM)
        inv = lax.rsqrt(ms + EPS)
        n1 = x1 * inv * qkg_ref[2 * idx:2 * idx + 1, :]
        n2 = x2 * inv * qkg_ref[2 * idx + 1:2 * idx + 2, :]
        attn_ref[:, idx * 256:idx * 256 + 128] = ((n1 * cos - n2 * sin) * scale).astype(BF16)
        attn_ref[:, idx * 256 + 128:idx * 256 + 256] = ((n2 * cos + n1 * sin) * scale).astype(BF16)
    n_kv = x_ref.shape[0] // KV_TILE
    for c in range(n_kv):
        vtd_ref[c] = pa[c * KV_TILE:(c + 1) * KV_TILE, 512:768].T.astype(BF16)

    pb = jnp.dot(hn, w_ref[:, 3 * GROUP_W:ATTN_W], preferred_element_type=F32)
    attn_ref[:, 512:768] = (pb[:, 0:256] * (HEAD_DIM ** -0.5)).astype(BF16)
    attn_ref[:, 768:1024] = pb[:, 256:512].astype(BF16)
    for c in range(n_kv):
        vts_ref[c] = pb[c * KV_TILE:(c + 1) * KV_TILE, 512:768].T.astype(BF16)

    gdn_ref[...] = jnp.dot(hn, w_ref[:, ATTN_W:ATTN_W + GDN_W], preferred_element_type=F32)
    rwkv_ref[...] = jnp.dot(hn, w_ref[:, ATTN_W + GDN_W:PROJ_W], preferred_element_type=F32)


def _inproj(h, mod_l, gain, w_perm, l, qkg, cos128, sin128, S):
    T, D = h.shape
    tm = min(512, S)
    tpb = S // tm
    vt_spec = pl.BlockSpec((tm // KV_TILE, GROUP_W, KV_TILE), lambda i: (i, 0, 0))
    vt_shape = jax.ShapeDtypeStruct((T // KV_TILE, GROUP_W, KV_TILE), BF16)
    return pl.pallas_call(
        _inproj_kernel,
        name="inproj",
        grid=(T // tm,),
        in_specs=[
            pl.BlockSpec((tm, D), lambda i: (i, 0)),
            pl.BlockSpec((None, N_MOD, D), lambda i: (i // tpb, 0, 0)),
            pl.BlockSpec((1, D), lambda i: (0, 0)),
            pl.BlockSpec((None, D, PROJ_W), lambda i: (l, 0, 0)),
            pl.BlockSpec((4, 128), lambda i: (0, 0)),
            pl.BlockSpec((tm, 128), lambda i: (i % tpb, 0)),
            pl.BlockSpec((tm, 128), lambda i: (i % tpb, 0)),
        ],
        out_specs=[
            pl.BlockSpec((tm, QK_W), lambda i: (i, 0)),
            vt_spec, vt_spec,
            pl.BlockSpec((tm, GDN_W), lambda i: (i, 0)),
            pl.BlockSpec((tm, RWKV_W), lambda i: (i, 0)),
        ],
        out_shape=[
            jax.ShapeDtypeStruct((T, QK_W), BF16),
            vt_shape, vt_shape,
            jax.ShapeDtypeStruct((T, GDN_W), F32),
            jax.ShapeDtypeStruct((T, RWKV_W), F32),
        ],
        compiler_params=_cparams(("parallel",)),
    )(h, mod_l, gain.reshape(1, D), w_perm, qkg, cos128, sin128)


def _diff_kernel(q_ref, k_ref, vt_ref, lamv_ref, g_ref, o_ref, qc_ref, acc_ref,
                 *, lam_init, tq, tk):
    i = pl.program_id(1)
    q = q_ref[...]
    lane = lax.broadcasted_iota(jnp.int32, (1, GROUP_W), 1)
    sub = (lane % 128) // 16
    for h in range(N_HEADS):
        qc_ref[h, 0:tq, :] = jnp.where(sub == 2 * h, q, jnp.zeros_like(q))
        qc_ref[h, tq:2 * tq, :] = jnp.where(sub == 2 * h + 1, q, jnp.zeros_like(q))
    acc_ref[...] = jnp.zeros_like(acc_ref)

    def tile(kt, carry, masked):
        start = pl.multiple_of(kt * tk, tk)
        k = k_ref[pl.ds(start, tk), :]
        heads = range(N_HEADS)
        ss = [lax.dot_general(k, qc_ref[h], (((1,), (1,)), ((), ())), preferred_element_type=F32)
              for h in heads]
        out, ps, alphas = [], [], []
        for h in heads:
            m, l = carry[h]
            s = ss[h]
            if masked:
                key = kt * tk + lax.broadcasted_iota(jnp.int32, (tk, 2 * tq), 0)
                qpos = i * tq + lax.broadcasted_iota(jnp.int32, (tk, 2 * tq), 1) % tq
                s = jnp.where(key <= qpos, s, -jnp.inf)
            m_new = jnp.maximum(m, jnp.max(s, axis=0, keepdims=True))
            alpha = jnp.exp2(m - m_new)
            p = jnp.exp2(s - m_new)
            out.append((m_new, alpha * l + jnp.sum(p, axis=0, keepdims=True)))
            ps.append(p.astype(BF16))
            alphas.append(alpha)
        for h in heads:
            vt = vt_ref[kt, h * HEAD_DIM:(h + 1) * HEAD_DIM, :]
            acc_ref[h] = alphas[h] * acc_ref[h] + jnp.dot(vt, ps[h], preferred_element_type=F32)
        return tuple(out)

    init = tuple((jnp.full((1, 2 * tq), -jnp.inf, F32), jnp.zeros((1, 2 * tq), F32))
                 for _ in range(N_HEADS))
    kd = (i * tq) // tk
    carry = lax.fori_loop(0, kd, lambda kt, c: tile(kt, c, False), init)
    carry = tile(kd, carry, True)

    lv = lamv_ref[...]
    lam = (jnp.exp(jnp.sum(lv[0:1, :] * lv[1:2, :], axis=1, keepdims=True))
           - jnp.exp(jnp.sum(lv[2:3, :] * lv[3:4, :], axis=1, keepdims=True)) + lam_init)
    rows = []
    for h in range(N_HEADS):
        o = acc_ref[h] / carry[h][1]
        oh = o[:, :tq] - lam * o[:, tq:]
        ms = jnp.mean(oh * oh, axis=0, keepdims=True)
        rows.append(oh * lax.rsqrt(ms + EPS) * g_ref[h * HEAD_DIM:(h + 1) * HEAD_DIM, :]
                    * (1.0 - lam_init))
    o_ref[...] = jnp.concatenate(rows, axis=0).T.astype(BF16)


def _diff_attention(attn, vt, lamv, subln_g, lam_init):
    B, S, _ = attn.shape
    tq = min(256, S)
    tk = KV_TILE
    gain = jnp.broadcast_to(jnp.tile(subln_g, N_HEADS)[:, None], (GROUP_W, tq))
    return pl.pallas_call(
        functools.partial(_diff_kernel, lam_init=lam_init, tq=tq, tk=tk),
        name="diff_attn",
        grid=(B, S // tq),
        in_specs=[
            pl.BlockSpec((None, tq, GROUP_W), lambda b, i: (b, i, 0)),
            pl.BlockSpec((None, S, GROUP_W), lambda b, i: (b, 0, 1)),
            pl.BlockSpec((None, S // tk, GROUP_W, tk), lambda b, i: (b, 0, 0, 0)),
            pl.BlockSpec((4, DIFF_DIM), lambda b, i: (0, 0)),
            pl.BlockSpec((GROUP_W, tq), lambda b, i: (0, 0)),
        ],
        out_specs=pl.BlockSpec((None, tq, GROUP_W), lambda b, i: (b, i, 0)),
        out_shape=jax.ShapeDtypeStruct((B, S, GROUP_W), BF16),
        scratch_shapes=[pltpu.VMEM((N_HEADS, 2 * tq, GROUP_W), BF16),
                        pltpu.VMEM((N_HEADS, HEAD_DIM, 2 * tq), F32)],
        compiler_params=_cparams(("parallel", "parallel")),
    )(attn, attn, vt, lamv, gain)


def _sb_kernel(q_ref, k_ref, vt_ref, o_ref, qh_ref, tri_ref, acc_ref, *, t):
    i = pl.program_id(1)
    q = q_ref[...]
    lane = lax.broadcasted_iota(jnp.int32, (1, GROUP_W), 1)
    r_i = lax.broadcasted_iota(jnp.int32, (t, t), 0)
    c_i = lax.broadcasted_iota(jnp.int32, (t, t), 1)
    tri = (c_i > r_i).astype(BF16)
    tri_ref[...] = tri
    for h in range(N_HEADS):
        qh_ref[h] = jnp.where(lane // HEAD_DIM == h, q, jnp.zeros_like(q))
    acc_ref[...] = jnp.zeros_like(acc_ref)

    def tile(kt, carry, diag):
        start = pl.multiple_of(kt * t, t)
        k = k_ref[pl.ds(start, t), :]
        heads = range(N_HEADS)
        strict = r_i < c_i
        zs = [lax.dot_general(k, qh_ref[h], (((1,), (1,)), ((), ())), preferred_element_type=F32)
              for h in heads]
        lss, lks, sums = [], [], []
        for h in heads:
            z = zs[h]
            neg_abs = lax.bitcast_convert_type(
                lax.bitcast_convert_type(z, jnp.uint32) | jnp.uint32(0x80000000), F32)
            ls = jnp.minimum(z, 0.0) - jnp.log(1.0 + jnp.exp(neg_abs))
            lk = ls - z
            if diag:
                lk = jnp.where(strict, lk, 0.0)
            lss.append(ls)
            lks.append(lk.astype(BF16))
            sums.append(jnp.sum(lk, axis=0, keepdims=True))
        bts = [jnp.dot(tri_ref[...], lks[h], preferred_element_type=F32) for h in heads]
        pas = []
        for h in heads:
            a = jnp.exp(lss[h] + (bts[h] + carry[h]))
            if diag:
                a = jnp.where(strict, a, 0.0)
            pas.append(a.astype(BF16))
        for h in heads:
            vt = vt_ref[kt, h * HEAD_DIM:(h + 1) * HEAD_DIM, :]
            acc_ref[h] += jnp.dot(vt, pas[h], preferred_element_type=F32)
        return tuple(carry[h] + sums[h] for h in heads)

    carry = tile(i, tuple(jnp.zeros((1, t), F32) for _ in range(N_HEADS)), True)
    lax.fori_loop(0, i, lambda n, c: tile(i - 1 - n, c, False), carry)
    o_ref[...] = jnp.concatenate([acc_ref[h] for h in range(N_HEADS)], axis=0).T.astype(BF16)


def _sb_attention(attn, vt):
    B, S, _ = attn.shape
    t = KV_TILE
    return pl.pallas_call(
        functools.partial(_sb_kernel, t=t),
        name="sb_attn",
        grid=(B, S // t),
        in_specs=[
            pl.BlockSpec((None, t, GROUP_W), lambda b, i: (b, i, 2)),
            pl.BlockSpec((None, S, GROUP_W), lambda b, i: (b, 0, 3)),
            pl.BlockSpec((None, S // t, GROUP_W, t), lambda b, i: (b, 0, 0, 0)),
        ],
        out_specs=pl.BlockSpec((None, t, GROUP_W), lambda b, i: (b, i, 0)),
        out_shape=jax.ShapeDtypeStruct((B, S, GROUP_W), BF16),
        scratch_shapes=[pltpu.VMEM((N_HEADS, t, GROUP_W), BF16),
                        pltpu.VMEM((t, t), BF16),
                        pltpu.VMEM((N_HEADS, HEAD_DIM, t), F32)],
        compiler_params=_cparams(("parallel", "parallel")),
    )(attn, attn, vt)


def _gdn_prep_kernel(x_ref, prev_ref, small_ref, cw_ref, alog_ref, dtb_ref,
                     q_ref, k_ref, v_ref, ab_ref, xs_ref, *, ts):
    i = pl.program_id(1)
    prev = prev_ref[...]
    xs_ref[0:8, :] = jnp.where(i == 0, jnp.zeros_like(prev), prev)
    xs_ref[8:8 + ts, :] = x_ref[...]
    conv = jnp.zeros((ts, 3 * GROUP_W), F32)
    for j in range(CONV_K):
        conv = conv + xs_ref[pl.ds(8 - (CONV_K - 1) + j, ts), :] * cw_ref[j:j + 1, :]
    y = conv * _sigmoid(conv)
    q = y[:, 0:GROUP_W]
    k = y[:, GROUP_W:2 * GROUP_W]
    q_ref[...] = (q * lax.rsqrt(_group_sum(q * q, HEAD_DIM) + EPS) * (HEAD_DIM ** -0.5)).astype(BF16)
    k_ref[...] = (k * lax.rsqrt(_group_sum(k * k, HEAD_DIM) + EPS)).astype(BF16)
    v_ref[...] = y[:, 2 * GROUP_W:3 * GROUP_W].astype(BF16)
    small = small_ref[...]
    beta = _sigmoid(small)
    alpha = jnp.exp(-jnp.exp(alog_ref[...]) * _softplus(small + dtb_ref[...]))
    lane = lax.broadcasted_iota(jnp.int32, (1, 128), 1)
    ab_ref[...] = jnp.where(lane < N_HEADS, beta, alpha).T[0:2 * N_HEADS, :]


def _gdn_prep(gdn, conv_w, a_log, dt_bias):
    B, S, _ = gdn.shape
    ts = min(512, S)
    pad = jnp.zeros((N_HEADS,), F32)
    alog = jnp.concatenate([pad, a_log, jnp.zeros((120,), F32)]).reshape(1, 128)
    dtb = jnp.concatenate([pad, dt_bias, jnp.zeros((120,), F32)]).reshape(1, 128)
    tok = jax.ShapeDtypeStruct((B, S, GROUP_W), F32)
    return pl.pallas_call(
        functools.partial(_gdn_prep_kernel, ts=ts),
        name="gdn_prep",
        grid=(B, S // ts),
        in_specs=[
            pl.BlockSpec((None, ts, 3 * GROUP_W), lambda b, i: (b, i, 0)),
            pl.BlockSpec((None, 8, 3 * GROUP_W), lambda b, i: (b, jnp.maximum(i * (ts // 8) - 1, 0), 0)),
            pl.BlockSpec((None, ts, 128), lambda b, i: (b, i, 8)),
            pl.BlockSpec((CONV_K, 3 * GROUP_W), lambda b, i: (0, 0)),
            pl.BlockSpec((1, 128), lambda b, i: (0, 0)),
            pl.BlockSpec((1, 128), lambda b, i: (0, 0)),
        ],
        out_specs=[
            pl.BlockSpec((None, ts, GROUP_W), lambda b, i: (b, i, 0)),
            pl.BlockSpec((None, ts, GROUP_W), lambda b, i: (b, i, 0)),
            pl.BlockSpec((None, ts, GROUP_W), lambda b, i: (b, i, 0)),
            pl.BlockSpec((None, 2 * N_HEADS, ts), lambda b, i: (b, 0, i)),
        ],
        out_shape=[tok.update(dtype=BF16)] * 3 + [jax.ShapeDtypeStruct((B, 2 * N_HEADS, S), F32)],
        scratch_shapes=[pltpu.VMEM((ts + 8, 3 * GROUP_W), F32)],
        compiler_params=_cparams(("parallel", "parallel")),
    )(gdn, gdn, gdn, conv_w, alog, dtb)


def _gdn_scan_kernel(qb_ref, kb_ref, vb_ref, al_ref, be_ref, knb_ref, ng_ref, o_ref,
                     s_ref, ks_ref, q_ref, k_ref, v_ref, kn_ref, *, tt):
    @pl.when(pl.program_id(0) == 0)
    def _():
        s_ref[...] = jnp.zeros_like(s_ref)
        ks_ref[...] = jnp.zeros_like(ks_ref)

    q_ref[...] = qb_ref[...].astype(F32)
    k_ref[...] = kb_ref[...].astype(F32)
    v_ref[...] = vb_ref[...].astype(F32)
    kn_ref[...] = knb_ref[...].astype(F32)

    def step(t, ks, k_next):
        al = al_ref[pl.ds(t, 1), :]
        d = be_ref[pl.ds(t, 1), :] * (v_ref[t] - al * ks)
        o_parts = [None] * 2
        n_parts = [None] * 2
        for kk in range(HEAD_DIM):
            s_new = al * s_ref[kk] + k_ref[t, pl.ds(kk, 1), :] * d
            s_ref[kk] = s_new
            t_o = s_new * q_ref[t, pl.ds(kk, 1), :]
            t_n = s_new * k_next(kk)
            o_parts[kk % 2] = t_o if o_parts[kk % 2] is None else o_parts[kk % 2] + t_o
            n_parts[kk % 2] = t_n if n_parts[kk % 2] is None else n_parts[kk % 2] + t_n
        o = o_parts[0] + o_parts[1]
        ms = jnp.mean(o * o, axis=0, keepdims=True)
        o_ref[t] = o * lax.rsqrt(ms + EPS) * ng_ref[...]
        return n_parts[0] + n_parts[1]

    ks = lax.fori_loop(0, tt - 1,
                       lambda t, ks: step(t, ks, lambda kk: k_ref[t + 1, pl.ds(kk, 1), :]),
                       ks_ref[...])
    ks_ref[...] = step(tt - 1, ks, lambda kk: kn_ref[0, pl.ds(kk, 1), :])


def _next_row_spec(tt, S, N, Ln):
    return pl.BlockSpec((1, N, Ln), lambda i: (jnp.minimum((i + 1) * tt, S - 1), 0, 0))


def _gdn_scan(q, k, v, al, be, norm_g):
    S, N, Ln = q.shape
    tt = min(32, S)
    seq = pl.BlockSpec((tt, N, Ln), lambda i: (i, 0, 0))
    vec = pl.BlockSpec((tt, Ln), lambda i: (i, 0))
    return pl.pallas_call(
        functools.partial(_gdn_scan_kernel, tt=tt),
        name="gdn_scan",
        grid=(S // tt,),
        in_specs=[seq, seq, seq, vec, vec, _next_row_spec(tt, S, N, Ln),
                  pl.BlockSpec((N, Ln), lambda i: (0, 0))],
        out_specs=seq,
        out_shape=jax.ShapeDtypeStruct((S, N, Ln), F32),
        scratch_shapes=[pltpu.VMEM((N, N, Ln), F32), pltpu.VMEM((N, Ln), F32)]
        + [pltpu.VMEM((tt, N, Ln), F32)] * 3 + [pltpu.VMEM((1, N, Ln), F32)],
        compiler_params=_cparams(("arbitrary",)),
    )(q, k, v, al, be, k, jnp.broadcast_to(norm_g[:, None], (N, Ln)))


def _rwkv_prep_kernel(*refs, ts, first):
    if first:
        (x_ref, prev_ref, mu_ref, vecs_ref, wup_ref, aup_ref, gup_ref,
         r_ref, w_ref, k_ref, v_ref, a_ref, b_ref, g_ref, vf_ref, xs_ref) = refs
    else:
        (x_ref, prev_ref, mu_ref, vecs_ref, wup_ref, aup_ref, gup_ref, vup_ref, vfin_ref,
         r_ref, w_ref, k_ref, v_ref, a_ref, b_ref, g_ref, xs_ref) = refs
    i = pl.program_id(1)
    prev = prev_ref[...]
    xs_ref[0:8, :] = jnp.where(i == 0, jnp.zeros_like(prev), prev)
    x = x_ref[...]
    xs_ref[8:8 + ts, :] = x
    xm = x + (xs_ref[pl.ds(7, ts), :] - x) * mu_ref[...]
    r = xm[:, 0:256]
    k = xm[:, 256:512]
    v = xm[:, 512:768]
    wd = xm[:, 768:896]
    ad = xm[:, 896:1024]
    gd = xm[:, 1024:1280]
    w0, a0, k_k, k_a = vecs_ref[0:1, :], vecs_ref[1:2, :], vecs_ref[2:3, :], vecs_ref[3:4, :]
    dot = lambda x, w: jnp.dot(x.astype(BF16), w, preferred_element_type=F32)
    w_log = -_softplus(-(w0 + dot(jnp.tanh(wd), wup_ref[...]))) - 0.5
    a = _sigmoid(a0 + dot(ad, aup_ref[...]))
    g_ref[...] = dot(_sigmoid(gd), gup_ref[...])
    if first:
        vf_ref[...] = v
    else:
        vd = xm[:, 1280:1408]
        v = v + (vfin_ref[...] - v) * _sigmoid(vecs_ref[4:5, :] + dot(vd, vup_ref[...]))
    kk = k * k_k
    kk = kk * lax.rsqrt(_group_sum(kk * kk, HEAD_DIM) + EPS)
    r_ref[...] = r.astype(BF16)
    w_ref[...] = jnp.exp(-jnp.exp(w_log))
    k_ref[...] = (k * (1.0 + (a - 1.0) * k_a)).astype(BF16)
    v_ref[...] = v.astype(BF16)
    a_ref[...] = (-kk).astype(BF16)
    b_ref[...] = (kk * a).astype(BF16)


def _pad_rows(w, rows):
    return jnp.concatenate([w, jnp.zeros((rows - w.shape[0], w.shape[1]), w.dtype)],
                           axis=0).astype(BF16)


def _rwkv_prep(rw, mu, vecs, w_up, a_up, g_up, vmix_up, v_first):
    B, S, _ = rw.shape
    ts = min(512, S)
    first = v_first is None
    tok_spec = pl.BlockSpec((None, ts, GROUP_W), lambda b, i: (b, i, 0))
    const = lambda shape: pl.BlockSpec(shape, lambda b, i: (0,) * len(shape))
    in_specs = [
        pl.BlockSpec((None, ts, RWKV_W), lambda b, i: (b, i, 0)),
        pl.BlockSpec((None, 8, RWKV_W), lambda b, i: (b, jnp.maximum(i * (ts // 8) - 1, 0), 0)),
        const((1, RWKV_W)), const((8, GROUP_W)),
        const((128, GROUP_W)), const((128, GROUP_W)), const((256, GROUP_W)),
    ]
    args = [rw, rw, mu, vecs, _pad_rows(w_up, 128), _pad_rows(a_up, 128), _pad_rows(g_up, 256)]
    if not first:
        in_specs += [const((128, GROUP_W)), tok_spec]
        args += [_pad_rows(vmix_up, 128), v_first]
    tok = jax.ShapeDtypeStruct((B, S, GROUP_W), F32)
    n_out = 8 if first else 7
    return pl.pallas_call(
        functools.partial(_rwkv_prep_kernel, ts=ts, first=first),
        name="rwkv_prep",
        grid=(B, S // ts),
        in_specs=in_specs,
        out_specs=[tok_spec] * n_out,
        out_shape=[tok.update(dtype=BF16 if n in (0, 2, 3, 4, 5) else F32) for n in range(n_out)],
        scratch_shapes=[pltpu.VMEM((ts + 8, RWKV_W), F32)],
        compiler_params=_cparams(("parallel", "parallel")),
    )(*args)


def _wkv_scan_kernel(rb_ref, w_ref, kb_ref, vb_ref, ab_ref, bb_ref, anb_ref, tab_ref, y_ref,
                     s_ref, sa_ref, r_ref, k_ref, v_ref, a_ref, b_ref, an_ref, *, tt):
    @pl.when(pl.program_id(0) == 0)
    def _():
        s_ref[...] = jnp.zeros_like(s_ref)
        sa_ref[...] = jnp.zeros_like(sa_ref)

    for dst, src in ((r_ref, rb_ref), (k_ref, kb_ref), (v_ref, vb_ref), (a_ref, ab_ref),
                     (b_ref, bb_ref), (an_ref, anb_ref)):
        dst[...] = src[...].astype(F32)

    def step(t, sa, a_next):
        vt = v_ref[t]
        y_parts = [None] * 2
        n_parts = [None] * 2
        for j in range(HEAD_DIM):
            s_new = (s_ref[j] * w_ref[t, pl.ds(j, 1), :]
                     + (sa * b_ref[t, pl.ds(j, 1), :] + vt * k_ref[t, pl.ds(j, 1), :]))
            s_ref[j] = s_new
            t_y = s_new * r_ref[t, pl.ds(j, 1), :]
            t_n = s_new * a_next(j)
            y_parts[j % 2] = t_y if y_parts[j % 2] is None else y_parts[j % 2] + t_y
            n_parts[j % 2] = t_n if n_parts[j % 2] is None else n_parts[j % 2] + t_n
        y = y_parts[0] + y_parts[1]
        yc = y - jnp.mean(y, axis=0, keepdims=True)
        var = jnp.mean(yc * yc, axis=0, keepdims=True)
        yn = yc * lax.rsqrt(var + LNX_EPS) * tab_ref[0] + tab_ref[1]
        bonus = jnp.sum(r_ref[t] * k_ref[t] * tab_ref[2], axis=0, keepdims=True)
        y_ref[t] = yn + bonus * vt
        return n_parts[0] + n_parts[1]

    sa = lax.fori_loop(0, tt - 1,
                       lambda t, sa: step(t, sa, lambda j: a_ref[t + 1, pl.ds(j, 1), :]),
                       sa_ref[...])
    sa_ref[...] = step(tt - 1, sa, lambda j: an_ref[0, pl.ds(j, 1), :])


def _wkv_scan(r, w, k, v, a, b, tables):
    S, N, Ln = r.shape
    tt = min(32, S)
    seq = pl.BlockSpec((tt, N, Ln), lambda i: (i, 0, 0))
    return pl.pallas_call(
        functools.partial(_wkv_scan_kernel, tt=tt),
        name="wkv_scan",
        grid=(S // tt,),
        in_specs=[seq] * 6 + [_next_row_spec(tt, S, N, Ln),
                              pl.BlockSpec((3, N, Ln), lambda i: (0, 0, 0))],
        out_specs=seq,
        out_shape=jax.ShapeDtypeStruct((S, N, Ln), F32),
        scratch_shapes=[pltpu.VMEM((N, N, Ln), F32), pltpu.VMEM((N, Ln), F32)]
        + [pltpu.VMEM((tt, N, Ln), F32)] * 5 + [pltpu.VMEM((1, N, Ln), F32)],
        compiler_params=_cparams(("arbitrary",)),
    )(r, w, k, v, a, b, a, tables)


def _pad_cols(w, width):
    pad = width - w.shape[-1]
    return jnp.concatenate([w, jnp.zeros(w.shape[:-1] + (pad,), w.dtype)], axis=-1)


def _rope_perm():
    p = np.arange(128)
    first = (p // 16) * DIFF_DIM + p % 16
    return np.concatenate([first, first + 16])


def _permute_w_in(w_in, vmix_down):
    L, D, _ = w_in.shape
    perm = _rope_perm()
    g0 = 9 * GROUP_W
    r0 = RWKV_START
    vd = jnp.concatenate([jnp.zeros((1, D, vmix_down.shape[-1]), w_in.dtype), vmix_down], axis=0)
    cols = [
        w_in[..., perm], w_in[..., GROUP_W + perm], w_in[..., 2 * GROUP_W:3 * GROUP_W],
        w_in[..., 3 * GROUP_W:6 * GROUP_W],
        w_in[..., 6 * GROUP_W:g0], w_in[..., g0 + 8:g0 + 8 + GROUP_W], _pad_cols(w_in[..., g0:g0 + 8], 128),
        w_in[..., r0:r0 + 768], _pad_cols(w_in[..., r0 + 768:r0 + 832], 128),
        _pad_cols(w_in[..., r0 + 832:r0 + 896], 128), _pad_cols(w_in[..., r0 + 896:r0 + 1056], 256),
        _pad_cols(vd, 128),
    ]
    return jnp.concatenate(cols, axis=-1).astype(BF16)


def _permute_mu(mu, vmix_mu):
    L = mu.shape[0]
    vm = jnp.concatenate([jnp.zeros((1, vmix_mu.shape[-1]), mu.dtype), vmix_mu], axis=0)
    cols = [mu[:, 0:768], _pad_cols(mu[:, 768:832], 128), _pad_cols(mu[:, 832:896], 128),
            _pad_cols(mu[:, 896:1056], 256), _pad_cols(vm, 128)]
    return jnp.concatenate(cols, axis=-1).reshape(L, 1, RWKV_W)


def _to_lanes(x):
    B, S, _ = x.shape
    return x.reshape(B, S, N_HEADS, HEAD_DIM).transpose(1, 3, 0, 2).reshape(S, HEAD_DIM, B * N_HEADS)


def _from_lanes(y, B):
    S = y.shape[0]
    return y.reshape(S, HEAD_DIM, B, N_HEADS).transpose(2, 0, 3, 1).reshape(B, S, GROUP_W)


def _per_head_table(x, B):
    return jnp.tile(x.T, (1, B))


def _head_scalars_to_lanes(x):
    B, _, S = x.shape
    return x.transpose(2, 0, 1).reshape(S, B * N_HEADS)


def kernel(x, c, ada_w, ada_b, norm_g, ffn_w13, ffn_w2, w_in, w_out, diff_qk_g, diff_lambda, diff_subln_g, gdn_conv_w, gdn_a_log, gdn_dt_bias, gdn_norm_g, rwkv_mu, rwkv_w0, rwkv_w_up, rwkv_a0, rwkv_a_up, rwkv_g_up, rwkv_k_k, rwkv_k_a, rwkv_r_k, rwkv_lnx_w, rwkv_lnx_b, rwkv_vmix_down, rwkv_vmix_mu, rwkv_v0, rwkv_vmix_up):
    B, S, D = x.shape
    L = ada_w.shape[0]
    T = B * S

    mod = _mod_all(c, ada_w, ada_b)
    w13 = ffn_w13.astype(BF16)
    w2 = ffn_w2.astype(BF16)
    w_perm = _permute_w_in(w_in, rwkv_vmix_down)
    w_o = w_out.astype(BF16)
    mu = _permute_mu(rwkv_mu, rwkv_vmix_mu)

    inv = ROPE_THETA ** (-jnp.arange(0, DIFF_DIM, 2, dtype=F32) / DIFF_DIM)
    ang = jnp.arange(S, dtype=F32)[:, None] * inv[None, :]
    cos128 = jnp.tile(jnp.cos(ang), (1, 8))
    sin128 = jnp.tile(jnp.sin(ang), (1, 8))

    h = x.reshape(T, D)
    v_first = None
    for l in range(L):
        lam_init = 0.8 - 0.6 * math.exp(-0.3 * l)
        h = _ffn(h, mod[l], norm_g[l, 0], w13, w2, l, 0, 0, S)

        qkg = jnp.stack([jnp.tile(diff_qk_g[l, 0, :16], 8), jnp.tile(diff_qk_g[l, 0, 16:], 8),
                         jnp.tile(diff_qk_g[l, 1, :16], 8), jnp.tile(diff_qk_g[l, 1, 16:], 8)])
        attn, vt_d, vt_s, gdn, rw = _inproj(h, mod[l], norm_g[l, 1], w_perm, l, qkg,
                                            cos128, sin128, S)
        attn = attn.reshape(B, S, QK_W)
        kv_tiles = lambda t: t.reshape(B, S // KV_TILE, GROUP_W, KV_TILE)
        gdn = gdn.reshape(B, S, GDN_W)
        rw = rw.reshape(B, S, RWKV_W)

        o_a = _diff_attention(attn, kv_tiles(vt_d), diff_lambda[l], diff_subln_g[l], lam_init)
        o_b = _sb_attention(attn, kv_tiles(vt_s))

        gq, gk, gv, gab = _gdn_prep(gdn, gdn_conv_w[l], gdn_a_log[l], gdn_dt_bias[l])
        go = _gdn_scan(_to_lanes(gq), _to_lanes(gk), _to_lanes(gv),
                       _head_scalars_to_lanes(gab[:, N_HEADS:2 * N_HEADS]),
                       _head_scalars_to_lanes(gab[:, 0:N_HEADS]), gdn_norm_g[l])
        o_c = _from_lanes(go, B)

        zeros = jnp.zeros((GROUP_W,), F32)
        vecs = jnp.stack([rwkv_w0[l], rwkv_a0[l], rwkv_k_k[l], rwkv_k_a[l],
                          rwkv_v0[l - 1] if l > 0 else zeros, zeros, zeros, zeros])
        outs = _rwkv_prep(rw, mu[l], vecs, rwkv_w_up[l], rwkv_a_up[l], rwkv_g_up[l],
                          rwkv_vmix_up[l - 1] if l > 0 else None, v_first)
        rr, rw_, rk, rv, ra, rb, rg = outs[:7]
        if l == 0:
            v_first = outs[7]
        tables = jnp.stack([_per_head_table(t.reshape(N_HEADS, HEAD_DIM), B)
                            for t in (rwkv_lnx_w[l], rwkv_lnx_b[l], rwkv_r_k[l])])
        ry = _wkv_scan(*[_to_lanes(t) for t in (rr, rw_, rk, rv, ra, rb)], tables)
        o_d = _from_lanes(ry, B)

        h = _ffn(h, mod[l], norm_g[l, 2], w13, w2, l, 1, 6, S,
                 mixed=(o_a, o_b, o_c, o_d), w_out=w_o, gates=(gdn, rg))
    return h.reshape(B, S, D)
```

```python
import functools
import math

import numpy as np
import jax
import jax.numpy as jnp
from jax import lax
from jax.experimental import pallas as pl
from jax.experimental.pallas import tpu as pltpu

F32 = jnp.float32
BF16 = jnp.bfloat16
HIGHEST = lax.Precision.HIGHEST

D_MODEL = 1024
N_HEADS = 4
HEAD_DIM = 64
GROUP_W = N_HEADS * HEAD_DIM
DIFF_DIM = HEAD_DIM // 2
D_FF = 2816
N_MOD = 9
EPS = 1e-6
LNX_EPS = 64e-5
ROPE_THETA = 10000.0
LOG2E = 1.4426950408889634
CONV_K = 4
RWKV_START = 9 * GROUP_W + 2 * N_HEADS + GROUP_W

ATTN_W = 6 * GROUP_W
QK_W = 4 * GROUP_W
KV_TILE = 256
GDN_W = 3 * GROUP_W + GROUP_W + 128
RWKV_W = 3 * GROUP_W + 128 + 128 + 256 + 128
PROJ_W = ATTN_W + GDN_W + RWKV_W
VMEM_LIMIT = 56 * 1024 * 1024


def _cparams(sem):
    return pltpu.CompilerParams(dimension_semantics=sem, vmem_limit_bytes=VMEM_LIMIT)


def _sigmoid(x):
    return 1.0 / (1.0 + jnp.exp(-x))


def _softplus(x):
    return jnp.maximum(x, 0.0) + jnp.log(1.0 + jnp.exp(-jnp.abs(x)))


def _group_ones(n, group):
    r = lax.broadcasted_iota(jnp.int32, (n, n), 0) // group
    c = lax.broadcasted_iota(jnp.int32, (n, n), 1) // group
    return (r == c).astype(BF16)


def _group_sum(x, group):
    ones = _group_ones(x.shape[-1], group)
    hi = x.astype(BF16)
    lo = (x - hi.astype(F32)).astype(BF16)
    return (jnp.dot(hi, ones, preferred_element_type=F32)
            + jnp.dot(lo, ones, preferred_element_type=F32))


def _modulate(x, gain, shift, scale):
    ms = jnp.mean(x * x, axis=-1, keepdims=True)
    return (x * lax.rsqrt(ms + EPS) * gain) * (1.0 + scale) + shift


def _mod_kernel(c_ref, w_ref, b_ref, o_ref):
    c = c_ref[...]
    cond = c * _sigmoid(c)
    o_ref[...] = jnp.dot(cond, w_ref[...], precision=HIGHEST,
                         preferred_element_type=F32) + b_ref[...]


def _mod_all(c, ada_w, ada_b):
    L, D, N = ada_w.shape
    B = c.shape[0]
    tn = 1024
    out = pl.pallas_call(
        _mod_kernel,
        name="adaln_mod",
        grid=(L, N // tn),
        in_specs=[
            pl.BlockSpec((B, D), lambda l, n: (0, 0)),
            pl.BlockSpec((None, D, tn), lambda l, n: (l, 0, n)),
            pl.BlockSpec((None, 1, tn), lambda l, n: (l, 0, n)),
        ],
        out_specs=pl.BlockSpec((None, B, tn), lambda l, n: (l, 0, n)),
        out_shape=jax.ShapeDtypeStruct((L, B, N), F32),
        compiler_params=_cparams(("parallel", "parallel")),
    )(c, ada_w, ada_b.reshape(L, 1, N))
    return out.reshape(L, B, N_MOD, D)


def _ffn_kernel(*refs, mb, tf, mix):
    x_ref, mod_ref, g_ref, w13_ref, w2_ref = refs[:5]
    o_ref, act_ref = refs[-2:]
    x = x_ref[...]
    if mix:
        oa_ref, ob_ref, oc_ref, od_ref, w_ref, gc_ref, gd_ref = refs[5:12]
        gc = gc_ref[...]
        o_c = (oc_ref[...].astype(F32) * (gc * _sigmoid(gc))).astype(BF16)
        o_d = (od_ref[...].astype(F32) * gd_ref[...]).astype(BF16)
        proj = jnp.dot(oa_ref[...], w_ref[0:GROUP_W, :], preferred_element_type=F32)
        for g, o_g in ((1, ob_ref[...]), (2, o_c), (3, o_d)):
            proj += jnp.dot(o_g, w_ref[g * GROUP_W:(g + 1) * GROUP_W, :],
                            preferred_element_type=F32)
        x = x + mod_ref[5:6, :] * proj
    hn = _modulate(x, g_ref[...], mod_ref[mb:mb + 1, :], mod_ref[mb + 1:mb + 2, :]).astype(BF16)
    for f in range(D_FF // tf):
        gate = jnp.dot(hn, w13_ref[:, f * tf:(f + 1) * tf], preferred_element_type=F32)
        up = jnp.dot(hn, w13_ref[:, D_FF + f * tf:D_FF + (f + 1) * tf], preferred_element_type=F32)
        act_ref[:, f * tf:(f + 1) * tf] = (gate * _sigmoid(gate) * up).astype(BF16)
    acc = jnp.dot(act_ref[...], w2_ref[...], preferred_element_type=F32)
    o_ref[...] = x + 0.5 * mod_ref[mb + 2:mb + 3, :] * acc


def _ffn(h, mod_l, gain, w13, w2, l, j, mb, S, mixed=None, w_out=None, gates=None):
    T, D = h.shape
    tm = min(1024, S)
    tf = 256
    tpb = S // tm
    resident = dict(pipeline_mode=pl.Buffered(1))
    in_specs = [
        pl.BlockSpec((tm, D), lambda i: (i, 0)),
        pl.BlockSpec((None, N_MOD, D), lambda i: (i // tpb, 0, 0)),
        pl.BlockSpec((1, D), lambda i: (0, 0)),
        pl.BlockSpec((None, None, D, 2 * D_FF), lambda i: (l, j, 0, 0), **resident),
        pl.BlockSpec((None, None, D_FF, D), lambda i: (l, j, 0, 0), **resident),
    ]
    args = [h, mod_l, gain.reshape(1, D), w13, w2]
    if mixed is not None:
        in_specs += [pl.BlockSpec((None, tm, GROUP_W), lambda i: (i // tpb, i % tpb, 0))] * 4
        in_specs += [pl.BlockSpec((None, D, D), lambda i: (l, 0, 0), **resident)]
        in_specs += [pl.BlockSpec((None, tm, GROUP_W), lambda i: (i // tpb, i % tpb, 3)),
                     pl.BlockSpec((None, tm, GROUP_W), lambda i: (i // tpb, i % tpb, 0))]
        args += list(mixed) + [w_out] + list(gates)
    return pl.pallas_call(
        functools.partial(_ffn_kernel, mb=mb, tf=tf, mix=mixed is not None),
        name="ffn",
        grid=(T // tm,),
        in_specs=in_specs,
        out_specs=pl.BlockSpec((tm, D), lambda i: (i, 0)),
        out_shape=jax.ShapeDtypeStruct((T, D), F32),
        scratch_shapes=[pltpu.VMEM((tm, D_FF), BF16)],
        compiler_params=_cparams(("parallel",)),
    )(*args)


def _inproj_kernel(x_ref, mod_ref, g_ref, w_ref, qkg_ref, cos_ref, sin_ref,
                   attn_ref, vtd_ref, vts_ref, gdn_ref, rwkv_ref):
    hn = _modulate(x_ref[...], g_ref[...], mod_ref[3:4, :], mod_ref[4:5, :]).astype(BF16)

    pa = jnp.dot(hn, w_ref[:, 0:3 * GROUP_W], preferred_element_type=F32)
    cos = cos_ref[...]
    sin = sin_ref[...]
    for idx, scale in ((0, DIFF_DIM ** -0.5 * LOG2E), (1, 1.0)):
        x1 = pa[:, idx * 256:idx * 256 + 128]
        x2 = pa[:, idx * 256 + 128:idx * 256 + 256]
        ms = _group_sum(x1 * x1 + x2 * x2, 16) * (1.0 / DIFF_DIM)
        inv = lax.rsqrt(ms + EPS)
        n1 = x1 * inv * qkg_ref[2 * idx:2 * idx + 1, :]
        n2 = x2 * inv * qkg_ref[2 * idx + 1:2 * idx + 2, :]
        attn_ref[:, idx * 256:idx * 256 + 128] = ((n1 * cos - n2 * sin) * scale).astype(BF16)
        attn_ref[:, idx * 256 + 128:idx * 256 + 256] = ((n2 * cos + n1 * sin) * scale).astype(BF16)
    n_kv = x_ref.shape[0] // KV_TILE
    for c in range(n_kv):
        vtd_ref[c] = pa[c * KV_TILE:(c + 1) * KV_TILE, 512:768].T.astype(BF16)

    pb = jnp.dot(hn, w_ref[:, 3 * GROUP_W:ATTN_W], preferred_element_type=F32)
    attn_ref[:, 512:768] = (pb[:, 0:256] * (HEAD_DIM ** -0.5)).astype(BF16)
    attn_ref[:, 768:1024] = pb[:, 256:512].astype(BF16)
    for c in range(n_kv):
        vts_ref[c] = pb[c * KV_TILE:(c + 1) * KV_TILE, 512:768].T.astype(BF16)

    gdn_ref[...] = jnp.dot(hn, w_ref[:, ATTN_W:ATTN_W + GDN_W], preferred_element_type=F32)
    rwkv_ref[...] = jnp.dot(hn, w_ref[:, ATTN_W + GDN_W:PROJ_W], preferred_element_type=F32)


def _inproj(h, mod_l, gain, w_perm, l, qkg, cos128, sin128, S):
    T, D = h.shape
    tm = min(512, S)
    tpb = S // tm
    vt_spec = pl.BlockSpec((tm // KV_TILE, GROUP_W, KV_TILE), lambda i: (i, 0, 0))
    vt_shape = jax.ShapeDtypeStruct((T // KV_TILE, GROUP_W, KV_TILE), BF16)
    return pl.pallas_call(
        _inproj_kernel,
        name="inproj",
        grid=(T // tm,),
        in_specs=[
            pl.BlockSpec((tm, D), lambda i: (i, 0)),
            pl.BlockSpec((None, N_MOD, D), lambda i: (i // tpb, 0, 0)),
            pl.BlockSpec((1, D), lambda i: (0, 0)),
            pl.BlockSpec((None, D, PROJ_W), lambda i: (l, 0, 0)),
            pl.BlockSpec((4, 128), lambda i: (0, 0)),
            pl.BlockSpec((tm, 128), lambda i: (i % tpb, 0)),
            pl.BlockSpec((tm, 128), lambda i: (i % tpb, 0)),
        ],
        out_specs=[
            pl.BlockSpec((tm, QK_W), lambda i: (i, 0)),
            vt_spec, vt_spec,
            pl.BlockSpec((tm, GDN_W), lambda i: (i, 0)),
            pl.BlockSpec((tm, RWKV_W), lambda i: (i, 0)),
        ],
        out_shape=[
            jax.ShapeDtypeStruct((T, QK_W), BF16),
            vt_shape, vt_shape,
            jax.ShapeDtypeStruct((T, GDN_W), F32),
            jax.ShapeDtypeStruct((T, RWKV_W), F32),
        ],
        compiler_params=_cparams(("parallel",)),
    )(h, mod_l, gain.reshape(1, D), w_perm, qkg, cos128, sin128)


def _attn_kernel(qd_ref, kd_ref, vtd_ref, qs_ref, ks_ref, vts_ref, lamv_ref, g_ref,
                 od_ref, os_ref, qc_ref, qh_ref, tri_ref, accd_ref, accs_ref, *, lam_init, t):
    i = pl.program_id(1)
    heads = range(N_HEADS)
    lane = lax.broadcasted_iota(jnp.int32, (1, GROUP_W), 1)
    r_i = lax.broadcasted_iota(jnp.int32, (t, t), 0)
    c_i = lax.broadcasted_iota(jnp.int32, (t, t), 1)
    tri_ref[...] = (c_i > r_i).astype(BF16)
    qd = qd_ref[...]
    qs = qs_ref[...]
    sub = (lane % 128) // 16
    for h in heads:
        qc_ref[h, 0:t, :] = jnp.where(sub == 2 * h, qd, jnp.zeros_like(qd))
        qc_ref[h, t:2 * t, :] = jnp.where(sub == 2 * h + 1, qd, jnp.zeros_like(qd))
        qh_ref[h] = jnp.where(lane // HEAD_DIM == h, qs, jnp.zeros_like(qs))
    accd_ref[...] = jnp.zeros_like(accd_ref)
    accs_ref[...] = jnp.zeros_like(accs_ref)

    def tile(kt, carry, diag):
        stats, keep = carry
        start = pl.multiple_of(kt * t, t)
        kd = kd_ref[pl.ds(start, t), :]
        ks = ks_ref[pl.ds(start, t), :]
        nt = (((1,), (1,)), ((), ()))
        ss = [lax.dot_general(kd, qc_ref[h], nt, preferred_element_type=F32) for h in heads]
        zs = [lax.dot_general(ks, qh_ref[h], nt, preferred_element_type=F32) for h in heads]
        new_stats, ps, alphas = [], [], []
        for h in heads:
            m, l = stats[h]
            s = ss[h]
            if diag:
                allowed = jnp.concatenate([r_i <= c_i, r_i <= c_i], axis=1)
                s = jnp.where(allowed, s, -jnp.inf)
            m_new = jnp.maximum(m, jnp.max(s, axis=0, keepdims=True))
            alpha = jnp.exp2(m - m_new)
            p = jnp.exp2(s - m_new)
            new_stats.append((m_new, alpha * l + jnp.sum(p, axis=0, keepdims=True)))
            ps.append(p.astype(BF16))
            alphas.append(alpha)
        strict = r_i < c_i
        lss, lks, sums = [], [], []
        for h in heads:
            z = zs[h]
            neg_abs = lax.bitcast_convert_type(
                lax.bitcast_convert_type(z, jnp.uint32) | jnp.uint32(0x80000000), F32)
            ls = jnp.minimum(z, 0.0) - jnp.log(1.0 + jnp.exp(neg_abs))
            lk = ls - z
            if diag:
                lk = jnp.where(strict, lk, 0.0)
            lss.append(ls)
            lks.append(lk.astype(BF16))
            sums.append(jnp.sum(lk, axis=0, keepdims=True))
        bts = [jnp.dot(tri_ref[...], lks[h], preferred_element_type=F32) for h in heads]
        for h in heads:
            vt = vtd_ref[kt, h * HEAD_DIM:(h + 1) * HEAD_DIM, :]
            accd_ref[h] = alphas[h] * accd_ref[h] + jnp.dot(vt, ps[h], preferred_element_type=F32)
        pas = []
        for h in heads:
            a = jnp.exp(lss[h] + (bts[h] + keep[h]))
            if diag:
                a = jnp.where(strict, a, 0.0)
            pas.append(a.astype(BF16))
        for h in heads:
            vt = vts_ref[kt, h * HEAD_DIM:(h + 1) * HEAD_DIM, :]
            accs_ref[h] += jnp.dot(vt, pas[h], preferred_element_type=F32)
        return tuple(new_stats), tuple(keep[h] + sums[h] for h in heads)

    init = (tuple((jnp.full((1, 2 * t), -jnp.inf, F32), jnp.zeros((1, 2 * t), F32)) for _ in heads),
            tuple(jnp.zeros((1, t), F32) for _ in heads))
    carry = tile(i, init, True)
    stats, _ = lax.fori_loop(0, i, lambda n, c: tile(i - 1 - n, c, False), carry)

    lv = lamv_ref[...]
    lam = (jnp.exp(jnp.sum(lv[0:1, :] * lv[1:2, :], axis=1, keepdims=True))
           - jnp.exp(jnp.sum(lv[2:3, :] * lv[3:4, :], axis=1, keepdims=True)) + lam_init)
    rows = []
    for h in heads:
        o = accd_ref[h] / stats[h][1]
        oh = o[:, :t] - lam * o[:, t:]
        ms = jnp.mean(oh * oh, axis=0, keepdims=True)
        rows.append(oh * lax.rsqrt(ms + EPS) * g_ref[h * HEAD_DIM:(h + 1) * HEAD_DIM, :]
                    * (1.0 - lam_init))
    od_ref[...] = jnp.concatenate(rows, axis=0).T.astype(BF16)
    os_ref[...] = jnp.concatenate([accs_ref[h] for h in heads], axis=0).T.astype(BF16)


def _attention(attn, vt_d, vt_s, lamv, subln_g, lam_init):
    B, S, _ = attn.shape
    t = KV_TILE
    gain = jnp.broadcast_to(jnp.tile(subln_g, N_HEADS)[:, None], (GROUP_W, t))
    q_spec = lambda c: pl.BlockSpec((None, t, GROUP_W), lambda b, i: (b, i, c))
    k_spec = lambda c: pl.BlockSpec((None, S, GROUP_W), lambda b, i: (b, 0, c))
    vt_spec = pl.BlockSpec((None, S // t, GROUP_W, t), lambda b, i: (b, 0, 0, 0))
    out = jax.ShapeDtypeStruct((B, S, GROUP_W), BF16)
    return pl.pallas_call(
        functools.partial(_attn_kernel, lam_init=lam_init, t=t),
        name="attn",
        grid=(B, S // t),
        in_specs=[q_spec(0), k_spec(1), vt_spec, q_spec(2), k_spec(3), vt_spec,
                  pl.BlockSpec((4, DIFF_DIM), lambda b, i: (0, 0)),
                  pl.BlockSpec((GROUP_W, t), lambda b, i: (0, 0))],
        out_specs=[q_spec(0), q_spec(0)],
        out_shape=[out, out],
        scratch_shapes=[pltpu.VMEM((N_HEADS, 2 * t, GROUP_W), BF16),
                        pltpu.VMEM((N_HEADS, t, GROUP_W), BF16),
                        pltpu.VMEM((t, t), BF16),
                        pltpu.VMEM((N_HEADS, HEAD_DIM, 2 * t), F32),
                        pltpu.VMEM((N_HEADS, HEAD_DIM, t), F32)],
        compiler_params=_cparams(("parallel", "parallel")),
    )(attn, attn, vt_d, attn, attn, vt_s, lamv, gain)


def _gdn_prep_kernel(x_ref, prev_ref, small_ref, cw_ref, alog_ref, dtb_ref,
                     q_ref, k_ref, v_ref, ab_ref, xs_ref, *, ts):
    i = pl.program_id(1)
    prev = prev_ref[...]
    xs_ref[0:8, :] = jnp.where(i == 0, jnp.zeros_like(prev), prev)
    xs_ref[8:8 + ts, :] = x_ref[...]
    conv = jnp.zeros((ts, 3 * GROUP_W), F32)
    for j in range(CONV_K):
        conv = conv + xs_ref[pl.ds(8 - (CONV_K - 1) + j, ts), :] * cw_ref[j:j + 1, :]
    y = conv * _sigmoid(conv)
    q = y[:, 0:GROUP_W]
    k = y[:, GROUP_W:2 * GROUP_W]
    q_ref[...] = (q * lax.rsqrt(_group_sum(q * q, HEAD_DIM) + EPS) * (HEAD_DIM ** -0.5)).astype(BF16)
    k_ref[...] = (k * lax.rsqrt(_group_sum(k * k, HEAD_DIM) + EPS)).astype(BF16)
    v_ref[...] = y[:, 2 * GROUP_W:3 * GROUP_W].astype(BF16)
    small = small_ref[...]
    beta = _sigmoid(small)
    alpha = jnp.exp(-jnp.exp(alog_ref[...]) * _softplus(small + dtb_ref[...]))
    lane = lax.broadcasted_iota(jnp.int32, (1, 128), 1)
    ab_ref[...] = jnp.where(lane < N_HEADS, beta, alpha).T[0:2 * N_HEADS, :]


def _gdn_prep(gdn, conv_w, a_log, dt_bias):
    B, S, _ = gdn.shape
    ts = min(512, S)
    pad = jnp.zeros((N_HEADS,), F32)
    alog = jnp.concatenate([pad, a_log, jnp.zeros((120,), F32)]).reshape(1, 128)
    dtb = jnp.concatenate([pad, dt_bias, jnp.zeros((120,), F32)]).reshape(1, 128)
    tok = jax.ShapeDtypeStruct((B, S, GROUP_W), F32)
    return pl.pallas_call(
        functools.partial(_gdn_prep_kernel, ts=ts),
        name="gdn_prep",
        grid=(B, S // ts),
        in_specs=[
            pl.BlockSpec((None, ts, 3 * GROUP_W), lambda b, i: (b, i, 0)),
            pl.BlockSpec((None, 8, 3 * GROUP_W), lambda b, i: (b, jnp.maximum(i * (ts // 8) - 1, 0), 0)),
            pl.BlockSpec((None, ts, 128), lambda b, i: (b, i, 8)),
            pl.BlockSpec((CONV_K, 3 * GROUP_W), lambda b, i: (0, 0)),
            pl.BlockSpec((1, 128), lambda b, i: (0, 0)),
            pl.BlockSpec((1, 128), lambda b, i: (0, 0)),
        ],
        out_specs=[
            pl.BlockSpec((None, ts, GROUP_W), lambda b, i: (b, i, 0)),
            pl.BlockSpec((None, ts, GROUP_W), lambda b, i: (b, i, 0)),
            pl.BlockSpec((None, ts, GROUP_W), lambda b, i: (b, i, 0)),
            pl.BlockSpec((None, 2 * N_HEADS, ts), lambda b, i: (b, 0, i)),
        ],
        out_shape=[tok.update(dtype=BF16)] * 3 + [jax.ShapeDtypeStruct((B, 2 * N_HEADS, S), F32)],
        scratch_shapes=[pltpu.VMEM((ts + 8, 3 * GROUP_W), F32)],
        compiler_params=_cparams(("parallel", "parallel")),
    )(gdn, gdn, gdn, conv_w, alog, dtb)


def _gdn_scan_kernel(qb_ref, kb_ref, vb_ref, al_ref, be_ref, knb_ref, ng_ref, o_ref,
                     s_ref, ks_ref, q_ref, k_ref, v_ref, kn_ref, *, tt):
    @pl.when(pl.program_id(0) == 0)
    def _():
        s_ref[...] = jnp.zeros_like(s_ref)
        ks_ref[...] = jnp.zeros_like(ks_ref)

    q_ref[...] = qb_ref[...].astype(F32)
    k_ref[...] = kb_ref[...].astype(F32)
    v_ref[...] = vb_ref[...].astype(F32)
    kn_ref[...] = knb_ref[...].astype(F32)

    def step(t, ks, k_next):
        al = al_ref[pl.ds(t, 1), :]
        d = be_ref[pl.ds(t, 1), :] * (v_ref[t] - al * ks)
        o_parts = [None] * 2
        n_parts = [None] * 2
        for kk in range(HEAD_DIM):
            s_new = al * s_ref[kk] + k_ref[t, pl.ds(kk, 1), :] * d
            s_ref[kk] = s_new
            t_o = s_new * q_ref[t, pl.ds(kk, 1), :]
            t_n = s_new * k_next(kk)
            o_parts[kk % 2] = t_o if o_parts[kk % 2] is None else o_parts[kk % 2] + t_o
            n_parts[kk % 2] = t_n if n_parts[kk % 2] is None else n_parts[kk % 2] + t_n
        o = o_parts[0] + o_parts[1]
        ms = jnp.mean(o * o, axis=0, keepdims=True)
        o_ref[t] = o * lax.rsqrt(ms + EPS) * ng_ref[...]
        return n_parts[0] + n_parts[1]

    ks = lax.fori_loop(0, tt - 1,
                       lambda t, ks: step(t, ks, lambda kk: k_ref[t + 1, pl.ds(kk, 1), :]),
                       ks_ref[...])
    ks_ref[...] = step(tt - 1, ks, lambda kk: kn_ref[0, pl.ds(kk, 1), :])


def _next_row_spec(tt, S, N, Ln):
    return pl.BlockSpec((1, N, Ln), lambda i: (jnp.minimum((i + 1) * tt, S - 1), 0, 0))


def _gdn_scan(q, k, v, al, be, norm_g):
    S, N, Ln = q.shape
    tt = min(32, S)
    seq = pl.BlockSpec((tt, N, Ln), lambda i: (i, 0, 0))
    vec = pl.BlockSpec((tt, Ln), lambda i: (i, 0))
    return pl.pallas_call(
        functools.partial(_gdn_scan_kernel, tt=tt),
        name="gdn_scan",
        grid=(S // tt,),
        in_specs=[seq, seq, seq, vec, vec, _next_row_spec(tt, S, N, Ln),
                  pl.BlockSpec((N, Ln), lambda i: (0, 0))],
        out_specs=seq,
        out_shape=jax.ShapeDtypeStruct((S, N, Ln), F32),
        scratch_shapes=[pltpu.VMEM((N, N, Ln), F32), pltpu.VMEM((N, Ln), F32)]
        + [pltpu.VMEM((tt, N, Ln), F32)] * 3 + [pltpu.VMEM((1, N, Ln), F32)],
        compiler_params=_cparams(("arbitrary",)),
    )(q, k, v, al, be, k, jnp.broadcast_to(norm_g[:, None], (N, Ln)))


def _rwkv_prep_kernel(*refs, ts, first):
    if first:
        (x_ref, prev_ref, mu_ref, vecs_ref, wup_ref, aup_ref, gup_ref,
         r_ref, w_ref, k_ref, v_ref, a_ref, b_ref, g_ref, vf_ref, xs_ref) = refs
    else:
        (x_ref, prev_ref, mu_ref, vecs_ref, wup_ref, aup_ref, gup_ref, vup_ref, vfin_ref,
         r_ref, w_ref, k_ref, v_ref, a_ref, b_ref, g_ref, xs_ref) = refs
    i = pl.program_id(1)
    prev = prev_ref[...]
    xs_ref[0:8, :] = jnp.where(i == 0, jnp.zeros_like(prev), prev)
    x = x_ref[...]
    xs_ref[8:8 + ts, :] = x
    xm = x + (xs_ref[pl.ds(7, ts), :] - x) * mu_ref[...]
    r = xm[:, 0:256]
    k = xm[:, 256:512]
    v = xm[:, 512:768]
    wd = xm[:, 768:896]
    ad = xm[:, 896:1024]
    gd = xm[:, 1024:1280]
    w0, a0, k_k, k_a = vecs_ref[0:1, :], vecs_ref[1:2, :], vecs_ref[2:3, :], vecs_ref[3:4, :]
    dot = lambda x, w: jnp.dot(x.astype(BF16), w, preferred_element_type=F32)
    w_log = -_softplus(-(w0 + dot(jnp.tanh(wd), wup_ref[...]))) - 0.5
    a = _sigmoid(a0 + dot(ad, aup_ref[...]))
    g_ref[...] = dot(_sigmoid(gd), gup_ref[...])
    if first:
        vf_ref[...] = v
    else:
        vd = xm[:, 1280:1408]
        v = v + (vfin_ref[...] - v) * _sigmoid(vecs_ref[4:5, :] + dot(vd, vup_ref[...]))
    kk = k * k_k
    kk = kk * lax.rsqrt(_group_sum(kk * kk, HEAD_DIM) + EPS)
    r_ref[...] = r.astype(BF16)
    w_ref[...] = jnp.exp(-jnp.exp(w_log))
    k_ref[...] = (k * (1.0 + (a - 1.0) * k_a)).astype(BF16)
    v_ref[...] = v.astype(BF16)
    a_ref[...] = (-kk).astype(BF16)
    b_ref[...] = (kk * a).astype(BF16)


def _pad_rows(w, rows):
    return jnp.concatenate([w, jnp.zeros((rows - w.shape[0], w.shape[1]), w.dtype)],
                           axis=0).astype(BF16)


def _rwkv_prep(rw, mu, vecs, w_up, a_up, g_up, vmix_up, v_first):
    B, S, _ = rw.shape
    ts = min(512, S)
    first = v_first is None
    tok_spec = pl.BlockSpec((None, ts, GROUP_W), lambda b, i: (b, i, 0))
    const = lambda shape: pl.BlockSpec(shape, lambda b, i: (0,) * len(shape))
    in_specs = [
        pl.BlockSpec((None, ts, RWKV_W), lambda b, i: (b, i, 0)),
        pl.BlockSpec((None, 8, RWKV_W), lambda b, i: (b, jnp.maximum(i * (ts // 8) - 1, 0), 0)),
        const((1, RWKV_W)), const((8, GROUP_W)),
        const((128, GROUP_W)), const((128, GROUP_W)), const((256, GROUP_W)),
    ]
    args = [rw, rw, mu, vecs, _pad_rows(w_up, 128), _pad_rows(a_up, 128), _pad_rows(g_up, 256)]
    if not first:
        in_specs += [const((128, GROUP_W)), tok_spec]
        args += [_pad_rows(vmix_up, 128), v_first]
    tok = jax.ShapeDtypeStruct((B, S, GROUP_W), F32)
    n_out = 8 if first else 7
    return pl.pallas_call(
        functools.partial(_rwkv_prep_kernel, ts=ts, first=first),
        name="rwkv_prep",
        grid=(B, S // ts),
        in_specs=in_specs,
        out_specs=[tok_spec] * n_out,
        out_shape=[tok.update(dtype=BF16 if n in (0, 2, 3, 4, 5) else F32) for n in range(n_out)],
        scratch_shapes=[pltpu.VMEM((ts + 8, RWKV_W), F32)],
        compiler_params=_cparams(("parallel", "parallel")),
    )(*args)


def _wkv_scan_kernel(rb_ref, w_ref, kb_ref, vb_ref, ab_ref, bb_ref, anb_ref, tab_ref, y_ref,
                     s_ref, sa_ref, r_ref, k_ref, v_ref, a_ref, b_ref, an_ref, *, tt):
    @pl.when(pl.program_id(0) == 0)
    def _():
        s_ref[...] = jnp.zeros_like(s_ref)
        sa_ref[...] = jnp.zeros_like(sa_ref)

    for dst, src in ((r_ref, rb_ref), (k_ref, kb_ref), (v_ref, vb_ref), (a_ref, ab_ref),
                     (b_ref, bb_ref), (an_ref, anb_ref)):
        dst[...] = src[...].astype(F32)

    def step(t, sa, a_next):
        vt = v_ref[t]
        y_parts = [None] * 2
        n_parts = [None] * 2
        for j in range(HEAD_DIM):
            s_new = (s_ref[j] * w_ref[t, pl.ds(j, 1), :]
                     + (sa * b_ref[t, pl.ds(j, 1), :] + vt * k_ref[t, pl.ds(j, 1), :]))
            s_ref[j] = s_new
            t_y = s_new * r_ref[t, pl.ds(j, 1), :]
            t_n = s_new * a_next(j)
            y_parts[j % 2] = t_y if y_parts[j % 2] is None else y_parts[j % 2] + t_y
            n_parts[j % 2] = t_n if n_parts[j % 2] is None else n_parts[j % 2] + t_n
        y = y_parts[0] + y_parts[1]
        yc = y - jnp.mean(y, axis=0, keepdims=True)
        var = jnp.mean(yc * yc, axis=0, keepdims=True)
        yn = yc * lax.rsqrt(var + LNX_EPS) * tab_ref[0] + tab_ref[1]
        bonus = jnp.sum(r_ref[t] * k_ref[t] * tab_ref[2], axis=0, keepdims=True)
        y_ref[t] = yn + bonus * vt
        return n_parts[0] + n_parts[1]

    sa = lax.fori_loop(0, tt - 1,
                       lambda t, sa: step(t, sa, lambda j: a_ref[t + 1, pl.ds(j, 1), :]),
                       sa_ref[...])
    sa_ref[...] = step(tt - 1, sa, lambda j: an_ref[0, pl.ds(j, 1), :])


def _wkv_scan(r, w, k, v, a, b, tables):
    S, N, Ln = r.shape
    tt = min(32, S)
    seq = pl.BlockSpec((tt, N, Ln), lambda i: (i, 0, 0))
    return pl.pallas_call(
        functools.partial(_wkv_scan_kernel, tt=tt),
        name="wkv_scan",
        grid=(S // tt,),
        in_specs=[seq] * 6 + [_next_row_spec(tt, S, N, Ln),
                              pl.BlockSpec((3, N, Ln), lambda i: (0, 0, 0))],
        out_specs=seq,
        out_shape=jax.ShapeDtypeStruct((S, N, Ln), F32),
        scratch_shapes=[pltpu.VMEM((N, N, Ln), F32), pltpu.VMEM((N, Ln), F32)]
        + [pltpu.VMEM((tt, N, Ln), F32)] * 5 + [pltpu.VMEM((1, N, Ln), F32)],
        compiler_params=_cparams(("arbitrary",)),
    )(r, w, k, v, a, b, a, tables)


def _pad_cols(w, width):
    pad = width - w.shape[-1]
    return jnp.concatenate([w, jnp.zeros(w.shape[:-1] + (pad,), w.dtype)], axis=-1)


def _rope_perm():
    p = np.arange(128)
    first = (p // 16) * DIFF_DIM + p % 16
    return np.concatenate([first, first + 16])


def _permute_w_in(w_in, vmix_down):
    L, D, _ = w_in.shape
    perm = _rope_perm()
    g0 = 9 * GROUP_W
    r0 = RWKV_START
    vd = jnp.concatenate([jnp.zeros((1, D, vmix_down.shape[-1]), w_in.dtype), vmix_down], axis=0)
    cols = [
        w_in[..., perm], w_in[..., GROUP_W + perm], w_in[..., 2 * GROUP_W:3 * GROUP_W],
        w_in[..., 3 * GROUP_W:6 * GROUP_W],
        w_in[..., 6 * GROUP_W:g0], w_in[..., g0 + 8:g0 + 8 + GROUP_W], _pad_cols(w_in[..., g0:g0 + 8], 128),
        w_in[..., r0:r0 + 768], _pad_cols(w_in[..., r0 + 768:r0 + 832], 128),
        _pad_cols(w_in[..., r0 + 832:r0 + 896], 128), _pad_cols(w_in[..., r0 + 896:r0 + 1056], 256),
        _pad_cols(vd, 128),
    ]
    return jnp.concatenate(cols, axis=-1).astype(BF16)


def _permute_mu(mu, vmix_mu):
    L = mu.shape[0]
    vm = jnp.concatenate([jnp.zeros((1, vmix_mu.shape[-1]), mu.dtype), vmix_mu], axis=0)
    cols = [mu[:, 0:768], _pad_cols(mu[:, 768:832], 128), _pad_cols(mu[:, 832:896], 128),
            _pad_cols(mu[:, 896:1056], 256), _pad_cols(vm, 128)]
    return jnp.concatenate(cols, axis=-1).reshape(L, 1, RWKV_W)


def _to_lanes(x):
    B, S, _ = x.shape
    return x.reshape(B, S, N_HEADS, HEAD_DIM).transpose(1, 3, 0, 2).reshape(S, HEAD_DIM, B * N_HEADS)


def _from_lanes(y, B):
    S = y.shape[0]
    return y.reshape(S, HEAD_DIM, B, N_HEADS).transpose(2, 0, 3, 1).reshape(B, S, GROUP_W)


def _per_head_table(x, B):
    return jnp.tile(x.T, (1, B))


def _head_scalars_to_lanes(x):
    B, _, S = x.shape
    return x.transpose(2, 0, 1).reshape(S, B * N_HEADS)


def kernel(x, c, ada_w, ada_b, norm_g, ffn_w13, ffn_w2, w_in, w_out, diff_qk_g, diff_lambda, diff_subln_g, gdn_conv_w, gdn_a_log, gdn_dt_bias, gdn_norm_g, rwkv_mu, rwkv_w0, rwkv_w_up, rwkv_a0, rwkv_a_up, rwkv_g_up, rwkv_k_k, rwkv_k_a, rwkv_r_k, rwkv_lnx_w, rwkv_lnx_b, rwkv_vmix_down, rwkv_vmix_mu, rwkv_v0, rwkv_vmix_up):
    B, S, D = x.shape
    L = ada_w.shape[0]
    T = B * S

    mod = _mod_all(c, ada_w, ada_b)
    w13 = ffn_w13.astype(BF16)
    w2 = ffn_w2.astype(BF16)
    w_perm = _permute_w_in(w_in, rwkv_vmix_down)
    w_o = w_out.astype(BF16)
    mu = _permute_mu(rwkv_mu, rwkv_vmix_mu)

    inv = ROPE_THETA ** (-jnp.arange(0, DIFF_DIM, 2, dtype=F32) / DIFF_DIM)
    ang = jnp.arange(S, dtype=F32)[:, None] * inv[None, :]
    cos128 = jnp.tile(jnp.cos(ang), (1, 8))
    sin128 = jnp.tile(jnp.sin(ang), (1, 8))

    h = x.reshape(T, D)
    v_first = None
    for l in range(L):
        lam_init = 0.8 - 0.6 * math.exp(-0.3 * l)
        h = _ffn(h, mod[l], norm_g[l, 0], w13, w2, l, 0, 0, S)

        qkg = jnp.stack([jnp.tile(diff_qk_g[l, 0, :16], 8), jnp.tile(diff_qk_g[l, 0, 16:], 8),
                         jnp.tile(diff_qk_g[l, 1, :16], 8), jnp.tile(diff_qk_g[l, 1, 16:], 8)])
        attn, vt_d, vt_s, gdn, rw = _inproj(h, mod[l], norm_g[l, 1], w_perm, l, qkg,
                                            cos128, sin128, S)
        attn = attn.reshape(B, S, QK_W)
        kv_tiles = lambda t: t.reshape(B, S // KV_TILE, GROUP_W, KV_TILE)
        gdn = gdn.reshape(B, S, GDN_W)
        rw = rw.reshape(B, S, RWKV_W)

        o_a, o_b = _attention(attn, kv_tiles(vt_d), kv_tiles(vt_s), diff_lambda[l],
                              diff_subln_g[l], lam_init)

        gq, gk, gv, gab = _gdn_prep(gdn, gdn_conv_w[l], gdn_a_log[l], gdn_dt_bias[l])
        go = _gdn_scan(_to_lanes(gq), _to_lanes(gk), _to_lanes(gv),
                       _head_scalars_to_lanes(gab[:, N_HEADS:2 * N_HEADS]),
                       _head_scalars_to_lanes(gab[:, 0:N_HEADS]), gdn_norm_g[l])
        o_c = _from_lanes(go, B)

        zeros = jnp.zeros((GROUP_W,), F32)
        vecs = jnp.stack([rwkv_w0[l], rwkv_a0[l], rwkv_k_k[l], rwkv_k_a[l],
                          rwkv_v0[l - 1] if l > 0 else zeros, zeros, zeros, zeros])
        outs = _rwkv_prep(rw, mu[l], vecs, rwkv_w_up[l], rwkv_a_up[l], rwkv_g_up[l],
                          rwkv_vmix_up[l - 1] if l > 0 else None, v_first)
        rr, rw_, rk, rv, ra, rb, rg = outs[:7]
        if l == 0:
            v_first = outs[7]
        tables = jnp.stack([_per_head_table(t.reshape(N_HEADS, HEAD_DIM), B)
                            for t in (rwkv_lnx_w[l], rwkv_lnx_b[l], rwkv_r_k[l])])
        ry = _wkv_scan(*[_to_lanes(t) for t in (rr, rw_, rk, rv, ra, rb)], tables)
        o_d = _from_lanes(ry, B)

        h = _ffn(h, mod[l], norm_g[l, 2], w13, w2, l, 1, 6, S,
                 mixed=(o_a, o_b, o_c, o_d), w_out=w_o, gates=(gdn, rg))
    return h.reshape(B, S, D)
```

```python
import functools
import math

import numpy as np
import jax
import jax.numpy as jnp
from jax import lax
from jax.experimental import pallas as pl
from jax.experimental.pallas import tpu as pltpu

F32 = jnp.float32
BF16 = jnp.bfloat16
HIGHEST = lax.Precision.HIGHEST

D_MODEL = 1024
N_HEADS = 4
HEAD_DIM = 64
GROUP_W = N_HEADS * HEAD_DIM
DIFF_DIM = HEAD_DIM // 2
D_FF = 2816
N_MOD = 9
EPS = 1e-6
LNX_EPS = 64e-5
ROPE_THETA = 10000.0
LOG2E = 1.4426950408889634
CONV_K = 4
RWKV_START = 9 * GROUP_W + 2 * N_HEADS + GROUP_W

ATTN_W = 6 * GROUP_W
QK_W = 4 * GROUP_W
KV_TILE = 256
GDN_W = 3 * GROUP_W + GROUP_W + 128
RWKV_W = 3 * GROUP_W + 128 + 128 + 256 + 128
PROJ_W = ATTN_W + GDN_W + RWKV_W
VMEM_LIMIT = 56 * 1024 * 1024


def _cparams(sem):
    return pltpu.CompilerParams(dimension_semantics=sem, vmem_limit_bytes=VMEM_LIMIT)


def _sigmoid(x):
    return 1.0 / (1.0 + jnp.exp(-x))


def _softplus(x):
    return jnp.maximum(x, 0.0) + jnp.log(1.0 + jnp.exp(-jnp.abs(x)))


def _group_ones(n, group):
    r = lax.broadcasted_iota(jnp.int32, (n, n), 0) // group
    c = lax.broadcasted_iota(jnp.int32, (n, n), 1) // group
    return (r == c).astype(BF16)


def _group_sum(x, group):
    ones = _group_ones(x.shape[-1], group)
    hi = x.astype(BF16)
    lo = (x - hi.astype(F32)).astype(BF16)
    return (jnp.dot(hi, ones, preferred_element_type=F32)
            + jnp.dot(lo, ones, preferred_element_type=F32))


def _modulate(x, gain, shift, scale):
    ms = jnp.mean(x * x, axis=-1, keepdims=True)
    return (x * lax.rsqrt(ms + EPS) * gain) * (1.0 + scale) + shift


def _mod_kernel(c_ref, w_ref, b_ref, o_ref):
    c = c_ref[...]
    cond = c * _sigmoid(c)
    o_ref[...] = jnp.dot(cond, w_ref[...], precision=HIGHEST,
                         preferred_element_type=F32) + b_ref[...]


def _mod_all(c, ada_w, ada_b):
    L, D, N = ada_w.shape
    B = c.shape[0]
    tn = 1024
    out = pl.pallas_call(
        _mod_kernel,
        name="adaln_mod",
        grid=(L, N // tn),
        in_specs=[
            pl.BlockSpec((B, D), lambda l, n: (0, 0)),
            pl.BlockSpec((None, D, tn), lambda l, n: (l, 0, n)),
            pl.BlockSpec((None, 1, tn), lambda l, n: (l, 0, n)),
        ],
        out_specs=pl.BlockSpec((None, B, tn), lambda l, n: (l, 0, n)),
        out_shape=jax.ShapeDtypeStruct((L, B, N), F32),
        compiler_params=_cparams(("parallel", "parallel")),
    )(c, ada_w, ada_b.reshape(L, 1, N))
    return out.reshape(L, B, N_MOD, D)


def _ffn_kernel(*refs, mb, tf, mix):
    x_ref, mod_ref, g_ref, w13_ref, w2_ref = refs[:5]
    o_ref, act_ref = refs[-2:]
    x = x_ref[...]
    if mix:
        oa_ref, ob_ref, oc_ref, od_ref, w_ref, gc_ref, gd_ref = refs[5:12]
        gc = gc_ref[...]
        o_c = (oc_ref[...].astype(F32) * (gc * _sigmoid(gc))).astype(BF16)
        o_d = (od_ref[...].astype(F32) * gd_ref[...]).astype(BF16)
        proj = jnp.dot(oa_ref[...], w_ref[0:GROUP_W, :], preferred_element_type=F32)
        for g, o_g in ((1, ob_ref[...]), (2, o_c), (3, o_d)):
            proj += jnp.dot(o_g, w_ref[g * GROUP_W:(g + 1) * GROUP_W, :],
                            preferred_element_type=F32)
        x = x + mod_ref[5:6, :] * proj
    hn = _modulate(x, g_ref[...], mod_ref[mb:mb + 1, :], mod_ref[mb + 1:mb + 2, :]).astype(BF16)
    for f in range(D_FF // tf):
        gate = jnp.dot(hn, w13_ref[:, f * tf:(f + 1) * tf], preferred_element_type=F32)
        up = jnp.dot(hn, w13_ref[:, D_FF + f * tf:D_FF + (f + 1) * tf], preferred_element_type=F32)
        act_ref[:, f * tf:(f + 1) * tf] = (gate * _sigmoid(gate) * up).astype(BF16)
    acc = jnp.dot(act_ref[...], w2_ref[...], preferred_element_type=F32)
    o_ref[...] = x + 0.5 * mod_ref[mb + 2:mb + 3, :] * acc


def _ffn(h, mod_l, gain, w13, w2, l, j, mb, S, mixed=None, w_out=None, gates=None):
    T, D = h.shape
    tm = min(1024, S)
    tf = 256
    tpb = S // tm
    resident = dict(pipeline_mode=pl.Buffered(1))
    in_specs = [
        pl.BlockSpec((tm, D), lambda i: (i, 0)),
        pl.BlockSpec((None, N_MOD, D), lambda i: (i // tpb, 0, 0)),
        pl.BlockSpec((1, D), lambda i: (0, 0)),
        pl.BlockSpec((None, None, D, 2 * D_FF), lambda i: (l, j, 0, 0), **resident),
        pl.BlockSpec((None, None, D_FF, D), lambda i: (l, j, 0, 0), **resident),
    ]
    args = [h, mod_l, gain.reshape(1, D), w13, w2]
    if mixed is not None:
        in_specs += [pl.BlockSpec((None, tm, GROUP_W), lambda i: (i // tpb, i % tpb, 0))] * 4
        in_specs += [pl.BlockSpec((None, D, D), lambda i: (l, 0, 0), **resident)]
        in_specs += [pl.BlockSpec((None, tm, GROUP_W), lambda i: (i // tpb, i % tpb, 3)),
                     pl.BlockSpec((None, tm, GROUP_W), lambda i: (i // tpb, i % tpb, 0))]
        args += list(mixed) + [w_out] + list(gates)
    return pl.pallas_call(
        functools.partial(_ffn_kernel, mb=mb, tf=tf, mix=mixed is not None),
        name="ffn",
        grid=(T // tm,),
        in_specs=in_specs,
        out_specs=pl.BlockSpec((tm, D), lambda i: (i, 0)),
        out_shape=jax.ShapeDtypeStruct((T, D), F32),
        scratch_shapes=[pltpu.VMEM((tm, D_FF), BF16)],
        compiler_params=_cparams(("parallel",)),
    )(*args)


def _inproj_kernel(x_ref, mod_ref, g_ref, w_ref, qkg_ref, cos_ref, sin_ref,
                   attn_ref, vtd_ref, vts_ref, gdn_ref, rwkv_ref):
    hn = _modulate(x_ref[...], g_ref[...], mod_ref[3:4, :], mod_ref[4:5, :]).astype(BF16)

    pa = jnp.dot(hn, w_ref[:, 0:3 * GROUP_W], preferred_element_type=F32)
    cos = cos_ref[...]
    sin = sin_ref[...]
    for idx, scale in ((0, DIFF_DIM ** -0.5 * LOG2E), (1, 1.0)):
        x1 = pa[:, idx * 256:idx * 256 + 128]
        x2 = pa[:, idx * 256 + 128:idx * 256 + 256]
        ms = _group_sum(x1 * x1 + x2 * x2, 16) * (1.0 / DIFF_DIM)
        inv = lax.rsqrt(ms + EPS)
        n1 = x1 * inv * qkg_ref[2 * idx:2 * idx + 1, :]
        n2 = x2 * inv * qkg_ref[2 * idx + 1:2 * idx + 2, :]
        attn_ref[:, idx * 256:idx * 256 + 128] = ((n1 * cos - n2 * sin) * scale).astype(BF16)
        attn_ref[:, idx * 256 + 128:idx * 256 + 256] = ((n2 * cos + n1 * sin) * scale).astype(BF16)
    n_kv = x_ref.shape[0] // KV_TILE
    for c in range(n_kv):
        vtd_ref[c] = pa[c * KV_TILE:(c + 1) * KV_TILE, 512:768].T.astype(BF16)

    pb = jnp.dot(hn, w_ref[:, 3 * GROUP_W:ATTN_W], preferred_element_type=F32)
    attn_ref[:, 512:768] = (pb[:, 0:256] * (HEAD_DIM ** -0.5)).astype(BF16)
    attn_ref[:, 768:1024] = pb[:, 256:512].astype(BF16)
    for c in range(n_kv):
        vts_ref[c] = pb[c * KV_TILE:(c + 1) * KV_TILE, 512:768].T.astype(BF16)

    gdn_ref[...] = jnp.dot(hn, w_ref[:, ATTN_W:ATTN_W + GDN_W], preferred_element_type=F32)
    rwkv_ref[...] = jnp.dot(hn, w_ref[:, ATTN_W + GDN_W:PROJ_W], preferred_element_type=F32)


def _inproj(h, mod_l, gain, w_perm, l, qkg, cos128, sin128, S):
    T, D = h.shape
    tm = min(512, S)
    tpb = S // tm
    vt_spec = pl.BlockSpec((tm // KV_TILE, GROUP_W, KV_TILE), lambda i: (i, 0, 0))
    vt_shape = jax.ShapeDtypeStruct((T // KV_TILE, GROUP_W, KV_TILE), BF16)
    return pl.pallas_call(
        _inproj_kernel,
        name="inproj",
        grid=(T // tm,),
        in_specs=[
            pl.BlockSpec((tm, D), lambda i: (i, 0)),
            pl.BlockSpec((None, N_MOD, D), lambda i: (i // tpb, 0, 0)),
            pl.BlockSpec((1, D), lambda i: (0, 0)),
            pl.BlockSpec((None, D, PROJ_W), lambda i: (l, 0, 0)),
            pl.BlockSpec((4, 128), lambda i: (0, 0)),
            pl.BlockSpec((tm, 128), lambda i: (i % tpb, 0)),
            pl.BlockSpec((tm, 128), lambda i: (i % tpb, 0)),
        ],
        out_specs=[
            pl.BlockSpec((tm, QK_W), lambda i: (i, 0)),
            vt_spec, vt_spec,
            pl.BlockSpec((tm, GDN_W), lambda i: (i, 0)),
            pl.BlockSpec((tm, RWKV_W), lambda i: (i, 0)),
        ],
        out_shape=[
            jax.ShapeDtypeStruct((T, QK_W), BF16),
            vt_shape, vt_shape,
            jax.ShapeDtypeStruct((T, GDN_W), F32),
            jax.ShapeDtypeStruct((T, RWKV_W), F32),
        ],
        compiler_params=_cparams(("parallel",)),
    )(h, mod_l, gain.reshape(1, D), w_perm, qkg, cos128, sin128)


def _attn_kernel(qd_ref, kd_ref, vtd_ref, qs_ref, ks_ref, vts_ref, lamv_ref, g_ref,
                 od_ref, os_ref, qc_ref, qh_ref, tri_ref, accd_ref, accs_ref, *, lam_init, t):
    i = pl.program_id(1)
    heads = range(N_HEADS)
    lane = lax.broadcasted_iota(jnp.int32, (1, GROUP_W), 1)
    r_i = lax.broadcasted_iota(jnp.int32, (t, t), 0)
    c_i = lax.broadcasted_iota(jnp.int32, (t, t), 1)
    tri_ref[...] = (c_i > r_i).astype(BF16)
    qd = qd_ref[...]
    qs = qs_ref[...]
    sub = (lane % 128) // 16
    for h in heads:
        qc_ref[h, 0:t, :] = jnp.where(sub == 2 * h, qd, jnp.zeros_like(qd))
        qc_ref[h, t:2 * t, :] = jnp.where(sub == 2 * h + 1, qd, jnp.zeros_like(qd))
        qh_ref[h] = jnp.where(lane // HEAD_DIM == h, qs, jnp.zeros_like(qs))
    accd_ref[...] = jnp.zeros_like(accd_ref)
    accs_ref[...] = jnp.zeros_like(accs_ref)

    def tile(kt, carry, diag):
        stats, keep = carry
        start = pl.multiple_of(kt * t, t)
        kd = kd_ref[pl.ds(start, t), :]
        ks = ks_ref[pl.ds(start, t), :]
        nt = (((1,), (1,)), ((), ()))
        ss = [lax.dot_general(kd, qc_ref[h], nt, preferred_element_type=F32) for h in heads]
        zs = [lax.dot_general(ks, qh_ref[h], nt, preferred_element_type=F32) for h in heads]
        new_stats, ps, alphas = [], [], []
        for h in heads:
            m, l = stats[h]
            s = ss[h]
            if diag:
                allowed = jnp.concatenate([r_i <= c_i, r_i <= c_i], axis=1)
                s = jnp.where(allowed, s, -jnp.inf)
            m_new = jnp.maximum(m, jnp.max(s, axis=0, keepdims=True))
            alpha = jnp.exp2(m - m_new)
            p = jnp.exp2(s - m_new)
            new_stats.append((m_new, alpha * l + jnp.sum(p, axis=0, keepdims=True)))
            ps.append(p.astype(BF16))
            alphas.append(alpha)
        strict = r_i < c_i
        lss, lks, sums = [], [], []
        for h in heads:
            z = zs[h]
            neg_abs = lax.bitcast_convert_type(
                lax.bitcast_convert_type(z, jnp.uint32) | jnp.uint32(0x80000000), F32)
            ls = jnp.minimum(z, 0.0) - jnp.log(1.0 + jnp.exp(neg_abs))
            lk = ls - z
            if diag:
                lk = jnp.where(strict, lk, 0.0)
            lss.append(ls)
            lks.append(lk.astype(BF16))
            sums.append(jnp.sum(lk, axis=0, keepdims=True))
        bts = [jnp.dot(tri_ref[...], lks[h], preferred_element_type=F32) for h in heads]
        for h in heads:
            vt = vtd_ref[kt, h * HEAD_DIM:(h + 1) * HEAD_DIM, :]
            accd_ref[h] = alphas[h] * accd_ref[h] + jnp.dot(vt, ps[h], preferred_element_type=F32)
        pas = []
        for h in heads:
            a = jnp.exp(lss[h] + (bts[h] + keep[h]))
            if diag:
                a = jnp.where(strict, a, 0.0)
            pas.append(a.astype(BF16))
        for h in heads:
            vt = vts_ref[kt, h * HEAD_DIM:(h + 1) * HEAD_DIM, :]
            accs_ref[h] += jnp.dot(vt, pas[h], preferred_element_type=F32)
        return tuple(new_stats), tuple(keep[h] + sums[h] for h in heads)

    init = (tuple((jnp.full((1, 2 * t), -jnp.inf, F32), jnp.zeros((1, 2 * t), F32)) for _ in heads),
            tuple(jnp.zeros((1, t), F32) for _ in heads))
    carry = tile(i, init, True)
    stats, _ = lax.fori_loop(0, i, lambda n, c: tile(i - 1 - n, c, False), carry)

    lv = lamv_ref[...]
    lam = (jnp.exp(jnp.sum(lv[0:1, :] * lv[1:2, :], axis=1, keepdims=True))
           - jnp.exp(jnp.sum(lv[2:3, :] * lv[3:4, :], axis=1, keepdims=True)) + lam_init)
    rows = []
    for h in heads:
        o = accd_ref[h] / stats[h][1]
        oh = o[:, :t] - lam * o[:, t:]
        ms = jnp.mean(oh * oh, axis=0, keepdims=True)
        rows.append(oh * lax.rsqrt(ms + EPS) * g_ref[h * HEAD_DIM:(h + 1) * HEAD_DIM, :]
                    * (1.0 - lam_init))
    od_ref[...] = jnp.concatenate(rows, axis=0).T.astype(BF16)
    os_ref[...] = jnp.concatenate([accs_ref[h] for h in heads], axis=0).T.astype(BF16)


def _attention(attn, vt_d, vt_s, lamv, subln_g, lam_init):
    B, S, _ = attn.shape
    t = KV_TILE
    gain = jnp.broadcast_to(jnp.tile(subln_g, N_HEADS)[:, None], (GROUP_W, t))
    q_spec = lambda c: pl.BlockSpec((None, t, GROUP_W), lambda b, i: (b, i, c))
    k_spec = lambda c: pl.BlockSpec((None, S, GROUP_W), lambda b, i: (b, 0, c))
    vt_spec = pl.BlockSpec((None, S // t, GROUP_W, t), lambda b, i: (b, 0, 0, 0))
    out = jax.ShapeDtypeStruct((B, S, GROUP_W), BF16)
    return pl.pallas_call(
        functools.partial(_attn_kernel, lam_init=lam_init, t=t),
        name="attn",
        grid=(B, S // t),
        in_specs=[q_spec(0), k_spec(1), vt_spec, q_spec(2), k_spec(3), vt_spec,
                  pl.BlockSpec((4, DIFF_DIM), lambda b, i: (0, 0)),
                  pl.BlockSpec((GROUP_W, t), lambda b, i: (0, 0))],
        out_specs=[q_spec(0), q_spec(0)],
        out_shape=[out, out],
        scratch_shapes=[pltpu.VMEM((N_HEADS, 2 * t, GROUP_W), BF16),
                        pltpu.VMEM((N_HEADS, t, GROUP_W), BF16),
                        pltpu.VMEM((t, t), BF16),
                        pltpu.VMEM((N_HEADS, HEAD_DIM, 2 * t), F32),
                        pltpu.VMEM((N_HEADS, HEAD_DIM, t), F32)],
        compiler_params=_cparams(("parallel", "parallel")),
    )(attn, attn, vt_d, attn, attn, vt_s, lamv, gain)


def _gdn_prep_kernel(x_ref, prev_ref, small_ref, cw_ref, alog_ref, dtb_ref,
                     q_ref, k_ref, v_ref, ab_ref, xs_ref, *, ts):
    i = pl.program_id(1)
    prev = prev_ref[...]
    xs_ref[0:8, :] = jnp.where(i == 0, jnp.zeros_like(prev), prev)
    xs_ref[8:8 + ts, :] = x_ref[...]
    conv = jnp.zeros((ts, 3 * GROUP_W), F32)
    for j in range(CONV_K):
        conv = conv + xs_ref[pl.ds(8 - (CONV_K - 1) + j, ts), :] * cw_ref[j:j + 1, :]
    y = conv * _sigmoid(conv)
    q = y[:, 0:GROUP_W]
    k = y[:, GROUP_W:2 * GROUP_W]
    q_ref[...] = (q * lax.rsqrt(_group_sum(q * q, HEAD_DIM) + EPS) * (HEAD_DIM ** -0.5)).astype(BF16)
    k_ref[...] = (k * lax.rsqrt(_group_sum(k * k, HEAD_DIM) + EPS)).astype(BF16)
    v_ref[...] = y[:, 2 * GROUP_W:3 * GROUP_W].astype(BF16)
    small = small_ref[...]
    beta = _sigmoid(small)
    alpha = jnp.exp(-jnp.exp(alog_ref[...]) * _softplus(small + dtb_ref[...]))
    lane = lax.broadcasted_iota(jnp.int32, (1, 128), 1)
    ab_ref[...] = jnp.where(lane < N_HEADS, beta, alpha).T[0:2 * N_HEADS, :]


def _gdn_prep(gdn, conv_w, a_log, dt_bias):
    B, S, _ = gdn.shape
    ts = min(512, S)
    pad = jnp.zeros((N_HEADS,), F32)
    alog = jnp.concatenate([pad, a_log, jnp.zeros((120,), F32)]).reshape(1, 128)
    dtb = jnp.concatenate([pad, dt_bias, jnp.zeros((120,), F32)]).reshape(1, 128)
    tok = jax.ShapeDtypeStruct((B, S, GROUP_W), F32)
    return pl.pallas_call(
        functools.partial(_gdn_prep_kernel, ts=ts),
        name="gdn_prep",
        grid=(B, S // ts),
        in_specs=[
            pl.BlockSpec((None, ts, 3 * GROUP_W), lambda b, i: (b, i, 0)),
            pl.BlockSpec((None, 8, 3 * GROUP_W), lambda b, i: (b, jnp.maximum(i * (ts // 8) - 1, 0), 0)),
            pl.BlockSpec((None, ts, 128), lambda b, i: (b, i, 8)),
            pl.BlockSpec((CONV_K, 3 * GROUP_W), lambda b, i: (0, 0)),
            pl.BlockSpec((1, 128), lambda b, i: (0, 0)),
            pl.BlockSpec((1, 128), lambda b, i: (0, 0)),
        ],
        out_specs=[
            pl.BlockSpec((None, ts, GROUP_W), lambda b, i: (b, i, 0)),
            pl.BlockSpec((None, ts, GROUP_W), lambda b, i: (b, i, 0)),
            pl.BlockSpec((None, ts, GROUP_W), lambda b, i: (b, i, 0)),
            pl.BlockSpec((None, 2 * N_HEADS, ts), lambda b, i: (b, 0, i)),
        ],
        out_shape=[tok.update(dtype=BF16)] * 3 + [jax.ShapeDtypeStruct((B, 2 * N_HEADS, S), F32)],
        scratch_shapes=[pltpu.VMEM((ts + 8, 3 * GROUP_W), F32)],
        compiler_params=_cparams(("parallel", "parallel")),
    )(gdn, gdn, gdn, conv_w, alog, dtb)


def _gdn_scan_kernel(qb_ref, kb_ref, vb_ref, al_ref, be_ref, knb_ref, ng_ref, o_ref,
                     s_ref, ks_ref, q_ref, k_ref, v_ref, kn_ref, *, tt):
    @pl.when(pl.program_id(0) == 0)
    def _():
        s_ref[...] = jnp.zeros_like(s_ref)
        ks_ref[...] = jnp.zeros_like(ks_ref)

    q_ref[...] = qb_ref[...].astype(F32)
    k_ref[...] = kb_ref[...].astype(F32)
    v_ref[...] = vb_ref[...].astype(F32)
    kn_ref[...] = knb_ref[...].astype(F32)

    def step(t, ks, k_next):
        al = al_ref[pl.ds(t, 1), :]
        d = be_ref[pl.ds(t, 1), :] * (v_ref[t] - al * ks)
        o_parts = [None] * 2
        n_parts = [None] * 2
        for kk in range(HEAD_DIM):
            s_new = al * s_ref[kk] + k_ref[t, pl.ds(kk, 1), :] * d
            s_ref[kk] = s_new
            t_o = s_new * q_ref[t, pl.ds(kk, 1), :]
            t_n = s_new * k_next(kk)
            o_parts[kk % 2] = t_o if o_parts[kk % 2] is None else o_parts[kk % 2] + t_o
            n_parts[kk % 2] = t_n if n_parts[kk % 2] is None else n_parts[kk % 2] + t_n
        return n_parts[0] + n_parts[1], o_parts[0] + o_parts[1]

    def finish(t, o):
        ms = jnp.mean(o * o, axis=0, keepdims=True)
        o_ref[t] = o * lax.rsqrt(ms + EPS) * ng_ref[...]

    def body(t, carry):
        ks, o_prev = carry
        finish(t - 1, o_prev)
        return step(t, ks, lambda kk: k_ref[t + 1, pl.ds(kk, 1), :])

    carry = step(0, ks_ref[...], lambda kk: k_ref[1, pl.ds(kk, 1), :])
    ks, o_prev = lax.fori_loop(1, tt - 1, body, carry)
    finish(tt - 2, o_prev)
    ks, o_last = step(tt - 1, ks, lambda kk: kn_ref[0, pl.ds(kk, 1), :])
    finish(tt - 1, o_last)
    ks_ref[...] = ks


def _next_row_spec(tt, S, N, Ln):
    return pl.BlockSpec((1, N, Ln), lambda i: (jnp.minimum((i + 1) * tt, S - 1), 0, 0))


def _gdn_scan(q, k, v, al, be, norm_g):
    S, N, Ln = q.shape
    tt = min(32, S)
    seq = pl.BlockSpec((tt, N, Ln), lambda i: (i, 0, 0))
    vec = pl.BlockSpec((tt, Ln), lambda i: (i, 0))
    return pl.pallas_call(
        functools.partial(_gdn_scan_kernel, tt=tt),
        name="gdn_scan",
        grid=(S // tt,),
        in_specs=[seq, seq, seq, vec, vec, _next_row_spec(tt, S, N, Ln),
                  pl.BlockSpec((N, Ln), lambda i: (0, 0))],
        out_specs=seq,
        out_shape=jax.ShapeDtypeStruct((S, N, Ln), F32),
        scratch_shapes=[pltpu.VMEM((N, N, Ln), F32), pltpu.VMEM((N, Ln), F32)]
        + [pltpu.VMEM((tt, N, Ln), F32)] * 3 + [pltpu.VMEM((1, N, Ln), F32)],
        compiler_params=_cparams(("arbitrary",)),
    )(q, k, v, al, be, k, jnp.broadcast_to(norm_g[:, None], (N, Ln)))


def _rwkv_prep_kernel(*refs, ts, first):
    if first:
        (x_ref, prev_ref, mu_ref, vecs_ref, wup_ref, aup_ref, gup_ref,
         r_ref, w_ref, k_ref, v_ref, a_ref, b_ref, g_ref, vf_ref, xs_ref) = refs
    else:
        (x_ref, prev_ref, mu_ref, vecs_ref, wup_ref, aup_ref, gup_ref, vup_ref, vfin_ref,
         r_ref, w_ref, k_ref, v_ref, a_ref, b_ref, g_ref, xs_ref) = refs
    i = pl.program_id(1)
    prev = prev_ref[...]
    xs_ref[0:8, :] = jnp.where(i == 0, jnp.zeros_like(prev), prev)
    x = x_ref[...]
    xs_ref[8:8 + ts, :] = x
    xm = x + (xs_ref[pl.ds(7, ts), :] - x) * mu_ref[...]
    r = xm[:, 0:256]
    k = xm[:, 256:512]
    v = xm[:, 512:768]
    wd = xm[:, 768:896]
    ad = xm[:, 896:1024]
    gd = xm[:, 1024:1280]
    w0, a0, k_k, k_a = vecs_ref[0:1, :], vecs_ref[1:2, :], vecs_ref[2:3, :], vecs_ref[3:4, :]
    dot = lambda x, w: jnp.dot(x.astype(BF16), w, preferred_element_type=F32)
    w_log = -_softplus(-(w0 + dot(jnp.tanh(wd), wup_ref[...]))) - 0.5
    a = _sigmoid(a0 + dot(ad, aup_ref[...]))
    g_ref[...] = dot(_sigmoid(gd), gup_ref[...])
    if first:
        vf_ref[...] = v
    else:
        vd = xm[:, 1280:1408]
        v = v + (vfin_ref[...] - v) * _sigmoid(vecs_ref[4:5, :] + dot(vd, vup_ref[...]))
    kk = k * k_k
    kk = kk * lax.rsqrt(_group_sum(kk * kk, HEAD_DIM) + EPS)
    r_ref[...] = r.astype(BF16)
    w_ref[...] = jnp.exp(-jnp.exp(w_log))
    k_ref[...] = (k * (1.0 + (a - 1.0) * k_a)).astype(BF16)
    v_ref[...] = v.astype(BF16)
    a_ref[...] = (-kk).astype(BF16)
    b_ref[...] = (kk * a).astype(BF16)


def _pad_rows(w, rows):
    return jnp.concatenate([w, jnp.zeros((rows - w.shape[0], w.shape[1]), w.dtype)],
                           axis=0).astype(BF16)


def _rwkv_prep(rw, mu, vecs, w_up, a_up, g_up, vmix_up, v_first):
    B, S, _ = rw.shape
    ts = min(512, S)
    first = v_first is None
    tok_spec = pl.BlockSpec((None, ts, GROUP_W), lambda b, i: (b, i, 0))
    const = lambda shape: pl.BlockSpec(shape, lambda b, i: (0,) * len(shape))
    in_specs = [
        pl.BlockSpec((None, ts, RWKV_W), lambda b, i: (b, i, 0)),
        pl.BlockSpec((None, 8, RWKV_W), lambda b, i: (b, jnp.maximum(i * (ts // 8) - 1, 0), 0)),
        const((1, RWKV_W)), const((8, GROUP_W)),
        const((128, GROUP_W)), const((128, GROUP_W)), const((256, GROUP_W)),
    ]
    args = [rw, rw, mu, vecs, _pad_rows(w_up, 128), _pad_rows(a_up, 128), _pad_rows(g_up, 256)]
    if not first:
        in_specs += [const((128, GROUP_W)), tok_spec]
        args += [_pad_rows(vmix_up, 128), v_first]
    tok = jax.ShapeDtypeStruct((B, S, GROUP_W), F32)
    n_out = 8 if first else 7
    return pl.pallas_call(
        functools.partial(_rwkv_prep_kernel, ts=ts, first=first),
        name="rwkv_prep",
        grid=(B, S // ts),
        in_specs=in_specs,
        out_specs=[tok_spec] * n_out,
        out_shape=[tok.update(dtype=BF16 if n in (0, 2, 3, 4, 5) else F32) for n in range(n_out)],
        scratch_shapes=[pltpu.VMEM((ts + 8, RWKV_W), F32)],
        compiler_params=_cparams(("parallel", "parallel")),
    )(*args)


def _wkv_scan_kernel(rb_ref, w_ref, kb_ref, vb_ref, ab_ref, bb_ref, anb_ref, tab_ref, y_ref,
                     s_ref, sa_ref, r_ref, k_ref, v_ref, a_ref, b_ref, an_ref, *, tt):
    @pl.when(pl.program_id(0) == 0)
    def _():
        s_ref[...] = jnp.zeros_like(s_ref)
        sa_ref[...] = jnp.zeros_like(sa_ref)

    for dst, src in ((r_ref, rb_ref), (k_ref, kb_ref), (v_ref, vb_ref), (a_ref, ab_ref),
                     (b_ref, bb_ref), (an_ref, anb_ref)):
        dst[...] = src[...].astype(F32)

    def step(t, sa, a_next):
        vt = v_ref[t]
        y_parts = [None] * 2
        n_parts = [None] * 2
        for j in range(HEAD_DIM):
            s_new = (s_ref[j] * w_ref[t, pl.ds(j, 1), :]
                     + (sa * b_ref[t, pl.ds(j, 1), :] + vt * k_ref[t, pl.ds(j, 1), :]))
            s_ref[j] = s_new
            t_y = s_new * r_ref[t, pl.ds(j, 1), :]
            t_n = s_new * a_next(j)
            y_parts[j % 2] = t_y if y_parts[j % 2] is None else y_parts[j % 2] + t_y
            n_parts[j % 2] = t_n if n_parts[j % 2] is None else n_parts[j % 2] + t_n
        return n_parts[0] + n_parts[1], y_parts[0] + y_parts[1]

    def finish(t, y):
        yc = y - jnp.mean(y, axis=0, keepdims=True)
        var = jnp.mean(yc * yc, axis=0, keepdims=True)
        yn = yc * lax.rsqrt(var + LNX_EPS) * tab_ref[0] + tab_ref[1]
        bonus = jnp.sum(r_ref[t] * k_ref[t] * tab_ref[2], axis=0, keepdims=True)
        y_ref[t] = yn + bonus * v_ref[t]

    def body(t, carry):
        sa, y_prev = carry
        finish(t - 1, y_prev)
        return step(t, sa, lambda j: a_ref[t + 1, pl.ds(j, 1), :])

    carry = step(0, sa_ref[...], lambda j: a_ref[1, pl.ds(j, 1), :])
    sa, y_prev = lax.fori_loop(1, tt - 1, body, carry)
    finish(tt - 2, y_prev)
    sa, y_last = step(tt - 1, sa, lambda j: an_ref[0, pl.ds(j, 1), :])
    finish(tt - 1, y_last)
    sa_ref[...] = sa


def _wkv_scan(r, w, k, v, a, b, tables):
    S, N, Ln = r.shape
    tt = min(32, S)
    seq = pl.BlockSpec((tt, N, Ln), lambda i: (i, 0, 0))
    return pl.pallas_call(
        functools.partial(_wkv_scan_kernel, tt=tt),
        name="wkv_scan",
        grid=(S // tt,),
        in_specs=[seq] * 6 + [_next_row_spec(tt, S, N, Ln),
                              pl.BlockSpec((3, N, Ln), lambda i: (0, 0, 0))],
        out_specs=seq,
        out_shape=jax.ShapeDtypeStruct((S, N, Ln), F32),
        scratch_shapes=[pltpu.VMEM((N, N, Ln), F32), pltpu.VMEM((N, Ln), F32)]
        + [pltpu.VMEM((tt, N, Ln), F32)] * 5 + [pltpu.VMEM((1, N, Ln), F32)],
        compiler_params=_cparams(("arbitrary",)),
    )(r, w, k, v, a, b, a, tables)


def _pad_cols(w, width):
    pad = width - w.shape[-1]
    return jnp.concatenate([w, jnp.zeros(w.shape[:-1] + (pad,), w.dtype)], axis=-1)


def _rope_perm():
    p = np.arange(128)
    first = (p // 16) * DIFF_DIM + p % 16
    return np.concatenate([first, first + 16])


def _permute_w_in(w_in, vmix_down):
    L, D, _ = w_in.shape
    perm = _rope_perm()
    g0 = 9 * GROUP_W
    r0 = RWKV_START
    vd = jnp.concatenate([jnp.zeros((1, D, vmix_down.shape[-1]), w_in.dtype), vmix_down], axis=0)
    cols = [
        w_in[..., perm], w_in[..., GROUP_W + perm], w_in[..., 2 * GROUP_W:3 * GROUP_W],
        w_in[..., 3 * GROUP_W:6 * GROUP_W],
        w_in[..., 6 * GROUP_W:g0], w_in[..., g0 + 8:g0 + 8 + GROUP_W], _pad_cols(w_in[..., g0:g0 + 8], 128),
        w_in[..., r0:r0 + 768], _pad_cols(w_in[..., r0 + 768:r0 + 832], 128),
        _pad_cols(w_in[..., r0 + 832:r0 + 896], 128), _pad_cols(w_in[..., r0 + 896:r0 + 1056], 256),
        _pad_cols(vd, 128),
    ]
    return jnp.concatenate(cols, axis=-1).astype(BF16)


def _permute_mu(mu, vmix_mu):
    L = mu.shape[0]
    vm = jnp.concatenate([jnp.zeros((1, vmix_mu.shape[-1]), mu.dtype), vmix_mu], axis=0)
    cols = [mu[:, 0:768], _pad_cols(mu[:, 768:832], 128), _pad_cols(mu[:, 832:896], 128),
            _pad_cols(mu[:, 896:1056], 256), _pad_cols(vm, 128)]
    return jnp.concatenate(cols, axis=-1).reshape(L, 1, RWKV_W)


def _to_lanes(x):
    B, S, _ = x.shape
    return x.reshape(B, S, N_HEADS, HEAD_DIM).transpose(1, 3, 0, 2).reshape(S, HEAD_DIM, B * N_HEADS)


def _from_lanes(y, B):
    S = y.shape[0]
    return y.reshape(S, HEAD_DIM, B, N_HEADS).transpose(2, 0, 3, 1).reshape(B, S, GROUP_W)


def _per_head_table(x, B):
    return jnp.tile(x.T, (1, B))


def _head_scalars_to_lanes(x):
    B, _, S = x.shape
    return x.transpose(2, 0, 1).reshape(S, B * N_HEADS)


def kernel(x, c, ada_w, ada_b, norm_g, ffn_w13, ffn_w2, w_in, w_out, diff_qk_g, diff_lambda, diff_subln_g, gdn_conv_w, gdn_a_log, gdn_dt_bias, gdn_norm_g, rwkv_mu, rwkv_w0, rwkv_w_up, rwkv_a0, rwkv_a_up, rwkv_g_up, rwkv_k_k, rwkv_k_a, rwkv_r_k, rwkv_lnx_w, rwkv_lnx_b, rwkv_vmix_down, rwkv_vmix_mu, rwkv_v0, rwkv_vmix_up):
    B, S, D = x.shape
    L = ada_w.shape[0]
    T = B * S

    mod = _mod_all(c, ada_w, ada_b)
    w13 = ffn_w13.astype(BF16)
    w2 = ffn_w2.astype(BF16)
    w_perm = _permute_w_in(w_in, rwkv_vmix_down)
    w_o = w_out.astype(BF16)
    mu = _permute_mu(rwkv_mu, rwkv_vmix_mu)

    inv = ROPE_THETA ** (-jnp.arange(0, DIFF_DIM, 2, dtype=F32) / DIFF_DIM)
    ang = jnp.arange(S, dtype=F32)[:, None] * inv[None, :]
    cos128 = jnp.tile(jnp.cos(ang), (1, 8))
    sin128 = jnp.tile(jnp.sin(ang), (1, 8))

    h = x.reshape(T, D)
    v_first = None
    for l in range(L):
        lam_init = 0.8 - 0.6 * math.exp(-0.3 * l)
        h = _ffn(h, mod[l], norm_g[l, 0], w13, w2, l, 0, 0, S)

        qkg = jnp.stack([jnp.tile(diff_qk_g[l, 0, :16], 8), jnp.tile(diff_qk_g[l, 0, 16:], 8),
                         jnp.tile(diff_qk_g[l, 1, :16], 8), jnp.tile(diff_qk_g[l, 1, 16:], 8)])
        attn, vt_d, vt_s, gdn, rw = _inproj(h, mod[l], norm_g[l, 1], w_perm, l, qkg,
                                            cos128, sin128, S)
        attn = attn.reshape(B, S, QK_W)
        kv_tiles = lambda t: t.reshape(B, S // KV_TILE, GROUP_W, KV_TILE)
        gdn = gdn.reshape(B, S, GDN_W)
        rw = rw.reshape(B, S, RWKV_W)

        o_a, o_b = _attention(attn, kv_tiles(vt_d), kv_tiles(vt_s), diff_lambda[l],
                              diff_subln_g[l], lam_init)

        gq, gk, gv, gab = _gdn_prep(gdn, gdn_conv_w[l], gdn_a_log[l], gdn_dt_bias[l])
        go = _gdn_scan(_to_lanes(gq), _to_lanes(gk), _to_lanes(gv),
                       _head_scalars_to_lanes(gab[:, N_HEADS:2 * N_HEADS]),
                       _head_scalars_to_lanes(gab[:, 0:N_HEADS]), gdn_norm_g[l])
        o_c = _from_lanes(go, B)

        zeros = jnp.zeros((GROUP_W,), F32)
        vecs = jnp.stack([rwkv_w0[l], rwkv_a0[l], rwkv_k_k[l], rwkv_k_a[l],
                          rwkv_v0[l - 1] if l > 0 else zeros, zeros, zeros, zeros])
        outs = _rwkv_prep(rw, mu[l], vecs, rwkv_w_up[l], rwkv_a_up[l], rwkv_g_up[l],
                          rwkv_vmix_up[l - 1] if l > 0 else None, v_first)
        rr, rw_, rk, rv, ra, rb, rg = outs[:7]
        if l == 0:
            v_first = outs[7]
        tables = jnp.stack([_per_head_table(t.reshape(N_HEADS, HEAD_DIM), B)
                            for t in (rwkv_lnx_w[l], rwkv_lnx_b[l], rwkv_r_k[l])])
        ry = _wkv_scan(*[_to_lanes(t) for t in (rr, rw_, rk, rv, ra, rb)], tables)
        o_d = _from_lanes(ry, B)

        h = _ffn(h, mod[l], norm_g[l, 2], w13, w2, l, 1, 6, S,
                 mixed=(o_a, o_b, o_c, o_d), w_out=w_o, gates=(gdn, rg))
    return h.reshape(B, S, D)
```

```python
import collections
import functools
import math

import numpy as np
import jax
import jax.numpy as jnp
from jax import lax
from jax.experimental import pallas as pl
from jax.experimental.pallas import tpu as pltpu

F32 = jnp.float32
BF16 = jnp.bfloat16
HIGHEST = lax.Precision.HIGHEST

D_MODEL = 1024
N_HEADS = 4
HEAD_DIM = 64
GROUP_W = N_HEADS * HEAD_DIM
DIFF_DIM = HEAD_DIM // 2
D_FF = 2816
N_MOD = 9
EPS = 1e-6
LNX_EPS = 64e-5
ROPE_THETA = 10000.0
LOG2E = 1.4426950408889634
CONV_K = 4
RWKV_START = 9 * GROUP_W + 2 * N_HEADS + GROUP_W

ATTN_W = 6 * GROUP_W
QK_W = 4 * GROUP_W
KV_TILE = 256
GDN_W = 3 * GROUP_W + GROUP_W + 128
RWKV_W = 3 * GROUP_W + 128 + 128 + 256 + 128
PROJ_W = ATTN_W + GDN_W + RWKV_W
VMEM_LIMIT = 56 * 1024 * 1024


def _cparams(sem):
    return pltpu.CompilerParams(dimension_semantics=sem, vmem_limit_bytes=VMEM_LIMIT)


def _sigmoid(x):
    return 1.0 / (1.0 + jnp.exp(-x))


def _softplus(x):
    return jnp.maximum(x, 0.0) + jnp.log(1.0 + jnp.exp(-jnp.abs(x)))


def _group_ones(n, group):
    r = lax.broadcasted_iota(jnp.int32, (n, n), 0) // group
    c = lax.broadcasted_iota(jnp.int32, (n, n), 1) // group
    return (r == c).astype(BF16)


def _group_sum(x, group):
    ones = _group_ones(x.shape[-1], group)
    hi = x.astype(BF16)
    lo = (x - hi.astype(F32)).astype(BF16)
    return (jnp.dot(hi, ones, preferred_element_type=F32)
            + jnp.dot(lo, ones, preferred_element_type=F32))


def _modulate(x, gain, shift, scale):
    ms = jnp.mean(x * x, axis=-1, keepdims=True)
    return (x * lax.rsqrt(ms + EPS) * gain) * (1.0 + scale) + shift


def _mod_kernel(c_ref, w_ref, b_ref, o_ref):
    c = c_ref[...]
    cond = c * _sigmoid(c)
    o_ref[...] = jnp.dot(cond, w_ref[...], precision=HIGHEST,
                         preferred_element_type=F32) + b_ref[...]


def _mod_all(c, ada_w, ada_b):
    L, D, N = ada_w.shape
    B = c.shape[0]
    tn = 1024
    out = pl.pallas_call(
        _mod_kernel,
        name="adaln_mod",
        grid=(L, N // tn),
        in_specs=[
            pl.BlockSpec((B, D), lambda l, n: (0, 0)),
            pl.BlockSpec((None, D, tn), lambda l, n: (l, 0, n)),
            pl.BlockSpec((None, 1, tn), lambda l, n: (l, 0, n)),
        ],
        out_specs=pl.BlockSpec((None, B, tn), lambda l, n: (l, 0, n)),
        out_shape=jax.ShapeDtypeStruct((L, B, N), F32),
        compiler_params=_cparams(("parallel", "parallel")),
    )(c, ada_w, ada_b.reshape(L, 1, N))
    return out.reshape(L, B, N_MOD, D)


def _ffn_kernel(*refs, mb, tf, mix):
    x_ref, mod_ref, g_ref, w13_ref, w2_ref = refs[:5]
    o_ref, act_ref = refs[-2:]
    x = x_ref[...]
    if mix:
        oa_ref, ob_ref, oc_ref, od_ref, w_ref, gc_ref, gd_ref = refs[5:12]
        gc = gc_ref[...]
        o_c = (oc_ref[...].astype(F32) * (gc * _sigmoid(gc))).astype(BF16)
        o_d = (od_ref[...].astype(F32) * gd_ref[...]).astype(BF16)
        proj = jnp.dot(oa_ref[...], w_ref[0:GROUP_W, :], preferred_element_type=F32)
        for g, o_g in ((1, ob_ref[...]), (2, o_c), (3, o_d)):
            proj += jnp.dot(o_g, w_ref[g * GROUP_W:(g + 1) * GROUP_W, :],
                            preferred_element_type=F32)
        x = x + mod_ref[5:6, :] * proj
    hn = _modulate(x, g_ref[...], mod_ref[mb:mb + 1, :], mod_ref[mb + 1:mb + 2, :]).astype(BF16)
    for f in range(D_FF // tf):
        gate = jnp.dot(hn, w13_ref[:, f * tf:(f + 1) * tf], preferred_element_type=F32)
        up = jnp.dot(hn, w13_ref[:, D_FF + f * tf:D_FF + (f + 1) * tf], preferred_element_type=F32)
        act_ref[:, f * tf:(f + 1) * tf] = (gate * _sigmoid(gate) * up).astype(BF16)
    acc = jnp.dot(act_ref[...], w2_ref[...], preferred_element_type=F32)
    o_ref[...] = x + 0.5 * mod_ref[mb + 2:mb + 3, :] * acc


def _ffn(h, mod_l, gain, w13, w2, l, j, mb, S, mixed=None, w_out=None, gates=None):
    T, D = h.shape
    tm = min(1024, S)
    tf = 256
    tpb = S // tm
    resident = dict(pipeline_mode=pl.Buffered(1))
    in_specs = [
        pl.BlockSpec((tm, D), lambda i: (i, 0)),
        pl.BlockSpec((None, N_MOD, D), lambda i: (i // tpb, 0, 0)),
        pl.BlockSpec((1, D), lambda i: (0, 0)),
        pl.BlockSpec((None, None, D, 2 * D_FF), lambda i: (l, j, 0, 0), **resident),
        pl.BlockSpec((None, None, D_FF, D), lambda i: (l, j, 0, 0), **resident),
    ]
    args = [h, mod_l, gain.reshape(1, D), w13, w2]
    if mixed is not None:
        in_specs += [pl.BlockSpec((None, tm, GROUP_W), lambda i: (i // tpb, i % tpb, 0))] * 4
        in_specs += [pl.BlockSpec((None, D, D), lambda i: (l, 0, 0), **resident)]
        in_specs += [pl.BlockSpec((None, tm, GROUP_W), lambda i: (i // tpb, i % tpb, 3)),
                     pl.BlockSpec((None, tm, GROUP_W), lambda i: (i // tpb, i % tpb, 0))]
        args += list(mixed) + [w_out] + list(gates)
    return pl.pallas_call(
        functools.partial(_ffn_kernel, mb=mb, tf=tf, mix=mixed is not None),
        name="ffn",
        grid=(T // tm,),
        in_specs=in_specs,
        out_specs=pl.BlockSpec((tm, D), lambda i: (i, 0)),
        out_shape=jax.ShapeDtypeStruct((T, D), F32),
        scratch_shapes=[pltpu.VMEM((tm, D_FF), BF16)],
        compiler_params=_cparams(("parallel",)),
    )(*args)


def _inproj_kernel(x_ref, mod_ref, g_ref, w_ref, qkg_ref, cos_ref, sin_ref,
                   attn_ref, vtd_ref, vts_ref, gdn_ref, rwkv_ref):
    hn = _modulate(x_ref[...], g_ref[...], mod_ref[3:4, :], mod_ref[4:5, :]).astype(BF16)

    pa = jnp.dot(hn, w_ref[:, 0:3 * GROUP_W], preferred_element_type=F32)
    cos = cos_ref[...]
    sin = sin_ref[...]
    for idx, scale in ((0, DIFF_DIM ** -0.5 * LOG2E), (1, 1.0)):
        x1 = pa[:, idx * 256:idx * 256 + 128]
        x2 = pa[:, idx * 256 + 128:idx * 256 + 256]
        ms = _group_sum(x1 * x1 + x2 * x2, 16) * (1.0 / DIFF_DIM)
        inv = lax.rsqrt(ms + EPS)
        n1 = x1 * inv * qkg_ref[2 * idx:2 * idx + 1, :]
        n2 = x2 * inv * qkg_ref[2 * idx + 1:2 * idx + 2, :]
        attn_ref[:, idx * 256:idx * 256 + 128] = ((n1 * cos - n2 * sin) * scale).astype(BF16)
        attn_ref[:, idx * 256 + 128:idx * 256 + 256] = ((n2 * cos + n1 * sin) * scale).astype(BF16)
    n_kv = x_ref.shape[0] // KV_TILE
    for c in range(n_kv):
        vtd_ref[c] = pa[c * KV_TILE:(c + 1) * KV_TILE, 512:768].T.astype(BF16)

    pb = jnp.dot(hn, w_ref[:, 3 * GROUP_W:ATTN_W], preferred_element_type=F32)
    attn_ref[:, 512:768] = (pb[:, 0:256] * (HEAD_DIM ** -0.5)).astype(BF16)
    attn_ref[:, 768:1024] = pb[:, 256:512].astype(BF16)
    for c in range(n_kv):
        vts_ref[c] = pb[c * KV_TILE:(c + 1) * KV_TILE, 512:768].T.astype(BF16)

    gdn_ref[...] = jnp.dot(hn, w_ref[:, ATTN_W:ATTN_W + GDN_W], preferred_element_type=F32)
    rwkv_ref[...] = jnp.dot(hn, w_ref[:, ATTN_W + GDN_W:PROJ_W], preferred_element_type=F32)


def _inproj(h, mod_l, gain, w_perm, l, qkg, cos128, sin128, S):
    T, D = h.shape
    tm = min(512, S)
    tpb = S // tm
    vt_spec = pl.BlockSpec((tm // KV_TILE, GROUP_W, KV_TILE), lambda i: (i, 0, 0))
    vt_shape = jax.ShapeDtypeStruct((T // KV_TILE, GROUP_W, KV_TILE), BF16)
    return pl.pallas_call(
        _inproj_kernel,
        name="inproj",
        grid=(T // tm,),
        in_specs=[
            pl.BlockSpec((tm, D), lambda i: (i, 0)),
            pl.BlockSpec((None, N_MOD, D), lambda i: (i // tpb, 0, 0)),
            pl.BlockSpec((1, D), lambda i: (0, 0)),
            pl.BlockSpec((None, D, PROJ_W), lambda i: (l, 0, 0)),
            pl.BlockSpec((4, 128), lambda i: (0, 0)),
            pl.BlockSpec((tm, 128), lambda i: (i % tpb, 0)),
            pl.BlockSpec((tm, 128), lambda i: (i % tpb, 0)),
        ],
        out_specs=[
            pl.BlockSpec((tm, QK_W), lambda i: (i, 0)),
            vt_spec, vt_spec,
            pl.BlockSpec((tm, GDN_W), lambda i: (i, 0)),
            pl.BlockSpec((tm, RWKV_W), lambda i: (i, 0)),
        ],
        out_shape=[
            jax.ShapeDtypeStruct((T, QK_W), BF16),
            vt_shape, vt_shape,
            jax.ShapeDtypeStruct((T, GDN_W), F32),
            jax.ShapeDtypeStruct((T, RWKV_W), F32),
        ],
        compiler_params=_cparams(("parallel",)),
    )(h, mod_l, gain.reshape(1, D), w_perm, qkg, cos128, sin128)


def _attn_kernel(qd_ref, kd_ref, vtd_ref, qs_ref, ks_ref, vts_ref, lamv_ref, g_ref,
                 od_ref, os_ref, qc_ref, qh_ref, tri_ref, accd_ref, accs_ref, *, lam_init, t):
    i = pl.program_id(1)
    heads = range(N_HEADS)
    lane = lax.broadcasted_iota(jnp.int32, (1, GROUP_W), 1)
    r_i = lax.broadcasted_iota(jnp.int32, (t, t), 0)
    c_i = lax.broadcasted_iota(jnp.int32, (t, t), 1)
    tri_ref[...] = (c_i > r_i).astype(BF16)
    qd = qd_ref[...]
    qs = qs_ref[...]
    sub = (lane % 128) // 16
    for h in heads:
        qc_ref[h, 0:t, :] = jnp.where(sub == 2 * h, qd, jnp.zeros_like(qd))
        qc_ref[h, t:2 * t, :] = jnp.where(sub == 2 * h + 1, qd, jnp.zeros_like(qd))
        qh_ref[h] = jnp.where(lane // HEAD_DIM == h, qs, jnp.zeros_like(qs))
    accd_ref[...] = jnp.zeros_like(accd_ref)
    accs_ref[...] = jnp.zeros_like(accs_ref)

    def tile(kt, carry, diag):
        stats, keep = carry
        start = pl.multiple_of(kt * t, t)
        kd = kd_ref[pl.ds(start, t), :]
        ks = ks_ref[pl.ds(start, t), :]
        nt = (((1,), (1,)), ((), ()))
        ss = [lax.dot_general(kd, qc_ref[h], nt, preferred_element_type=F32) for h in heads]
        zs = [lax.dot_general(ks, qh_ref[h], nt, preferred_element_type=F32) for h in heads]
        new_stats, ps, alphas = [], [], []
        for h in heads:
            m, l = stats[h]
            s = ss[h]
            if diag:
                allowed = jnp.concatenate([r_i <= c_i, r_i <= c_i], axis=1)
                s = jnp.where(allowed, s, -jnp.inf)
            m_new = jnp.maximum(m, jnp.max(s, axis=0, keepdims=True))
            alpha = jnp.exp2(m - m_new)
            p = jnp.exp2(s - m_new)
            new_stats.append((m_new, alpha * l + jnp.sum(p, axis=0, keepdims=True)))
            ps.append(p.astype(BF16))
            alphas.append(alpha)
        strict = r_i < c_i
        lss, lks, sums = [], [], []
        for h in heads:
            z = zs[h]
            neg_abs = lax.bitcast_convert_type(
                lax.bitcast_convert_type(z, jnp.uint32) | jnp.uint32(0x80000000), F32)
            ls = jnp.minimum(z, 0.0) - jnp.log(1.0 + jnp.exp(neg_abs))
            lk = ls - z
            if diag:
                lk = jnp.where(strict, lk, 0.0)
            lss.append(ls)
            lks.append(lk.astype(BF16))
            sums.append(jnp.sum(lk, axis=0, keepdims=True))
        bts = [jnp.dot(tri_ref[...], lks[h], preferred_element_type=F32) for h in heads]
        for h in heads:
            vt = vtd_ref[kt, h * HEAD_DIM:(h + 1) * HEAD_DIM, :]
            accd_ref[h] = alphas[h] * accd_ref[h] + jnp.dot(vt, ps[h], preferred_element_type=F32)
        pas = []
        for h in heads:
            a = jnp.exp(lss[h] + (bts[h] + keep[h]))
            if diag:
                a = jnp.where(strict, a, 0.0)
            pas.append(a.astype(BF16))
        for h in heads:
            vt = vts_ref[kt, h * HEAD_DIM:(h + 1) * HEAD_DIM, :]
            accs_ref[h] += jnp.dot(vt, pas[h], preferred_element_type=F32)
        return tuple(new_stats), tuple(keep[h] + sums[h] for h in heads)

    init = (tuple((jnp.full((1, 2 * t), -jnp.inf, F32), jnp.zeros((1, 2 * t), F32)) for _ in heads),
            tuple(jnp.zeros((1, t), F32) for _ in heads))
    carry = tile(i, init, True)
    stats, _ = lax.fori_loop(0, i, lambda n, c: tile(i - 1 - n, c, False), carry)

    lv = lamv_ref[...]
    lam = (jnp.exp(jnp.sum(lv[0:1, :] * lv[1:2, :], axis=1, keepdims=True))
           - jnp.exp(jnp.sum(lv[2:3, :] * lv[3:4, :], axis=1, keepdims=True)) + lam_init)
    rows = []
    for h in heads:
        o = accd_ref[h] / stats[h][1]
        oh = o[:, :t] - lam * o[:, t:]
        ms = jnp.mean(oh * oh, axis=0, keepdims=True)
        rows.append(oh * lax.rsqrt(ms + EPS) * g_ref[h * HEAD_DIM:(h + 1) * HEAD_DIM, :]
                    * (1.0 - lam_init))
    od_ref[...] = jnp.concatenate(rows, axis=0).T.astype(BF16)
    os_ref[...] = jnp.concatenate([accs_ref[h] for h in heads], axis=0).T.astype(BF16)


def _attention(attn, vt_d, vt_s, lamv, subln_g, lam_init):
    B, S, _ = attn.shape
    t = KV_TILE
    gain = jnp.broadcast_to(jnp.tile(subln_g, N_HEADS)[:, None], (GROUP_W, t))
    q_spec = lambda c: pl.BlockSpec((None, t, GROUP_W), lambda b, i: (b, i, c))
    k_spec = lambda c: pl.BlockSpec((None, S, GROUP_W), lambda b, i: (b, 0, c))
    vt_spec = pl.BlockSpec((None, S // t, GROUP_W, t), lambda b, i: (b, 0, 0, 0))
    out = jax.ShapeDtypeStruct((B, S, GROUP_W), BF16)
    return pl.pallas_call(
        functools.partial(_attn_kernel, lam_init=lam_init, t=t),
        name="attn",
        grid=(B, S // t),
        in_specs=[q_spec(0), k_spec(1), vt_spec, q_spec(2), k_spec(3), vt_spec,
                  pl.BlockSpec((4, DIFF_DIM), lambda b, i: (0, 0)),
                  pl.BlockSpec((GROUP_W, t), lambda b, i: (0, 0))],
        out_specs=[q_spec(0), q_spec(0)],
        out_shape=[out, out],
        scratch_shapes=[pltpu.VMEM((N_HEADS, 2 * t, GROUP_W), BF16),
                        pltpu.VMEM((N_HEADS, t, GROUP_W), BF16),
                        pltpu.VMEM((t, t), BF16),
                        pltpu.VMEM((N_HEADS, HEAD_DIM, 2 * t), F32),
                        pltpu.VMEM((N_HEADS, HEAD_DIM, t), F32)],
        compiler_params=_cparams(("parallel", "parallel")),
    )(attn, attn, vt_d, attn, attn, vt_s, lamv, gain)


def _gdn_prep_kernel(x_ref, prev_ref, small_ref, cw_ref, alog_ref, dtb_ref,
                     q_ref, k_ref, v_ref, ab_ref, xs_ref, *, ts):
    i = pl.program_id(1)
    prev = prev_ref[...]
    xs_ref[0:8, :] = jnp.where(i == 0, jnp.zeros_like(prev), prev)
    xs_ref[8:8 + ts, :] = x_ref[...]
    conv = jnp.zeros((ts, 3 * GROUP_W), F32)
    for j in range(CONV_K):
        conv = conv + xs_ref[pl.ds(8 - (CONV_K - 1) + j, ts), :] * cw_ref[j:j + 1, :]
    y = conv * _sigmoid(conv)
    q = y[:, 0:GROUP_W]
    k = y[:, GROUP_W:2 * GROUP_W]
    q_ref[...] = (q * lax.rsqrt(_group_sum(q * q, HEAD_DIM) + EPS) * (HEAD_DIM ** -0.5)).astype(BF16)
    k_ref[...] = (k * lax.rsqrt(_group_sum(k * k, HEAD_DIM) + EPS)).astype(BF16)
    v_ref[...] = y[:, 2 * GROUP_W:3 * GROUP_W].astype(BF16)
    small = small_ref[...]
    beta = _sigmoid(small)
    alpha = jnp.exp(-jnp.exp(alog_ref[...]) * _softplus(small + dtb_ref[...]))
    lane = lax.broadcasted_iota(jnp.int32, (1, 128), 1)
    ab_ref[...] = jnp.where(lane < N_HEADS, beta, alpha).T[0:2 * N_HEADS, :]


def _gdn_prep(gdn, conv_w, a_log, dt_bias):
    B, S, _ = gdn.shape
    ts = min(512, S)
    pad = jnp.zeros((N_HEADS,), F32)
    alog = jnp.concatenate([pad, a_log, jnp.zeros((120,), F32)]).reshape(1, 128)
    dtb = jnp.concatenate([pad, dt_bias, jnp.zeros((120,), F32)]).reshape(1, 128)
    tok = jax.ShapeDtypeStruct((B, S, GROUP_W), F32)
    return pl.pallas_call(
        functools.partial(_gdn_prep_kernel, ts=ts),
        name="gdn_prep",
        grid=(B, S // ts),
        in_specs=[
            pl.BlockSpec((None, ts, 3 * GROUP_W), lambda b, i: (b, i, 0)),
            pl.BlockSpec((None, 8, 3 * GROUP_W), lambda b, i: (b, jnp.maximum(i * (ts // 8) - 1, 0), 0)),
            pl.BlockSpec((None, ts, 128), lambda b, i: (b, i, 8)),
            pl.BlockSpec((CONV_K, 3 * GROUP_W), lambda b, i: (0, 0)),
            pl.BlockSpec((1, 128), lambda b, i: (0, 0)),
            pl.BlockSpec((1, 128), lambda b, i: (0, 0)),
        ],
        out_specs=[
            pl.BlockSpec((None, ts, GROUP_W), lambda b, i: (b, i, 0)),
            pl.BlockSpec((None, ts, GROUP_W), lambda b, i: (b, i, 0)),
            pl.BlockSpec((None, ts, GROUP_W), lambda b, i: (b, i, 0)),
            pl.BlockSpec((None, 2 * N_HEADS, ts), lambda b, i: (b, 0, i)),
        ],
        out_shape=[tok.update(dtype=BF16)] * 3 + [jax.ShapeDtypeStruct((B, 2 * N_HEADS, S), F32)],
        scratch_shapes=[pltpu.VMEM((ts + 8, 3 * GROUP_W), F32)],
        compiler_params=_cparams(("parallel", "parallel")),
    )(gdn, gdn, gdn, conv_w, alog, dtb)


def _gdn_parts(qb_ref, kb_ref, vb_ref, al_ref, be_ref, knb_ref, ng_ref, o_ref,
               s_ref, ks_ref, q_ref, k_ref, v_ref, kn_ref):
    @pl.when(pl.program_id(0) == 0)
    def _():
        s_ref[...] = jnp.zeros_like(s_ref)
        ks_ref[...] = jnp.zeros_like(ks_ref)

    q_ref[...] = qb_ref[...].astype(F32)
    k_ref[...] = kb_ref[...].astype(F32)
    v_ref[...] = vb_ref[...].astype(F32)
    kn_ref[...] = knb_ref[...].astype(F32)

    def step(t, ks, k_next):
        al = al_ref[pl.ds(t, 1), :]
        d = be_ref[pl.ds(t, 1), :] * (v_ref[t] - al * ks)
        o_parts = [None] * 2
        n_parts = [None] * 2
        for kk in range(HEAD_DIM):
            s_new = al * s_ref[kk] + k_ref[t, pl.ds(kk, 1), :] * d
            s_ref[kk] = s_new
            t_o = s_new * q_ref[t, pl.ds(kk, 1), :]
            t_n = s_new * k_next(kk)
            o_parts[kk % 2] = t_o if o_parts[kk % 2] is None else o_parts[kk % 2] + t_o
            n_parts[kk % 2] = t_n if n_parts[kk % 2] is None else n_parts[kk % 2] + t_n
        return n_parts[0] + n_parts[1], o_parts[0] + o_parts[1]

    def finish(t, o):
        ms = jnp.mean(o * o, axis=0, keepdims=True)
        o_ref[t] = o * lax.rsqrt(ms + EPS) * ng_ref[...]

    return _ScanParts(step, finish, lambda t: (lambda kk: k_ref[t + 1, pl.ds(kk, 1), :]),
                      lambda kk: kn_ref[0, pl.ds(kk, 1), :], ks_ref)


def _next_row_spec(tt, S, N, Ln):
    return pl.BlockSpec((1, N, Ln), lambda i: (jnp.minimum((i + 1) * tt, S - 1), 0, 0))


def _rwkv_prep_kernel(*refs, ts, first):
    if first:
        (x_ref, prev_ref, mu_ref, vecs_ref, wup_ref, aup_ref, gup_ref,
         r_ref, w_ref, k_ref, v_ref, a_ref, b_ref, g_ref, vf_ref, xs_ref) = refs
    else:
        (x_ref, prev_ref, mu_ref, vecs_ref, wup_ref, aup_ref, gup_ref, vup_ref, vfin_ref,
         r_ref, w_ref, k_ref, v_ref, a_ref, b_ref, g_ref, xs_ref) = refs
    i = pl.program_id(1)
    prev = prev_ref[...]
    xs_ref[0:8, :] = jnp.where(i == 0, jnp.zeros_like(prev), prev)
    x = x_ref[...]
    xs_ref[8:8 + ts, :] = x
    xm = x + (xs_ref[pl.ds(7, ts), :] - x) * mu_ref[...]
    r = xm[:, 0:256]
    k = xm[:, 256:512]
    v = xm[:, 512:768]
    wd = xm[:, 768:896]
    ad = xm[:, 896:1024]
    gd = xm[:, 1024:1280]
    w0, a0, k_k, k_a = vecs_ref[0:1, :], vecs_ref[1:2, :], vecs_ref[2:3, :], vecs_ref[3:4, :]
    dot = lambda x, w: jnp.dot(x.astype(BF16), w, preferred_element_type=F32)
    w_log = -_softplus(-(w0 + dot(jnp.tanh(wd), wup_ref[...]))) - 0.5
    a = _sigmoid(a0 + dot(ad, aup_ref[...]))
    g_ref[...] = dot(_sigmoid(gd), gup_ref[...])
    if first:
        vf_ref[...] = v
    else:
        vd = xm[:, 1280:1408]
        v = v + (vfin_ref[...] - v) * _sigmoid(vecs_ref[4:5, :] + dot(vd, vup_ref[...]))
    kk = k * k_k
    kk = kk * lax.rsqrt(_group_sum(kk * kk, HEAD_DIM) + EPS)
    r_ref[...] = r.astype(BF16)
    w_ref[...] = jnp.exp(-jnp.exp(w_log))
    k_ref[...] = (k * (1.0 + (a - 1.0) * k_a)).astype(BF16)
    v_ref[...] = v.astype(BF16)
    a_ref[...] = (-kk).astype(BF16)
    b_ref[...] = (kk * a).astype(BF16)


def _pad_rows(w, rows):
    return jnp.concatenate([w, jnp.zeros((rows - w.shape[0], w.shape[1]), w.dtype)],
                           axis=0).astype(BF16)


def _rwkv_prep(rw, mu, vecs, w_up, a_up, g_up, vmix_up, v_first):
    B, S, _ = rw.shape
    ts = min(512, S)
    first = v_first is None
    tok_spec = pl.BlockSpec((None, ts, GROUP_W), lambda b, i: (b, i, 0))
    const = lambda shape: pl.BlockSpec(shape, lambda b, i: (0,) * len(shape))
    in_specs = [
        pl.BlockSpec((None, ts, RWKV_W), lambda b, i: (b, i, 0)),
        pl.BlockSpec((None, 8, RWKV_W), lambda b, i: (b, jnp.maximum(i * (ts // 8) - 1, 0), 0)),
        const((1, RWKV_W)), const((8, GROUP_W)),
        const((128, GROUP_W)), const((128, GROUP_W)), const((256, GROUP_W)),
    ]
    args = [rw, rw, mu, vecs, _pad_rows(w_up, 128), _pad_rows(a_up, 128), _pad_rows(g_up, 256)]
    if not first:
        in_specs += [const((128, GROUP_W)), tok_spec]
        args += [_pad_rows(vmix_up, 128), v_first]
    tok = jax.ShapeDtypeStruct((B, S, GROUP_W), F32)
    n_out = 8 if first else 7
    return pl.pallas_call(
        functools.partial(_rwkv_prep_kernel, ts=ts, first=first),
        name="rwkv_prep",
        grid=(B, S // ts),
        in_specs=in_specs,
        out_specs=[tok_spec] * n_out,
        out_shape=[tok.update(dtype=BF16 if n in (0, 2, 3, 4, 5) else F32) for n in range(n_out)],
        scratch_shapes=[pltpu.VMEM((ts + 8, RWKV_W), F32)],
        compiler_params=_cparams(("parallel", "parallel")),
    )(*args)


def _wkv_parts(rb_ref, w_ref, kb_ref, vb_ref, ab_ref, bb_ref, anb_ref, tab_ref, y_ref,
               s_ref, sa_ref, r_ref, k_ref, v_ref, a_ref, b_ref, an_ref):
    @pl.when(pl.program_id(0) == 0)
    def _():
        s_ref[...] = jnp.zeros_like(s_ref)
        sa_ref[...] = jnp.zeros_like(sa_ref)

    for dst, src in ((r_ref, rb_ref), (k_ref, kb_ref), (v_ref, vb_ref), (a_ref, ab_ref),
                     (b_ref, bb_ref), (an_ref, anb_ref)):
        dst[...] = src[...].astype(F32)

    def step(t, sa, a_next):
        vt = v_ref[t]
        y_parts = [None] * 2
        n_parts = [None] * 2
        for j in range(HEAD_DIM):
            s_new = (s_ref[j] * w_ref[t, pl.ds(j, 1), :]
                     + (sa * b_ref[t, pl.ds(j, 1), :] + vt * k_ref[t, pl.ds(j, 1), :]))
            s_ref[j] = s_new
            t_y = s_new * r_ref[t, pl.ds(j, 1), :]
            t_n = s_new * a_next(j)
            y_parts[j % 2] = t_y if y_parts[j % 2] is None else y_parts[j % 2] + t_y
            n_parts[j % 2] = t_n if n_parts[j % 2] is None else n_parts[j % 2] + t_n
        return n_parts[0] + n_parts[1], y_parts[0] + y_parts[1]

    def finish(t, y):
        yc = y - jnp.mean(y, axis=0, keepdims=True)
        var = jnp.mean(yc * yc, axis=0, keepdims=True)
        yn = yc * lax.rsqrt(var + LNX_EPS) * tab_ref[0] + tab_ref[1]
        bonus = jnp.sum(r_ref[t] * k_ref[t] * tab_ref[2], axis=0, keepdims=True)
        y_ref[t] = yn + bonus * v_ref[t]

    return _ScanParts(step, finish, lambda t: (lambda j: a_ref[t + 1, pl.ds(j, 1), :]),
                      lambda j: an_ref[0, pl.ds(j, 1), :], sa_ref)


_ScanParts = collections.namedtuple("_ScanParts", "step finish next_in_block next_block carry_ref")


def _scans_kernel(*refs, tt):
    gdn = _gdn_parts(*refs[0:7], refs[15], *refs[17:23])
    wkv = _wkv_parts(*refs[7:15], refs[16], *refs[23:31])
    parts = (gdn, wkv)

    def body(t, carries):
        for p, c in zip(parts, carries):
            p.finish(t - 1, c[1])
        return tuple(p.step(t, c[0], p.next_in_block(t)) for p, c in zip(parts, carries))

    carries = tuple(p.step(0, p.carry_ref[...], p.next_in_block(0)) for p in parts)
    carries = lax.fori_loop(1, tt - 1, body, carries)
    for p, c in zip(parts, carries):
        p.finish(tt - 2, c[1])
    carries = tuple(p.step(tt - 1, c[0], p.next_block) for p, c in zip(parts, carries))
    for p, c in zip(parts, carries):
        p.finish(tt - 1, c[1])
        p.carry_ref[...] = c[0]


def _scans(gq, gk, gv, al, be, norm_g, r, w, k, v, a, b, tables):
    S, N, Ln = gq.shape
    tt = min(32, S)
    seq = pl.BlockSpec((tt, N, Ln), lambda i: (i, 0, 0))
    vec = pl.BlockSpec((tt, Ln), lambda i: (i, 0))
    nxt = _next_row_spec(tt, S, N, Ln)
    out = jax.ShapeDtypeStruct((S, N, Ln), F32)
    state = [pltpu.VMEM((N, N, Ln), F32), pltpu.VMEM((N, Ln), F32)]
    wide = pltpu.VMEM((tt, N, Ln), F32)
    wide_row = pltpu.VMEM((1, N, Ln), F32)
    return pl.pallas_call(
        functools.partial(_scans_kernel, tt=tt),
        name="scans",
        grid=(S // tt,),
        in_specs=[seq, seq, seq, vec, vec, nxt, pl.BlockSpec((N, Ln), lambda i: (0, 0))]
        + [seq] * 6 + [nxt, pl.BlockSpec((3, N, Ln), lambda i: (0, 0, 0))],
        out_specs=[seq, seq],
        out_shape=[out, out],
        scratch_shapes=state + [wide] * 3 + [wide_row] + state + [wide] * 5 + [wide_row],
        compiler_params=_cparams(("arbitrary",)),
    )(gq, gk, gv, al, be, gk, jnp.broadcast_to(norm_g[:, None], (N, Ln)),
      r, w, k, v, a, b, a, tables)


def _pad_cols(w, width):
    pad = width - w.shape[-1]
    return jnp.concatenate([w, jnp.zeros(w.shape[:-1] + (pad,), w.dtype)], axis=-1)


def _rope_perm():
    p = np.arange(128)
    first = (p // 16) * DIFF_DIM + p % 16
    return np.concatenate([first, first + 16])


def _permute_w_in(w_in, vmix_down):
    L, D, _ = w_in.shape
    perm = _rope_perm()
    g0 = 9 * GROUP_W
    r0 = RWKV_START
    vd = jnp.concatenate([jnp.zeros((1, D, vmix_down.shape[-1]), w_in.dtype), vmix_down], axis=0)
    cols = [
        w_in[..., perm], w_in[..., GROUP_W + perm], w_in[..., 2 * GROUP_W:3 * GROUP_W],
        w_in[..., 3 * GROUP_W:6 * GROUP_W],
        w_in[..., 6 * GROUP_W:g0], w_in[..., g0 + 8:g0 + 8 + GROUP_W], _pad_cols(w_in[..., g0:g0 + 8], 128),
        w_in[..., r0:r0 + 768], _pad_cols(w_in[..., r0 + 768:r0 + 832], 128),
        _pad_cols(w_in[..., r0 + 832:r0 + 896], 128), _pad_cols(w_in[..., r0 + 896:r0 + 1056], 256),
        _pad_cols(vd, 128),
    ]
    return jnp.concatenate(cols, axis=-1).astype(BF16)


def _permute_mu(mu, vmix_mu):
    L = mu.shape[0]
    vm = jnp.concatenate([jnp.zeros((1, vmix_mu.shape[-1]), mu.dtype), vmix_mu], axis=0)
    cols = [mu[:, 0:768], _pad_cols(mu[:, 768:832], 128), _pad_cols(mu[:, 832:896], 128),
            _pad_cols(mu[:, 896:1056], 256), _pad_cols(vm, 128)]
    return jnp.concatenate(cols, axis=-1).reshape(L, 1, RWKV_W)


def _to_lanes(x):
    B, S, _ = x.shape
    return x.reshape(B, S, N_HEADS, HEAD_DIM).transpose(1, 3, 0, 2).reshape(S, HEAD_DIM, B * N_HEADS)


def _from_lanes(y, B):
    S = y.shape[0]
    return y.reshape(S, HEAD_DIM, B, N_HEADS).transpose(2, 0, 3, 1).reshape(B, S, GROUP_W)


def _per_head_table(x, B):
    return jnp.tile(x.T, (1, B))


def _head_scalars_to_lanes(x):
    B, _, S = x.shape
    return x.transpose(2, 0, 1).reshape(S, B * N_HEADS)


def kernel(x, c, ada_w, ada_b, norm_g, ffn_w13, ffn_w2, w_in, w_out, diff_qk_g, diff_lambda, diff_subln_g, gdn_conv_w, gdn_a_log, gdn_dt_bias, gdn_norm_g, rwkv_mu, rwkv_w0, rwkv_w_up, rwkv_a0, rwkv_a_up, rwkv_g_up, rwkv_k_k, rwkv_k_a, rwkv_r_k, rwkv_lnx_w, rwkv_lnx_b, rwkv_vmix_down, rwkv_vmix_mu, rwkv_v0, rwkv_vmix_up):
    B, S, D = x.shape
    L = ada_w.shape[0]
    T = B * S

    mod = _mod_all(c, ada_w, ada_b)
    w13 = ffn_w13.astype(BF16)
    w2 = ffn_w2.astype(BF16)
    w_perm = _permute_w_in(w_in, rwkv_vmix_down)
    w_o = w_out.astype(BF16)
    mu = _permute_mu(rwkv_mu, rwkv_vmix_mu)

    inv = ROPE_THETA ** (-jnp.arange(0, DIFF_DIM, 2, dtype=F32) / DIFF_DIM)
    ang = jnp.arange(S, dtype=F32)[:, None] * inv[None, :]
    cos128 = jnp.tile(jnp.cos(ang), (1, 8))
    sin128 = jnp.tile(jnp.sin(ang), (1, 8))

    h = x.reshape(T, D)
    v_first = None
    for l in range(L):
        lam_init = 0.8 - 0.6 * math.exp(-0.3 * l)
        h = _ffn(h, mod[l], norm_g[l, 0], w13, w2, l, 0, 0, S)

        qkg = jnp.stack([jnp.tile(diff_qk_g[l, 0, :16], 8), jnp.tile(diff_qk_g[l, 0, 16:], 8),
                         jnp.tile(diff_qk_g[l, 1, :16], 8), jnp.tile(diff_qk_g[l, 1, 16:], 8)])
        attn, vt_d, vt_s, gdn, rw = _inproj(h, mod[l], norm_g[l, 1], w_perm, l, qkg,
                                            cos128, sin128, S)
        attn = attn.reshape(B, S, QK_W)
        kv_tiles = lambda t: t.reshape(B, S // KV_TILE, GROUP_W, KV_TILE)
        gdn = gdn.reshape(B, S, GDN_W)
        rw = rw.reshape(B, S, RWKV_W)

        o_a, o_b = _attention(attn, kv_tiles(vt_d), kv_tiles(vt_s), diff_lambda[l],
                              diff_subln_g[l], lam_init)

        gq, gk, gv, gab = _gdn_prep(gdn, gdn_conv_w[l], gdn_a_log[l], gdn_dt_bias[l])

        zeros = jnp.zeros((GROUP_W,), F32)
        vecs = jnp.stack([rwkv_w0[l], rwkv_a0[l], rwkv_k_k[l], rwkv_k_a[l],
                          rwkv_v0[l - 1] if l > 0 else zeros, zeros, zeros, zeros])
        outs = _rwkv_prep(rw, mu[l], vecs, rwkv_w_up[l], rwkv_a_up[l], rwkv_g_up[l],
                          rwkv_vmix_up[l - 1] if l > 0 else None, v_first)
        rr, rw_, rk, rv, ra, rb, rg = outs[:7]
        if l == 0:
            v_first = outs[7]
        tables = jnp.stack([_per_head_table(t.reshape(N_HEADS, HEAD_DIM), B)
                            for t in (rwkv_lnx_w[l], rwkv_lnx_b[l], rwkv_r_k[l])])
        go, ry = _scans(_to_lanes(gq), _to_lanes(gk), _to_lanes(gv),
                        _head_scalars_to_lanes(gab[:, N_HEADS:2 * N_HEADS]),
                        _head_scalars_to_lanes(gab[:, 0:N_HEADS]), gdn_norm_g[l],
                        *[_to_lanes(t) for t in (rr, rw_, rk, rv, ra, rb)], tables)
        o_c = _from_lanes(go, B)
        o_d = _from_lanes(ry, B)

        h = _ffn(h, mod[l], norm_g[l, 2], w13, w2, l, 1, 6, S,
                 mixed=(o_a, o_b, o_c, o_d), w_out=w_o, gates=(gdn, rg))
    return h.reshape(B, S, D)
```

```python
import collections
import functools
import math

import numpy as np
import jax
import jax.numpy as jnp
from jax import lax
from jax.experimental import pallas as pl
from jax.experimental.pallas import tpu as pltpu

F32 = jnp.float32
BF16 = jnp.bfloat16
HIGHEST = lax.Precision.HIGHEST

D_MODEL = 1024
N_HEADS = 4
HEAD_DIM = 64
GROUP_W = N_HEADS * HEAD_DIM
DIFF_DIM = HEAD_DIM // 2
D_FF = 2816
N_MOD = 9
EPS = 1e-6
LNX_EPS = 64e-5
ROPE_THETA = 10000.0
LOG2E = 1.4426950408889634
CONV_K = 4
RWKV_START = 9 * GROUP_W + 2 * N_HEADS + GROUP_W

ATTN_W = 6 * GROUP_W
QK_W = 4 * GROUP_W
KV_TILE = 256
GDN_W = 3 * GROUP_W + GROUP_W + 128
RWKV_W = 3 * GROUP_W + 128 + 128 + 256 + 128
PROJ_W = ATTN_W + GDN_W + RWKV_W
VMEM_LIMIT = 56 * 1024 * 1024


def _cparams(sem):
    return pltpu.CompilerParams(dimension_semantics=sem, vmem_limit_bytes=VMEM_LIMIT)


def _sigmoid(x):
    return 1.0 / (1.0 + jnp.exp(-x))


def _softplus(x):
    return jnp.maximum(x, 0.0) + jnp.log(1.0 + jnp.exp(-jnp.abs(x)))


def _group_ones(n, group):
    r = lax.broadcasted_iota(jnp.int32, (n, n), 0) // group
    c = lax.broadcasted_iota(jnp.int32, (n, n), 1) // group
    return (r == c).astype(BF16)


def _group_sum(x, group):
    ones = _group_ones(x.shape[-1], group)
    hi = x.astype(BF16)
    lo = (x - hi.astype(F32)).astype(BF16)
    return (jnp.dot(hi, ones, preferred_element_type=F32)
            + jnp.dot(lo, ones, preferred_element_type=F32))


def _modulate(x, gain, shift, scale):
    ms = jnp.mean(x * x, axis=-1, keepdims=True)
    return (x * lax.rsqrt(ms + EPS) * gain) * (1.0 + scale) + shift


def _mod_kernel(c_ref, w_ref, b_ref, o_ref):
    c = c_ref[...]
    cond = c * _sigmoid(c)
    o_ref[...] = jnp.dot(cond, w_ref[...], precision=HIGHEST,
                         preferred_element_type=F32) + b_ref[...]


def _mod_all(c, ada_w, ada_b):
    L, D, N = ada_w.shape
    B = c.shape[0]
    tn = 1024
    out = pl.pallas_call(
        _mod_kernel,
        name="adaln_mod",
        grid=(L, N // tn),
        in_specs=[
            pl.BlockSpec((B, D), lambda l, n: (0, 0)),
            pl.BlockSpec((None, D, tn), lambda l, n: (l, 0, n)),
            pl.BlockSpec((None, 1, tn), lambda l, n: (l, 0, n)),
        ],
        out_specs=pl.BlockSpec((None, B, tn), lambda l, n: (l, 0, n)),
        out_shape=jax.ShapeDtypeStruct((L, B, N), F32),
        compiler_params=_cparams(("parallel", "parallel")),
    )(c, ada_w, ada_b.reshape(L, 1, N))
    return out.reshape(L, B, N_MOD, D)


def _ffn_kernel(*refs, mb, tf, mix):
    x_ref, mod_ref, g_ref, w13_ref, w2_ref = refs[:5]
    o_ref, act_ref = refs[-2:]
    x = x_ref[...]
    if mix:
        oa_ref, ob_ref, oc_ref, od_ref, w_ref, gc_ref, gd_ref = refs[5:12]
        gc = gc_ref[...]
        o_c = (oc_ref[...].astype(F32) * (gc * _sigmoid(gc))).astype(BF16)
        o_d = (od_ref[...].astype(F32) * gd_ref[...]).astype(BF16)
        proj = jnp.dot(oa_ref[...], w_ref[0:GROUP_W, :], preferred_element_type=F32)
        for g, o_g in ((1, ob_ref[...]), (2, o_c), (3, o_d)):
            proj += jnp.dot(o_g, w_ref[g * GROUP_W:(g + 1) * GROUP_W, :],
                            preferred_element_type=F32)
        x = x + mod_ref[5:6, :] * proj
    hn = _modulate(x, g_ref[...], mod_ref[mb:mb + 1, :], mod_ref[mb + 1:mb + 2, :]).astype(BF16)
    for f in range(D_FF // tf):
        gate = jnp.dot(hn, w13_ref[:, f * tf:(f + 1) * tf], preferred_element_type=F32)
        up = jnp.dot(hn, w13_ref[:, D_FF + f * tf:D_FF + (f + 1) * tf], preferred_element_type=F32)
        act_ref[:, f * tf:(f + 1) * tf] = (gate * _sigmoid(gate) * up).astype(BF16)
    acc = jnp.dot(act_ref[...], w2_ref[...], preferred_element_type=F32)
    o_ref[...] = x + 0.5 * mod_ref[mb + 2:mb + 3, :] * acc


def _ffn(h, mod_l, gain, w13, w2, l, j, mb, S, mixed=None, w_out=None, gates=None):
    T, D = h.shape
    tm = min(1024, S)
    tf = 256
    tpb = S // tm
    resident = dict(pipeline_mode=pl.Buffered(1))
    in_specs = [
        pl.BlockSpec((tm, D), lambda i: (i, 0)),
        pl.BlockSpec((None, N_MOD, D), lambda i: (i // tpb, 0, 0)),
        pl.BlockSpec((1, D), lambda i: (0, 0)),
        pl.BlockSpec((None, None, D, 2 * D_FF), lambda i: (l, j, 0, 0), **resident),
        pl.BlockSpec((None, None, D_FF, D), lambda i: (l, j, 0, 0), **resident),
    ]
    args = [h, mod_l, gain.reshape(1, D), w13, w2]
    if mixed is not None:
        in_specs += [pl.BlockSpec((None, tm, GROUP_W), lambda i: (i // tpb, i % tpb, 0))] * 4
        in_specs += [pl.BlockSpec((None, D, D), lambda i: (l, 0, 0), **resident)]
        in_specs += [pl.BlockSpec((None, tm, GROUP_W), lambda i: (i // tpb, i % tpb, 3)),
                     pl.BlockSpec((None, tm, GROUP_W), lambda i: (i // tpb, i % tpb, 0))]
        args += list(mixed) + [w_out] + list(gates)
    return pl.pallas_call(
        functools.partial(_ffn_kernel, mb=mb, tf=tf, mix=mixed is not None),
        name="ffn",
        grid=(T // tm,),
        in_specs=in_specs,
        out_specs=pl.BlockSpec((tm, D), lambda i: (i, 0)),
        out_shape=jax.ShapeDtypeStruct((T, D), F32),
        scratch_shapes=[pltpu.VMEM((tm, D_FF), BF16)],
        compiler_params=_cparams(("parallel",)),
    )(*args)


def _inproj_kernel(x_ref, mod_ref, g_ref, w_ref, qkg_ref, cos_ref, sin_ref,
                   attn_ref, vtd_ref, vts_ref, gdn_ref, rwkv_ref):
    hn = _modulate(x_ref[...], g_ref[...], mod_ref[3:4, :], mod_ref[4:5, :]).astype(BF16)

    pa = jnp.dot(hn, w_ref[:, 0:3 * GROUP_W], preferred_element_type=F32)
    cos = cos_ref[...]
    sin = sin_ref[...]
    for idx, scale in ((0, DIFF_DIM ** -0.5 * LOG2E), (1, 1.0)):
        x1 = pa[:, idx * 256:idx * 256 + 128]
        x2 = pa[:, idx * 256 + 128:idx * 256 + 256]
        ms = _group_sum(x1 * x1 + x2 * x2, 16) * (1.0 / DIFF_DIM)
        inv = lax.rsqrt(ms + EPS)
        n1 = x1 * inv * qkg_ref[2 * idx:2 * idx + 1, :]
        n2 = x2 * inv * qkg_ref[2 * idx + 1:2 * idx + 2, :]
        attn_ref[:, idx * 256:idx * 256 + 128] = ((n1 * cos - n2 * sin) * scale).astype(BF16)
        attn_ref[:, idx * 256 + 128:idx * 256 + 256] = ((n2 * cos + n1 * sin) * scale).astype(BF16)
    n_kv = x_ref.shape[0] // KV_TILE
    for c in range(n_kv):
        vtd_ref[c] = pa[c * KV_TILE:(c + 1) * KV_TILE, 512:768].T.astype(BF16)

    pb = jnp.dot(hn, w_ref[:, 3 * GROUP_W:ATTN_W], preferred_element_type=F32)
    attn_ref[:, 512:768] = (pb[:, 0:256] * (HEAD_DIM ** -0.5)).astype(BF16)
    attn_ref[:, 768:1024] = pb[:, 256:512].astype(BF16)
    for c in range(n_kv):
        vts_ref[c] = pb[c * KV_TILE:(c + 1) * KV_TILE, 512:768].T.astype(BF16)

    gdn_ref[...] = jnp.dot(hn, w_ref[:, ATTN_W:ATTN_W + GDN_W], preferred_element_type=F32)
    rwkv_ref[...] = jnp.dot(hn, w_ref[:, ATTN_W + GDN_W:PROJ_W], preferred_element_type=F32)


def _inproj(h, mod_l, gain, w_perm, l, qkg, cos128, sin128, S):
    T, D = h.shape
    tm = min(512, S)
    tpb = S // tm
    vt_spec = pl.BlockSpec((tm // KV_TILE, GROUP_W, KV_TILE), lambda i: (i, 0, 0))
    vt_shape = jax.ShapeDtypeStruct((T // KV_TILE, GROUP_W, KV_TILE), BF16)
    return pl.pallas_call(
        _inproj_kernel,
        name="inproj",
        grid=(T // tm,),
        in_specs=[
            pl.BlockSpec((tm, D), lambda i: (i, 0)),
            pl.BlockSpec((None, N_MOD, D), lambda i: (i // tpb, 0, 0)),
            pl.BlockSpec((1, D), lambda i: (0, 0)),
            pl.BlockSpec((None, D, PROJ_W), lambda i: (l, 0, 0)),
            pl.BlockSpec((4, 128), lambda i: (0, 0)),
            pl.BlockSpec((tm, 128), lambda i: (i % tpb, 0)),
            pl.BlockSpec((tm, 128), lambda i: (i % tpb, 0)),
        ],
        out_specs=[
            pl.BlockSpec((tm, QK_W), lambda i: (i, 0)),
            vt_spec, vt_spec,
            pl.BlockSpec((tm, GDN_W), lambda i: (i, 0)),
            pl.BlockSpec((tm, RWKV_W), lambda i: (i, 0)),
        ],
        out_shape=[
            jax.ShapeDtypeStruct((T, QK_W), BF16),
            vt_shape, vt_shape,
            jax.ShapeDtypeStruct((T, GDN_W), F32),
            jax.ShapeDtypeStruct((T, RWKV_W), F32),
        ],
        compiler_params=_cparams(("parallel",)),
    )(h, mod_l, gain.reshape(1, D), w_perm, qkg, cos128, sin128)


def _attn_kernel(qd_ref, kd_ref, vtd_ref, qs_ref, ks_ref, vts_ref, lamv_ref, g_ref,
                 od_ref, os_ref, qc_ref, qh_ref, tri_ref, accd_ref, accs_ref, *, lam_init, t):
    i = pl.program_id(1)
    heads = range(N_HEADS)
    lane = lax.broadcasted_iota(jnp.int32, (1, GROUP_W), 1)
    r_i = lax.broadcasted_iota(jnp.int32, (t, t), 0)
    c_i = lax.broadcasted_iota(jnp.int32, (t, t), 1)
    tri_ref[...] = (c_i > r_i).astype(BF16)
    qd = qd_ref[...]
    qs = qs_ref[...]
    sub = (lane % 128) // 16
    for h in heads:
        qc_ref[h, 0:t, :] = jnp.where(sub == 2 * h, qd, jnp.zeros_like(qd))
        qc_ref[h, t:2 * t, :] = jnp.where(sub == 2 * h + 1, qd, jnp.zeros_like(qd))
        qh_ref[h] = jnp.where(lane // HEAD_DIM == h, qs, jnp.zeros_like(qs))
    accd_ref[...] = jnp.zeros_like(accd_ref)
    accs_ref[...] = jnp.zeros_like(accs_ref)

    def tile(kt, carry, diag):
        stats, keep = carry
        start = pl.multiple_of(kt * t, t)
        kd = kd_ref[pl.ds(start, t), :]
        ks = ks_ref[pl.ds(start, t), :]
        nt = (((1,), (1,)), ((), ()))
        ss = [lax.dot_general(kd, qc_ref[h], nt, preferred_element_type=F32) for h in heads]
        zs = [lax.dot_general(ks, qh_ref[h], nt, preferred_element_type=F32) for h in heads]
        new_stats, ps, alphas = [], [], []
        for h in heads:
            m, l = stats[h]
            s = ss[h]
            if diag:
                allowed = jnp.concatenate([r_i <= c_i, r_i <= c_i], axis=1)
                s = jnp.where(allowed, s, -jnp.inf)
            m_new = jnp.maximum(m, jnp.max(s, axis=0, keepdims=True))
            alpha = jnp.exp2(m - m_new)
            p = jnp.exp2(s - m_new)
            new_stats.append((m_new, alpha * l + jnp.sum(p, axis=0, keepdims=True)))
            ps.append(p.astype(BF16))
            alphas.append(alpha)
        strict = r_i < c_i
        lss, lks, sums = [], [], []
        for h in heads:
            z = zs[h]
            neg_abs = lax.bitcast_convert_type(
                lax.bitcast_convert_type(z, jnp.uint32) | jnp.uint32(0x80000000), F32)
            ls = jnp.minimum(z, 0.0) - jnp.log(1.0 + jnp.exp(neg_abs))
            lk = ls - z
            if diag:
                lk = jnp.where(strict, lk, 0.0)
            lss.append(ls)
            lks.append(lk.astype(BF16))
            sums.append(jnp.sum(lk, axis=0, keepdims=True))
        bts = [jnp.dot(tri_ref[...], lks[h], preferred_element_type=F32) for h in heads]
        for h in heads:
            vt = vtd_ref[kt, h * HEAD_DIM:(h + 1) * HEAD_DIM, :]
            accd_ref[h] = alphas[h] * accd_ref[h] + jnp.dot(vt, ps[h], preferred_element_type=F32)
        pas = []
        for h in heads:
            a = jnp.exp(lss[h] + (bts[h] + keep[h]))
            if diag:
                a = jnp.where(strict, a, 0.0)
            pas.append(a.astype(BF16))
        for h in heads:
            vt = vts_ref[kt, h * HEAD_DIM:(h + 1) * HEAD_DIM, :]
            accs_ref[h] += jnp.dot(vt, pas[h], preferred_element_type=F32)
        return tuple(new_stats), tuple(keep[h] + sums[h] for h in heads)

    init = (tuple((jnp.full((1, 2 * t), -jnp.inf, F32), jnp.zeros((1, 2 * t), F32)) for _ in heads),
            tuple(jnp.zeros((1, t), F32) for _ in heads))
    carry = tile(i, init, True)
    stats, _ = lax.fori_loop(0, i, lambda n, c: tile(i - 1 - n, c, False), carry)

    lv = lamv_ref[...]
    lam = (jnp.exp(jnp.sum(lv[0:1, :] * lv[1:2, :], axis=1, keepdims=True))
           - jnp.exp(jnp.sum(lv[2:3, :] * lv[3:4, :], axis=1, keepdims=True)) + lam_init)
    rows = []
    for h in heads:
        o = accd_ref[h] / stats[h][1]
        oh = o[:, :t] - lam * o[:, t:]
        ms = jnp.mean(oh * oh, axis=0, keepdims=True)
        rows.append(oh * lax.rsqrt(ms + EPS) * g_ref[h * HEAD_DIM:(h + 1) * HEAD_DIM, :]
                    * (1.0 - lam_init))
    od_ref[...] = jnp.concatenate(rows, axis=0).T.astype(BF16)
    os_ref[...] = jnp.concatenate([accs_ref[h] for h in heads], axis=0).T.astype(BF16)


def _attention(attn, vt_d, vt_s, lamv, subln_g, lam_init):
    B, S, _ = attn.shape
    t = KV_TILE
    gain = jnp.broadcast_to(jnp.tile(subln_g, N_HEADS)[:, None], (GROUP_W, t))
    q_spec = lambda c: pl.BlockSpec((None, t, GROUP_W), lambda b, i: (b, i, c))
    k_spec = lambda c: pl.BlockSpec((None, S, GROUP_W), lambda b, i: (b, 0, c))
    vt_spec = pl.BlockSpec((None, S // t, GROUP_W, t), lambda b, i: (b, 0, 0, 0))
    out = jax.ShapeDtypeStruct((B, S, GROUP_W), BF16)
    return pl.pallas_call(
        functools.partial(_attn_kernel, lam_init=lam_init, t=t),
        name="attn",
        grid=(B, S // t),
        in_specs=[q_spec(0), k_spec(1), vt_spec, q_spec(2), k_spec(3), vt_spec,
                  pl.BlockSpec((4, DIFF_DIM), lambda b, i: (0, 0)),
                  pl.BlockSpec((GROUP_W, t), lambda b, i: (0, 0))],
        out_specs=[q_spec(0), q_spec(0)],
        out_shape=[out, out],
        scratch_shapes=[pltpu.VMEM((N_HEADS, 2 * t, GROUP_W), BF16),
                        pltpu.VMEM((N_HEADS, t, GROUP_W), BF16),
                        pltpu.VMEM((t, t), BF16),
                        pltpu.VMEM((N_HEADS, HEAD_DIM, 2 * t), F32),
                        pltpu.VMEM((N_HEADS, HEAD_DIM, t), F32)],
        compiler_params=_cparams(("parallel", "parallel")),
    )(attn, attn, vt_d, attn, attn, vt_s, lamv, gain)


def _gdn_prep_kernel(x_ref, prev_ref, small_ref, cw_ref, alog_ref, dtb_ref,
                     q_ref, k_ref, v_ref, ab_ref, xs_ref, *, ts):
    i = pl.program_id(1)
    prev = prev_ref[...]
    xs_ref[0:8, :] = jnp.where(i == 0, jnp.zeros_like(prev), prev)
    xs_ref[8:8 + ts, :] = x_ref[...]
    conv = jnp.zeros((ts, 3 * GROUP_W), F32)
    for j in range(CONV_K):
        conv = conv + xs_ref[pl.ds(8 - (CONV_K - 1) + j, ts), :] * cw_ref[j:j + 1, :]
    y = conv * _sigmoid(conv)
    q = y[:, 0:GROUP_W]
    k = y[:, GROUP_W:2 * GROUP_W]
    q_ref[...] = (q * lax.rsqrt(_group_sum(q * q, HEAD_DIM) + EPS) * (HEAD_DIM ** -0.5)).astype(BF16)
    k_ref[...] = (k * lax.rsqrt(_group_sum(k * k, HEAD_DIM) + EPS)).astype(BF16)
    v_ref[...] = y[:, 2 * GROUP_W:3 * GROUP_W].astype(BF16)
    small = small_ref[...]
    beta = _sigmoid(small)
    alpha = jnp.exp(-jnp.exp(alog_ref[...]) * _softplus(small + dtb_ref[...]))
    lane = lax.broadcasted_iota(jnp.int32, (1, 128), 1)
    ab_ref[...] = jnp.where(lane < N_HEADS, beta, alpha).T[0:2 * N_HEADS, :]


def _gdn_prep(gdn, conv_w, a_log, dt_bias):
    B, S, _ = gdn.shape
    ts = min(1024, S)
    pad = jnp.zeros((N_HEADS,), F32)
    alog = jnp.concatenate([pad, a_log, jnp.zeros((120,), F32)]).reshape(1, 128)
    dtb = jnp.concatenate([pad, dt_bias, jnp.zeros((120,), F32)]).reshape(1, 128)
    tok = jax.ShapeDtypeStruct((B, S, GROUP_W), F32)
    return pl.pallas_call(
        functools.partial(_gdn_prep_kernel, ts=ts),
        name="gdn_prep",
        grid=(B, S // ts),
        in_specs=[
            pl.BlockSpec((None, ts, 3 * GROUP_W), lambda b, i: (b, i, 0)),
            pl.BlockSpec((None, 8, 3 * GROUP_W), lambda b, i: (b, jnp.maximum(i * (ts // 8) - 1, 0), 0)),
            pl.BlockSpec((None, ts, 128), lambda b, i: (b, i, 8)),
            pl.BlockSpec((CONV_K, 3 * GROUP_W), lambda b, i: (0, 0)),
            pl.BlockSpec((1, 128), lambda b, i: (0, 0)),
            pl.BlockSpec((1, 128), lambda b, i: (0, 0)),
        ],
        out_specs=[
            pl.BlockSpec((None, ts, GROUP_W), lambda b, i: (b, i, 0)),
            pl.BlockSpec((None, ts, GROUP_W), lambda b, i: (b, i, 0)),
            pl.BlockSpec((None, ts, GROUP_W), lambda b, i: (b, i, 0)),
            pl.BlockSpec((None, 2 * N_HEADS, ts), lambda b, i: (b, 0, i)),
        ],
        out_shape=[tok.update(dtype=BF16)] * 3 + [jax.ShapeDtypeStruct((B, 2 * N_HEADS, S), F32)],
        scratch_shapes=[pltpu.VMEM((ts + 8, 3 * GROUP_W), F32)],
        compiler_params=_cparams(("parallel", "parallel")),
    )(gdn, gdn, gdn, conv_w, alog, dtb)


def _gdn_parts(qb_ref, kb_ref, vb_ref, al_ref, be_ref, knb_ref, ng_ref, o_ref,
               s_ref, ks_ref, q_ref, k_ref, v_ref, kn_ref):
    @pl.when(pl.program_id(0) == 0)
    def _():
        s_ref[...] = jnp.zeros_like(s_ref)
        ks_ref[...] = jnp.zeros_like(ks_ref)

    q_ref[...] = qb_ref[...].astype(F32)
    k_ref[...] = kb_ref[...].astype(F32)
    v_ref[...] = vb_ref[...].astype(F32)
    kn_ref[...] = knb_ref[...].astype(F32)

    def step(t, ks, k_next):
        al = al_ref[pl.ds(t, 1), :]
        d = be_ref[pl.ds(t, 1), :] * (v_ref[t] - al * ks)
        o_parts = [None] * 2
        n_parts = [None] * 2
        for kk in range(HEAD_DIM):
            s_new = al * s_ref[kk] + k_ref[t, pl.ds(kk, 1), :] * d
            s_ref[kk] = s_new
            t_o = s_new * q_ref[t, pl.ds(kk, 1), :]
            t_n = s_new * k_next(kk)
            o_parts[kk % 2] = t_o if o_parts[kk % 2] is None else o_parts[kk % 2] + t_o
            n_parts[kk % 2] = t_n if n_parts[kk % 2] is None else n_parts[kk % 2] + t_n
        return n_parts[0] + n_parts[1], o_parts[0] + o_parts[1]

    def finish(t, o):
        ms = jnp.mean(o * o, axis=0, keepdims=True)
        o_ref[t] = o * lax.rsqrt(ms + EPS) * ng_ref[...]

    return _ScanParts(step, finish, lambda t: (lambda kk: k_ref[t + 1, pl.ds(kk, 1), :]),
                      lambda kk: kn_ref[0, pl.ds(kk, 1), :], ks_ref)


def _next_row_spec(tt, S, N, Ln):
    return pl.BlockSpec((1, N, Ln), lambda i: (jnp.minimum((i + 1) * tt, S - 1), 0, 0))


def _rwkv_prep_kernel(*refs, ts, first):
    if first:
        (x_ref, prev_ref, mu_ref, vecs_ref, wup_ref, aup_ref, gup_ref,
         r_ref, w_ref, k_ref, v_ref, a_ref, b_ref, g_ref, vf_ref, xs_ref) = refs
    else:
        (x_ref, prev_ref, mu_ref, vecs_ref, wup_ref, aup_ref, gup_ref, vup_ref, vfin_ref,
         r_ref, w_ref, k_ref, v_ref, a_ref, b_ref, g_ref, xs_ref) = refs
    i = pl.program_id(1)
    prev = prev_ref[...]
    xs_ref[0:8, :] = jnp.where(i == 0, jnp.zeros_like(prev), prev)
    x = x_ref[...]
    xs_ref[8:8 + ts, :] = x
    xm = x + (xs_ref[pl.ds(7, ts), :] - x) * mu_ref[...]
    r = xm[:, 0:256]
    k = xm[:, 256:512]
    v = xm[:, 512:768]
    wd = xm[:, 768:896]
    ad = xm[:, 896:1024]
    gd = xm[:, 1024:1280]
    w0, a0, k_k, k_a = vecs_ref[0:1, :], vecs_ref[1:2, :], vecs_ref[2:3, :], vecs_ref[3:4, :]
    dot = lambda x, w: jnp.dot(x.astype(BF16), w, preferred_element_type=F32)
    w_log = -_softplus(-(w0 + dot(jnp.tanh(wd), wup_ref[...]))) - 0.5
    a = _sigmoid(a0 + dot(ad, aup_ref[...]))
    g_ref[...] = dot(_sigmoid(gd), gup_ref[...])
    if first:
        vf_ref[...] = v
    else:
        vd = xm[:, 1280:1408]
        v = v + (vfin_ref[...] - v) * _sigmoid(vecs_ref[4:5, :] + dot(vd, vup_ref[...]))
    kk = k * k_k
    kk = kk * lax.rsqrt(_group_sum(kk * kk, HEAD_DIM) + EPS)
    r_ref[...] = r.astype(BF16)
    w_ref[...] = jnp.exp(-jnp.exp(w_log))
    k_ref[...] = (k * (1.0 + (a - 1.0) * k_a)).astype(BF16)
    v_ref[...] = v.astype(BF16)
    a_ref[...] = (-kk).astype(BF16)
    b_ref[...] = (kk * a).astype(BF16)


def _pad_rows(w, rows):
    return jnp.concatenate([w, jnp.zeros((rows - w.shape[0], w.shape[1]), w.dtype)],
                           axis=0).astype(BF16)


def _rwkv_prep(rw, mu, vecs, w_up, a_up, g_up, vmix_up, v_first):
    B, S, _ = rw.shape
    ts = min(1024, S)
    first = v_first is None
    tok_spec = pl.BlockSpec((None, ts, GROUP_W), lambda b, i: (b, i, 0))
    const = lambda shape: pl.BlockSpec(shape, lambda b, i: (0,) * len(shape))
    in_specs = [
        pl.BlockSpec((None, ts, RWKV_W), lambda b, i: (b, i, 0)),
        pl.BlockSpec((None, 8, RWKV_W), lambda b, i: (b, jnp.maximum(i * (ts // 8) - 1, 0), 0)),
        const((1, RWKV_W)), const((8, GROUP_W)),
        const((128, GROUP_W)), const((128, GROUP_W)), const((256, GROUP_W)),
    ]
    args = [rw, rw, mu, vecs, _pad_rows(w_up, 128), _pad_rows(a_up, 128), _pad_rows(g_up, 256)]
    if not first:
        in_specs += [const((128, GROUP_W)), tok_spec]
        args += [_pad_rows(vmix_up, 128), v_first]
    tok = jax.ShapeDtypeStruct((B, S, GROUP_W), F32)
    n_out = 8 if first else 7
    return pl.pallas_call(
        functools.partial(_rwkv_prep_kernel, ts=ts, first=first),
        name="rwkv_prep",
        grid=(B, S // ts),
        in_specs=in_specs,
        out_specs=[tok_spec] * n_out,
        out_shape=[tok.update(dtype=BF16 if n in (0, 2, 3, 4, 5) else F32) for n in range(n_out)],
        scratch_shapes=[pltpu.VMEM((ts + 8, RWKV_W), F32)],
        compiler_params=_cparams(("parallel", "parallel")),
    )(*args)


def _wkv_parts(rb_ref, w_ref, kb_ref, vb_ref, ab_ref, bb_ref, anb_ref, tab_ref, y_ref,
               s_ref, sa_ref, r_ref, k_ref, v_ref, a_ref, b_ref, an_ref):
    @pl.when(pl.program_id(0) == 0)
    def _():
        s_ref[...] = jnp.zeros_like(s_ref)
        sa_ref[...] = jnp.zeros_like(sa_ref)

    for dst, src in ((r_ref, rb_ref), (k_ref, kb_ref), (v_ref, vb_ref), (a_ref, ab_ref),
                     (b_ref, bb_ref), (an_ref, anb_ref)):
        dst[...] = src[...].astype(F32)

    def step(t, sa, a_next):
        vt = v_ref[t]
        y_parts = [None] * 2
        n_parts = [None] * 2
        for j in range(HEAD_DIM):
            s_new = (s_ref[j] * w_ref[t, pl.ds(j, 1), :]
                     + (sa * b_ref[t, pl.ds(j, 1), :] + vt * k_ref[t, pl.ds(j, 1), :]))
            s_ref[j] = s_new
            t_y = s_new * r_ref[t, pl.ds(j, 1), :]
            t_n = s_new * a_next(j)
            y_parts[j % 2] = t_y if y_parts[j % 2] is None else y_parts[j % 2] + t_y
            n_parts[j % 2] = t_n if n_parts[j % 2] is None else n_parts[j % 2] + t_n
        return n_parts[0] + n_parts[1], y_parts[0] + y_parts[1]

    def finish(t, y):
        yc = y - jnp.mean(y, axis=0, keepdims=True)
        var = jnp.mean(yc * yc, axis=0, keepdims=True)
        yn = yc * lax.rsqrt(var + LNX_EPS) * tab_ref[0] + tab_ref[1]
        bonus = jnp.sum(r_ref[t] * k_ref[t] * tab_ref[2], axis=0, keepdims=True)
        y_ref[t] = yn + bonus * v_ref[t]

    return _ScanParts(step, finish, lambda t: (lambda j: a_ref[t + 1, pl.ds(j, 1), :]),
                      lambda j: an_ref[0, pl.ds(j, 1), :], sa_ref)


_ScanParts = collections.namedtuple("_ScanParts", "step finish next_in_block next_block carry_ref")


def _scans_kernel(*refs, tt):
    gdn = _gdn_parts(*refs[0:7], refs[15], *refs[17:23])
    wkv = _wkv_parts(*refs[7:15], refs[16], *refs[23:31])
    parts = (gdn, wkv)

    def body(t, carries):
        for p, c in zip(parts, carries):
            p.finish(t - 1, c[1])
        return tuple(p.step(t, c[0], p.next_in_block(t)) for p, c in zip(parts, carries))

    carries = tuple(p.step(0, p.carry_ref[...], p.next_in_block(0)) for p in parts)
    carries = lax.fori_loop(1, tt - 1, body, carries)
    for p, c in zip(parts, carries):
        p.finish(tt - 2, c[1])
    carries = tuple(p.step(tt - 1, c[0], p.next_block) for p, c in zip(parts, carries))
    for p, c in zip(parts, carries):
        p.finish(tt - 1, c[1])
        p.carry_ref[...] = c[0]


def _scans(gq, gk, gv, al, be, norm_g, r, w, k, v, a, b, tables):
    S, N, Ln = gq.shape
    tt = min(32, S)
    seq = pl.BlockSpec((tt, N, Ln), lambda i: (i, 0, 0))
    vec = pl.BlockSpec((tt, Ln), lambda i: (i, 0))
    nxt = _next_row_spec(tt, S, N, Ln)
    out = jax.ShapeDtypeStruct((S, N, Ln), F32)
    state = [pltpu.VMEM((N, N, Ln), F32), pltpu.VMEM((N, Ln), F32)]
    wide = pltpu.VMEM((tt, N, Ln), F32)
    wide_row = pltpu.VMEM((1, N, Ln), F32)
    return pl.pallas_call(
        functools.partial(_scans_kernel, tt=tt),
        name="scans",
        grid=(S // tt,),
        in_specs=[seq, seq, seq, vec, vec, nxt, pl.BlockSpec((N, Ln), lambda i: (0, 0))]
        + [seq] * 6 + [nxt, pl.BlockSpec((3, N, Ln), lambda i: (0, 0, 0))],
        out_specs=[seq, seq],
        out_shape=[out, out],
        scratch_shapes=state + [wide] * 3 + [wide_row] + state + [wide] * 5 + [wide_row],
        compiler_params=_cparams(("arbitrary",)),
    )(gq, gk, gv, al, be, gk, jnp.broadcast_to(norm_g[:, None], (N, Ln)),
      r, w, k, v, a, b, a, tables)


def _pad_cols(w, width):
    pad = width - w.shape[-1]
    return jnp.concatenate([w, jnp.zeros(w.shape[:-1] + (pad,), w.dtype)], axis=-1)


def _rope_perm():
    p = np.arange(128)
    first = (p // 16) * DIFF_DIM + p % 16
    return np.concatenate([first, first + 16])


def _permute_w_in(w_in, vmix_down):
    L, D, _ = w_in.shape
    perm = _rope_perm()
    g0 = 9 * GROUP_W
    r0 = RWKV_START
    vd = jnp.concatenate([jnp.zeros((1, D, vmix_down.shape[-1]), w_in.dtype), vmix_down], axis=0)
    cols = [
        w_in[..., perm], w_in[..., GROUP_W + perm], w_in[..., 2 * GROUP_W:3 * GROUP_W],
        w_in[..., 3 * GROUP_W:6 * GROUP_W],
        w_in[..., 6 * GROUP_W:g0], w_in[..., g0 + 8:g0 + 8 + GROUP_W], _pad_cols(w_in[..., g0:g0 + 8], 128),
        w_in[..., r0:r0 + 768], _pad_cols(w_in[..., r0 + 768:r0 + 832], 128),
        _pad_cols(w_in[..., r0 + 832:r0 + 896], 128), _pad_cols(w_in[..., r0 + 896:r0 + 1056], 256),
        _pad_cols(vd, 128),
    ]
    return jnp.concatenate(cols, axis=-1).astype(BF16)


def _permute_mu(mu, vmix_mu):
    L = mu.shape[0]
    vm = jnp.concatenate([jnp.zeros((1, vmix_mu.shape[-1]), mu.dtype), vmix_mu], axis=0)
    cols = [mu[:, 0:768], _pad_cols(mu[:, 768:832], 128), _pad_cols(mu[:, 832:896], 128),
            _pad_cols(mu[:, 896:1056], 256), _pad_cols(vm, 128)]
    return jnp.concatenate(cols, axis=-1).reshape(L, 1, RWKV_W)


def _to_lanes(x):
    B, S, _ = x.shape
    return x.reshape(B, S, N_HEADS, HEAD_DIM).transpose(1, 3, 0, 2).reshape(S, HEAD_DIM, B * N_HEADS)


def _from_lanes(y, B):
    S = y.shape[0]
    return y.reshape(S, HEAD_DIM, B, N_HEADS).transpose(2, 0, 3, 1).reshape(B, S, GROUP_W)


def _per_head_table(x, B):
    return jnp.tile(x.T, (1, B))


def _head_scalars_to_lanes(x):
    B, _, S = x.shape
    return x.transpose(2, 0, 1).reshape(S, B * N_HEADS)


def kernel(x, c, ada_w, ada_b, norm_g, ffn_w13, ffn_w2, w_in, w_out, diff_qk_g, diff_lambda, diff_subln_g, gdn_conv_w, gdn_a_log, gdn_dt_bias, gdn_norm_g, rwkv_mu, rwkv_w0, rwkv_w_up, rwkv_a0, rwkv_a_up, rwkv_g_up, rwkv_k_k, rwkv_k_a, rwkv_r_k, rwkv_lnx_w, rwkv_lnx_b, rwkv_vmix_down, rwkv_vmix_mu, rwkv_v0, rwkv_vmix_up):
    B, S, D = x.shape
    L = ada_w.shape[0]
    T = B * S

    mod = _mod_all(c, ada_w, ada_b)
    w13 = ffn_w13.astype(BF16)
    w2 = ffn_w2.astype(BF16)
    w_perm = _permute_w_in(w_in, rwkv_vmix_down)
    w_o = w_out.astype(BF16)
    mu = _permute_mu(rwkv_mu, rwkv_vmix_mu)

    inv = ROPE_THETA ** (-jnp.arange(0, DIFF_DIM, 2, dtype=F32) / DIFF_DIM)
    ang = jnp.arange(S, dtype=F32)[:, None] * inv[None, :]
    cos128 = jnp.tile(jnp.cos(ang), (1, 8))
    sin128 = jnp.tile(jnp.sin(ang), (1, 8))

    h = x.reshape(T, D)
    v_first = None
    for l in range(L):
        lam_init = 0.8 - 0.6 * math.exp(-0.3 * l)
        h = _ffn(h, mod[l], norm_g[l, 0], w13, w2, l, 0, 0, S)

        qkg = jnp.stack([jnp.tile(diff_qk_g[l, 0, :16], 8), jnp.tile(diff_qk_g[l, 0, 16:], 8),
                         jnp.tile(diff_qk_g[l, 1, :16], 8), jnp.tile(diff_qk_g[l, 1, 16:], 8)])
        attn, vt_d, vt_s, gdn, rw = _inproj(h, mod[l], norm_g[l, 1], w_perm, l, qkg,
                                            cos128, sin128, S)
        attn = attn.reshape(B, S, QK_W)
        kv_tiles = lambda t: t.reshape(B, S // KV_TILE, GROUP_W, KV_TILE)
        gdn = gdn.reshape(B, S, GDN_W)
        rw = rw.reshape(B, S, RWKV_W)

        o_a, o_b = _attention(attn, kv_tiles(vt_d), kv_tiles(vt_s), diff_lambda[l],
                              diff_subln_g[l], lam_init)

        gq, gk, gv, gab = _gdn_prep(gdn, gdn_conv_w[l], gdn_a_log[l], gdn_dt_bias[l])

        zeros = jnp.zeros((GROUP_W,), F32)
        vecs = jnp.stack([rwkv_w0[l], rwkv_a0[l], rwkv_k_k[l], rwkv_k_a[l],
                          rwkv_v0[l - 1] if l > 0 else zeros, zeros, zeros, zeros])
        outs = _rwkv_prep(rw, mu[l], vecs, rwkv_w_up[l], rwkv_a_up[l], rwkv_g_up[l],
                          rwkv_vmix_up[l - 1] if l > 0 else None, v_first)
        rr, rw_, rk, rv, ra, rb, rg = outs[:7]
        if l == 0:
            v_first = outs[7]
        tables = jnp.stack([_per_head_table(t.reshape(N_HEADS, HEAD_DIM), B)
                            for t in (rwkv_lnx_w[l], rwkv_lnx_b[l], rwkv_r_k[l])])
        go, ry = _scans(_to_lanes(gq), _to_lanes(gk), _to_lanes(gv),
                        _head_scalars_to_lanes(gab[:, N_HEADS:2 * N_HEADS]),
                        _head_scalars_to_lanes(gab[:, 0:N_HEADS]), gdn_norm_g[l],
                        *[_to_lanes(t) for t in (rr, rw_, rk, rv, ra, rb)], tables)
        o_c = _from_lanes(go, B)
        o_d = _from_lanes(ry, B)

        h = _ffn(h, mod[l], norm_g[l, 2], w13, w2, l, 1, 6, S,
                 mixed=(o_a, o_b, o_c, o_d), w_out=w_o, gates=(gdn, rg))
    return h.reshape(B, S, D)
```
